```python
import math
import jax, jax.numpy as jnp
from jax import lax
import numpy as np

D_MODEL = 1024
BATCH = 2
SEQ = 16384
DEPTH = 4
DEC_BATCH = 16
DEC_SEQ = 16
PAST_LEN = 2048

CHUNK = 64
N_MIXERS = 4
PLE_DIM = 256
EPS = 1e-6
FOX_HEADS = 16
FOX_HD = 64
FOX_DIM = FOX_HEADS * FOX_HD
FOX_F_BIAS = 3.0
Q_BLOCK = 128
CONF_K = 31
SSD_INNER = 2 * D_MODEL
SSD_HD = 64
SSD_HEADS = SSD_INNER // SSD_HD
SSD_GROUPS = 4
SSD_HPG = SSD_HEADS // SSD_GROUPS
SSD_N = 128
SSD_CONV_K = 4
SSD_CONV_DIM = SSD_INNER + 2 * SSD_GROUPS * SSD_N
SSD_IN_DIM = SSD_INNER + SSD_CONV_DIM + SSD_HEADS
POOL_WINDOWS = (2, 4, 8, 16)
POOL_GROUP = D_MODEL // len(POOL_WINDOWS)
POOL_HIST = max(POOL_WINDOWS) - 1
D_FF = 2816
N_EXPERTS = 8
TOP_K = 2
D_FF_EXPERT = 1408
N_DENSE = (DEPTH + 1) // 2
N_MOE = DEPTH // 2

kernel_name = 'hybrid_streaming_encoder_step'

F32 = jnp.float32


def rmsnorm(x, g):
    xf = x.astype(F32)
    y = xf * lax.rsqrt(jnp.mean(xf * xf, axis=-1, keepdims=True) + EPS)
    return y.astype(x.dtype) * g


def layernorm(x, g, b):
    xf = x.astype(F32)
    mu = jnp.mean(xf, axis=-1, keepdims=True)
    var = jnp.mean(jnp.square(xf - mu), axis=-1, keepdims=True)
    return ((xf - mu) * lax.rsqrt(var + EPS)).astype(x.dtype) * g + b


def causal_dwconv(x, hist, w, b):
    xp = jnp.concatenate([hist.astype(x.dtype), x], axis=1)
    y = lax.conv_general_dilated(xp, w[:, None, :].astype(x.dtype), (1,), 'VALID',
                                 dimension_numbers=('NWC', 'WIO', 'NWC'),
                                 feature_group_count=x.shape[-1])
    return y + b, xp[:, xp.shape[1] - (w.shape[0] - 1):]


def fox_qkvf(xn, w_q, w_k, w_v, w_f, b_f):
    B, T, _ = xn.shape
    q = (xn @ w_q).reshape(B, T, FOX_HEADS, FOX_HD)
    k = (xn @ w_k).reshape(B, T, FOX_HEADS, FOX_HD)
    v = (xn @ w_v).reshape(B, T, FOX_HEADS, FOX_HD)
    logf = jax.nn.log_sigmoid((xn @ w_f + b_f).astype(F32))
    return q, k, v, logf


def fox_block(q, k, v, Fq, Fk, qpos, kpos):
    s = jnp.einsum('bqhd,bkhd->bhqk', q, k).astype(F32) * (FOX_HD ** -0.5)
    s = s + jnp.transpose(Fq, (0, 2, 1))[:, :, :, None] - jnp.transpose(Fk, (0, 2, 1))[:, :, None, :]
    mask = kpos[None, :] <= qpos[:, None]
    s = jnp.where(mask, s, -jnp.inf)
    p = jax.nn.softmax(s, axis=-1).astype(v.dtype)
    return jnp.einsum('bhqk,bkhd->bqhd', p, v)


def fox_prompt(xn, w_q, w_k, w_v, w_f, b_f, w_o):
    B, T, _ = xn.shape
    q, k, v, logf = fox_qkvf(xn, w_q, w_k, w_v, w_f, b_f)
    F = jnp.cumsum(logf, axis=1)
    nb = T // Q_BLOCK
    kpos = jnp.arange(T)
    qb = q.reshape(B, nb, Q_BLOCK, FOX_HEADS, FOX_HD).transpose(1, 0, 2, 3, 4)
    Fb = F.reshape(B, nb, Q_BLOCK, FOX_HEADS).transpose(1, 0, 2, 3)
    pb = kpos.reshape(nb, Q_BLOCK)
    ob = lax.map(lambda a: fox_block(a[0], k, v, a[1], F, a[2], kpos), (qb, Fb, pb))
    o = ob.transpose(1, 0, 2, 3, 4).reshape(B, T, FOX_DIM)
    return o @ w_o, (k, v, logf)


def fox_sample(xn, ck, cv, clf, w_q, w_k, w_v, w_f, b_f, w_o):
    B, S, _ = xn.shape
    P = ck.shape[1]
    q, k, v, logf = fox_qkvf(xn, w_q, w_k, w_v, w_f, b_f)
    k_all = jnp.concatenate([ck.astype(k.dtype), k], axis=1)
    v_all = jnp.concatenate([cv.astype(v.dtype), v], axis=1)
    F = jnp.cumsum(jnp.concatenate([clf.astype(F32), logf], axis=1), axis=1)
    kpos = jnp.arange(P + S)
    qpos = P + jnp.arange(S)
    o = fox_block(q, k_all, v_all, F[:, P:], F, qpos, kpos)
    return o.reshape(B, S, FOX_DIM) @ w_o, (k, v, logf)


def conformer_conv(xn, hist, w_pw1, b_pw1, w_dw, b_dw, g_ln, b_ln, w_pw2, b_pw2):
    a, gate = jnp.split(xn @ w_pw1 + b_pw1, 2, axis=-1)
    u = a * jax.nn.sigmoid(gate)
    u, new_hist = causal_dwconv(u, hist, w_dw, b_dw)
    u = jax.nn.silu(layernorm(u, g_ln, b_ln))
    return u @ w_pw2 + b_pw2, new_hist


def ssd_scan(x, dt, A, Bm, Cm, h0, L):
    b, T, G, E, P = x.shape
    N = Bm.shape[-1]
    c = T // L
    dtype = x.dtype
    x = x.astype(F32).reshape(b, c, L, G, E, P)
    dt = dt.reshape(b, c, L, G, E)
    Bm = Bm.astype(F32).reshape(b, c, L, G, N)
    Cm = Cm.astype(F32).reshape(b, c, L, G, N)
    cs = jnp.cumsum(dt * A, axis=2)
    causal = jnp.tril(jnp.ones((L, L), dtype=bool))
    seg = cs[:, :, :, None] - cs[:, :, None, :]
    decay = jnp.exp(jnp.where(causal[:, :, None, None], seg, -jnp.inf))
    cb = jnp.einsum('bclgn,bcsgn->bclsg', Cm, Bm)
    w = cb[..., None] * decay * dt[:, :, None]
    y_diag = jnp.einsum('bclsge,bcsgep->bclgep', w, x)
    to_end = jnp.exp(cs[:, :, -1:] - cs) * dt
    chunk_states = jnp.einsum('bclgn,bclge,bclgep->bcgepn', Bm, to_end, x)
    chunk_decay = jnp.exp(cs[:, :, -1])

    def step(h, inp):
        st, dec, c_c, cs_c = inp
        y_off = jnp.einsum('blgn,bgepn,blge->blgep', c_c, h, jnp.exp(cs_c))
        return h * dec[..., None, None] + st, y_off

    h, y_off = lax.scan(step, h0.astype(F32),
                        (chunk_states.swapaxes(0, 1), chunk_decay.swapaxes(0, 1),
                         Cm.swapaxes(0, 1), cs.swapaxes(0, 1)))
    y = y_diag + y_off.swapaxes(0, 1)
    return y.reshape(b, T, G, E, P).astype(dtype), h


def ssd_mixer(xn, conv_hist, h0, w_in, w_conv, b_conv, dt_bias, a_log, d_skip, g_norm, w_out):
    B, T, _ = xn.shape
    zxbcdt = xn @ w_in
    z = zxbcdt[..., :SSD_INNER]
    xbc = zxbcdt[..., SSD_INNER:SSD_INNER + SSD_CONV_DIM]
    dt = zxbcdt[..., SSD_INNER + SSD_CONV_DIM:]
    xbc, new_conv = causal_dwconv(xbc, conv_hist, w_conv, b_conv)
    xbc = jax.nn.silu(xbc)
    gn = SSD_GROUPS * SSD_N
    xs = xbc[..., :SSD_INNER].reshape(B, T, SSD_GROUPS, SSD_HPG, SSD_HD)
    Bm = xbc[..., SSD_INNER:SSD_INNER + gn].reshape(B, T, SSD_GROUPS, SSD_N)
    Cm = xbc[..., SSD_INNER + gn:].reshape(B, T, SSD_GROUPS, SSD_N)
    dt = jax.nn.softplus((dt + dt_bias).astype(F32)).reshape(B, T, SSD_GROUPS, SSD_HPG)
    A = -jnp.exp(a_log.astype(F32)).reshape(SSD_GROUPS, SSD_HPG)
    h0 = h0.reshape(B, SSD_GROUPS, SSD_HPG, SSD_HD, SSD_N)
    y, h = ssd_scan(xs, dt, A, Bm, Cm, h0, min(CHUNK, T))
    y = y + d_skip.reshape(SSD_GROUPS, SSD_HPG)[..., None] * xs
    y = y.reshape(B, T, SSD_INNER) * jax.nn.silu(z)
    y = rmsnorm(y.reshape(B, T, SSD_GROUPS, SSD_INNER // SSD_GROUPS),
                g_norm.reshape(SSD_GROUPS, SSD_INNER // SSD_GROUPS)).reshape(B, T, SSD_INNER)
    h = h.reshape(B, SSD_HEADS, SSD_HD, SSD_N).astype(xn.dtype)
    return y @ w_out, (new_conv, h)


def pool_mixer(xn, hist, pos0, w_pool, b_pool, scale):
    B, T, D = xn.shape
    xp = jnp.concatenate([hist.astype(xn.dtype), xn], axis=1)
    cs = jnp.cumsum(xp.astype(F32), axis=1)
    cs = jnp.concatenate([jnp.zeros((B, 1, D), F32), cs], axis=1)
    pos = pos0 + jnp.arange(T)
    xf = xn.astype(F32)
    outs = []
    for gi, wlen in enumerate(POOL_WINDOWS):
        lo, hi = gi * POOL_GROUP, (gi + 1) * POOL_GROUP
        s = cs[:, POOL_HIST + 1:, lo:hi] - cs[:, POOL_HIST + 1 - wlen:POOL_HIST + 1 - wlen + T, lo:hi]
        cnt = jnp.minimum(pos + 1, wlen).astype(F32)[None, :, None]
        outs.append(s / cnt - xf[..., lo:hi])
    d = jnp.stack(outs, axis=2).astype(xn.dtype)
    y = jnp.einsum('btgc,gcd->btgd', d, w_pool) + b_pool
    return y.reshape(B, T, D) * scale, xp[:, xp.shape[1] - POOL_HIST:]


def swiglu(x, wg, wu, wd):
    return (jax.nn.silu(x @ wg) * (x @ wu)) @ wd


def moe_swiglu(x, w_router, wg, wu, wd):
    logits = (x @ w_router).astype(F32)
    top_v, top_i = lax.top_k(logits, TOP_K)
    gates = jax.nn.softmax(top_v, axis=-1)
    comb = jnp.sum(jax.nn.one_hot(top_i, N_EXPERTS, dtype=F32) * gates[..., None], axis=-2).astype(x.dtype)
    y = jnp.zeros_like(x)
    for e in range(N_EXPERTS):
        y = y + comb[..., e:e + 1] * swiglu(x, wg[e], wu[e], wd[e])
    return y


def per_layer_embed(h, p_i, g, w_up, w_gate):
    gate = jax.nn.sigmoid(rmsnorm(h, g) @ w_gate)
    return h + gate * (p_i @ w_up)


def setup_inputs(seed: int = 0) -> dict:
    key = jax.random.key(seed)
    ks = iter(jax.random.split(key, 80))
    D = D_MODEL

    def nrm(shape, scale):
        return jax.random.normal(next(ks), shape, F32) * scale

    def gain(shape):
        return 1.0 + nrm(shape, 0.05)

    dt0 = jnp.exp(jax.random.uniform(next(ks), (SSD_HEADS,), F32, math.log(1e-3), math.log(1e-1)))
    inp = {}
    inp['x_prompt'] = nrm((BATCH, SEQ, D), 1.0)
    inp['x_sample'] = nrm((DEC_BATCH, DEC_SEQ, D), 1.0)
    inp['cache_fox_k'] = nrm((DEC_BATCH, PAST_LEN, FOX_HEADS, FOX_HD), 1.0)
    inp['cache_fox_v'] = nrm((DEC_BATCH, PAST_LEN, FOX_HEADS, FOX_HD), 1.0)
    inp['cache_fox_logf'] = jax.nn.log_sigmoid(FOX_F_BIAS + nrm((DEC_BATCH, PAST_LEN, FOX_HEADS), 0.5))
    inp['state_conf_conv'] = nrm((DEC_BATCH, CONF_K - 1, D), 0.5)
    inp['state_ssd_conv'] = nrm((DEC_BATCH, SSD_CONV_K - 1, SSD_CONV_DIM), 1.0)
    inp['state_ssd'] = nrm((DEC_BATCH, SSD_HEADS, SSD_HD, SSD_N), 0.1)
    inp['state_pool'] = nrm((DEC_BATCH, POOL_HIST, D), 1.0)
    inp['p_prompt'] = nrm((DEPTH, BATCH, SEQ, PLE_DIM), 1.0)
    inp['p_sample'] = nrm((DEPTH, DEC_BATCH, DEC_SEQ, PLE_DIM), 1.0)
    inp['g_mix'] = gain((DEPTH, D))
    inp['g_ffn'] = gain((DEPTH, D))
    inp['g_ple'] = gain((DEPTH, D))
    inp['g_final'] = gain((D,))
    inp['fox_w_q'] = nrm((D, FOX_DIM), D ** -0.5)
    inp['fox_w_k'] = nrm((D, FOX_DIM), D ** -0.5)
    inp['fox_w_v'] = nrm((D, FOX_DIM), D ** -0.5)
    inp['fox_w_f'] = nrm((D, FOX_HEADS), 0.5 * D ** -0.5)
    inp['fox_b_f'] = FOX_F_BIAS + nrm((FOX_HEADS,), 0.5)
    inp['fox_w_o'] = nrm((FOX_DIM, D), FOX_DIM ** -0.5)
    inp['conf_w_pw1'] = nrm((D, 2 * D), D ** -0.5)
    inp['conf_b_pw1'] = nrm((2 * D,), 0.02)
    inp['conf_w_dw'] = nrm((CONF_K, D), CONF_K ** -0.5)
    inp['conf_b_dw'] = nrm((D,), 0.02)
    inp['conf_g_ln'] = gain((D,))
    inp['conf_b_ln'] = nrm((D,), 0.02)
    inp['conf_w_pw2'] = nrm((D, D), D ** -0.5)
    inp['conf_b_pw2'] = nrm((D,), 0.02)
    inp['ssd_w_in'] = nrm((D, SSD_IN_DIM), D ** -0.5)
    inp['ssd_w_conv'] = nrm((SSD_CONV_K, SSD_CONV_DIM), SSD_CONV_K ** -0.5)
    inp['ssd_b_conv'] = nrm((SSD_CONV_DIM,), 0.02)
    inp['ssd_dt_bias'] = dt0 + jnp.log(-jnp.expm1(-dt0))
    inp['ssd_a_log'] = jnp.log(jax.random.uniform(next(ks), (SSD_HEADS,), F32, 1.0, 16.0))
    inp['ssd_d'] = 1.0 + nrm((SSD_HEADS,), 0.1)
    inp['ssd_g_norm'] = gain((SSD_INNER,))
    inp['ssd_w_out'] = nrm((SSD_INNER, D), SSD_INNER ** -0.5)
    inp['pool_w'] = nrm((len(POOL_WINDOWS), POOL_GROUP, POOL_GROUP), POOL_GROUP ** -0.5)
    inp['pool_b'] = nrm((len(POOL_WINDOWS), POOL_GROUP), 0.02)
    inp['pool_scale'] = 1.0 + nrm((D,), 0.1)
    inp['ffn_w_gate'] = nrm((N_DENSE, D, D_FF), D ** -0.5)
    inp['ffn_w_up'] = nrm((N_DENSE, D, D_FF), D ** -0.5)
    inp['ffn_w_down'] = nrm((N_DENSE, D_FF, D), D_FF ** -0.5)
    inp['moe_w_router'] = nrm((N_MOE, D, N_EXPERTS), D ** -0.5)
    inp['moe_w_gate'] = nrm((N_MOE, N_EXPERTS, D, D_FF_EXPERT), D ** -0.5)
    inp['moe_w_up'] = nrm((N_MOE, N_EXPERTS, D, D_FF_EXPERT), D ** -0.5)
    inp['moe_w_down'] = nrm((N_MOE, N_EXPERTS, D_FF_EXPERT, D), D_FF_EXPERT ** -0.5)
    inp['ple_w_up'] = nrm((DEPTH, PLE_DIM, D), 0.5 * PLE_DIM ** -0.5)
    inp['ple_w_gate'] = nrm((DEPTH, D, D), D ** -0.5)
    return inp


def reference(x_prompt, x_sample, cache_fox_k, cache_fox_v, cache_fox_logf, state_conf_conv,
              state_ssd_conv, state_ssd, state_pool, p_prompt, p_sample,
              g_mix, g_ffn, g_ple, g_final,
              fox_w_q, fox_w_k, fox_w_v, fox_w_f, fox_b_f, fox_w_o,
              conf_w_pw1, conf_b_pw1, conf_w_dw, conf_b_dw, conf_g_ln, conf_b_ln, conf_w_pw2, conf_b_pw2,
              ssd_w_in, ssd_w_conv, ssd_b_conv, ssd_dt_bias, ssd_a_log, ssd_d, ssd_g_norm, ssd_w_out,
              pool_w, pool_b, pool_scale,
              ffn_w_gate, ffn_w_up, ffn_w_down,
              moe_w_router, moe_w_gate, moe_w_up, moe_w_down,
              ple_w_up, ple_w_gate):
    B = x_prompt.shape[0]
    fox_w = (fox_w_q, fox_w_k, fox_w_v, fox_w_f, fox_b_f, fox_w_o)
    conf_w = (conf_w_pw1, conf_b_pw1, conf_w_dw, conf_b_dw, conf_g_ln, conf_b_ln, conf_w_pw2, conf_b_pw2)
    ssd_w = (ssd_w_in, ssd_w_conv, ssd_b_conv, ssd_dt_bias, ssd_a_log, ssd_d, ssd_g_norm, ssd_w_out)
    hp, hs = x_prompt, x_sample
    for i in range(DEPTH):
        m = i % N_MIXERS
        up = rmsnorm(hp, g_mix[i])
        us = rmsnorm(hs, g_mix[i])
        if m == 0:
            o_p, (fox_k_p, fox_v_p, fox_lf_p) = fox_prompt(up, *fox_w)
            o_s, (fox_k_s, fox_v_s, fox_lf_s) = fox_sample(us, cache_fox_k, cache_fox_v, cache_fox_logf, *fox_w)
        elif m == 1:
            o_p, conf_p = conformer_conv(up, jnp.zeros((B, CONF_K - 1, D_MODEL), up.dtype), *conf_w)
            o_s, conf_s = conformer_conv(us, state_conf_conv, *conf_w)
        elif m == 2:
            o_p, (ssdc_p, ssdh_p) = ssd_mixer(up, jnp.zeros((B, SSD_CONV_K - 1, SSD_CONV_DIM), up.dtype),
                                              jnp.zeros((B, SSD_HEADS, SSD_HD, SSD_N), up.dtype), *ssd_w)
            o_s, (ssdc_s, ssdh_s) = ssd_mixer(us, state_ssd_conv, state_ssd, *ssd_w)
        else:
            o_p, pool_p = pool_mixer(up, jnp.zeros((B, POOL_HIST, D_MODEL), up.dtype), 0,
                                     pool_w, pool_b, pool_scale)
            o_s, pool_s = pool_mixer(us, state_pool, PAST_LEN, pool_w, pool_b, pool_scale)
        hp = hp + o_p
        hs = hs + o_s
        up = rmsnorm(hp, g_ffn[i])
        us = rmsnorm(hs, g_ffn[i])
        j = i // 2
        if i % 2 == 0:
            hp = hp + swiglu(up, ffn_w_gate[j], ffn_w_up[j], ffn_w_down[j])
            hs = hs + swiglu(us, ffn_w_gate[j], ffn_w_up[j], ffn_w_down[j])
        else:
            hp = hp + moe_swiglu(up, moe_w_router[j], moe_w_gate[j], moe_w_up[j], moe_w_down[j])
            hs = hs + moe_swiglu(us, moe_w_router[j], moe_w_gate[j], moe_w_up[j], moe_w_down[j])
        hp = per_layer_embed(hp, p_prompt[i], g_ple[i], ple_w_up[i], ple_w_gate[i])
        hs = per_layer_embed(hs, p_sample[i], g_ple[i], ple_w_up[i], ple_w_gate[i])
    y_prompt = rmsnorm(hp, g_final)
    y_sample = rmsnorm(hs, g_final)
    return (y_prompt, y_sample, fox_k_p, fox_v_p, fox_lf_p, fox_k_s, fox_v_s, fox_lf_s,
            conf_p, conf_s, ssdc_p, ssdc_s, ssdh_p, ssdh_s, pool_p, pool_s)
```

```python
import functools

import jax
import jax.numpy as jnp
from jax import lax
from jax.experimental import pallas as pl
from jax.experimental.pallas import tpu as pltpu

F32 = jnp.float32
BF16 = jnp.bfloat16

D_MODEL = 1024
EPS = 1e-6
PLE_DIM = 256
FOX_HEADS = 16
FOX_HD = 64
FOX_PAIRS = FOX_HEADS // 2
CONF_K = 31
CONF_HALO = 32
SSD_INNER = 2048
SSD_HD = 64
SSD_HEADS = 32
SSD_GROUPS = 4
SSD_HPG = SSD_HEADS // SSD_GROUPS
SSD_N = 128
SSD_CONV_K = 4
SSD_CONV_DIM = SSD_INNER + 2 * SSD_GROUPS * SSD_N
SSD_HALO = 8
POOL_WINDOWS = (2, 4, 8, 16)
POOL_GROUP = D_MODEL // len(POOL_WINDOWS)
POOL_HIST = 15
POOL_HALO = 16
N_EXPERTS = 8
D_FF = 2816
D_FF_EXPERT = 1408
LANES = 128

VMEM_LIMIT_BYTES = 56 * 1024 * 1024
TOKEN_TILE = 512
ATTN_BLOCK = 512
SSD_CHUNK = 256
FFN_CHUNK = 256


def _cparams(*sem):
    return pltpu.CompilerParams(dimension_semantics=sem, vmem_limit_bytes=VMEM_LIMIT_BYTES)


def _resident(shape):
    n = len(shape)
    return pl.BlockSpec(shape, lambda *_: (0,) * n, pipeline_mode=pl.Buffered(1))


def _dot(a, b):
    return jnp.dot(a, b, preferred_element_type=F32)


def _dot_nt(a, b):
    return lax.dot_general(a, b, (((1,), (1,)), ((), ())), preferred_element_type=F32)


def _rms(x, g):
    return x * lax.rsqrt(jnp.mean(x * x, axis=-1, keepdims=True) + EPS) * g


def _sigmoid(x):
    return 1.0 / (1.0 + jnp.exp(-x))


def _silu(x):
    return x * _sigmoid(x)


def _softplus(x):
    return jnp.maximum(x, 0.0) + jnp.log1p(jnp.exp(-jnp.abs(x)))


def _split3(x):
    p1 = x.astype(BF16)
    r1 = x - p1.astype(F32)
    p2 = r1.astype(BF16)
    r2 = r1 - p2.astype(F32)
    return p1, p2, r2.astype(BF16)


def _dot3(a_f32, b_bf16):
    p1, p2, p3 = _split3(a_f32)
    return (_dot(p1, b_bf16) + _dot(p2, b_bf16)) + _dot(p3, b_bf16)


def _dot3_left(a_bf16, b_f32):
    p1, p2, p3 = _split3(b_f32)
    return (_dot(a_bf16, p1) + _dot(a_bf16, p2)) + _dot(a_bf16, p3)


def _rep_rows(a, n):
    s, w = a.shape
    return jnp.broadcast_to(a[:, None, :], (s, n, w)).reshape(s * n, w)


def _tile_rows(a, n):
    h, w = a.shape
    return jnp.broadcast_to(a[None], (n, h, w)).reshape(n * h, w)


def _tril(n):
    r = lax.broadcasted_iota(jnp.int32, (n, n), 0)
    c = lax.broadcasted_iota(jnp.int32, (n, n), 1)
    return jnp.where(c <= r, 1.0, 0.0).astype(BF16)


def _ple_apply(h1, p, g_ple, w_gate, w_up):
    hn = _rms(h1, g_ple).astype(BF16)
    gate = _sigmoid(_dot(hn, w_gate))
    return h1 + gate * _dot(p.astype(BF16), w_up)


def _ffn_ple_kernel(h_ref, p_ref, gf_ref, wg_ref, wu_ref, wd_ref, gp_ref, wpg_ref, wpu_ref, o_ref):
    x = h_ref[...]
    xn = _rms(x, gf_ref[...]).astype(BF16)
    acc = x
    for c in range(D_FF // FFN_CHUNK):
        sl = slice(c * FFN_CHUNK, (c + 1) * FFN_CHUNK)
        hh = (_silu(_dot(xn, wg_ref[:, sl])) * _dot(xn, wu_ref[:, sl])).astype(BF16)
        acc = acc + _dot(hh, wd_ref[sl, :])
    o_ref[...] = _ple_apply(acc, p_ref[...], gp_ref[...], wpg_ref[...], wpu_ref[...])


def ffn_ple(h, p, g_ffn, wg, wu, wd, g_ple, wpg, wpu):
    m = h.shape[0]
    tm = min(TOKEN_TILE, m)
    row = lambda w: pl.BlockSpec((tm, w), lambda i: (i, 0))
    return pl.pallas_call(
        _ffn_ple_kernel,
        grid=(m // tm,),
        in_specs=[row(D_MODEL), row(PLE_DIM), _resident((1, D_MODEL)),
                  _resident((D_MODEL, D_FF)), _resident((D_MODEL, D_FF)), _resident((D_FF, D_MODEL)),
                  _resident((1, D_MODEL)), _resident((D_MODEL, D_MODEL)), _resident((PLE_DIM, D_MODEL))],
        out_specs=row(D_MODEL),
        out_shape=jax.ShapeDtypeStruct((m, D_MODEL), F32),
        compiler_params=_cparams("parallel"),
        name="ffn_ple",
    )(h, p, g_ffn, wg, wu, wd, g_ple, wpg, wpu)


def _moe_ple_kernel(h_ref, p_ref, gf_ref, wr_ref, wg_ref, wu_ref, wd_ref, gp_ref, wpg_ref, wpu_ref, gfin_ref,
                    o_ref, xn_s, comb_s, acc_s, *, final_norm):
    e = pl.program_id(1)

    @pl.when(e == 0)
    def _():
        x = h_ref[...]
        xn = _rms(x, gf_ref[...]).astype(BF16)
        xn_s[...] = xn
        acc_s[...] = x
        logits = _dot(xn, wr_ref[...])
        lane = lax.broadcasted_iota(jnp.int32, logits.shape, 1).astype(F32)
        neg = jnp.float32(-jnp.inf)
        lg = jnp.where(lane < N_EXPERTS, logits, neg)
        m1 = jnp.max(lg, axis=-1, keepdims=True)
        i1 = jnp.min(jnp.where(lg == m1, lane, float(LANES)), axis=-1, keepdims=True)
        lg2 = jnp.where(lane == i1, neg, lg)
        m2 = jnp.max(lg2, axis=-1, keepdims=True)
        i2 = jnp.min(jnp.where(lg2 == m2, lane, float(LANES)), axis=-1, keepdims=True)
        e2 = jnp.exp(m2 - m1)
        den = 1.0 + e2
        comb_s[...] = jnp.where(lane == i1, 1.0 / den, jnp.where(lane == i2, e2 / den, 0.0))

    xn = xn_s[...]
    lane = lax.broadcasted_iota(jnp.int32, comb_s.shape, 1)
    ce = jnp.sum(jnp.where(lane == e, comb_s[...], 0.0), axis=-1, keepdims=True)
    hh = (_silu(_dot(xn, wg_ref[0])) * _dot(xn, wu_ref[0])).astype(BF16)
    acc_s[...] += ce * _dot(hh, wd_ref[0])

    @pl.when(e == N_EXPERTS - 1)
    def _():
        h2 = _ple_apply(acc_s[...], p_ref[...], gp_ref[...], wpg_ref[...], wpu_ref[...])
        if final_norm:
            h2 = _rms(h2, gfin_ref[...])
        o_ref[...] = h2


def moe_ple(h, p, g_ffn, wr, wg, wu, wd, g_ple, wpg, wpu, g_final, final_norm):
    m = h.shape[0]
    tm = min(TOKEN_TILE, m)
    row = lambda w: pl.BlockSpec((tm, w), lambda i, e: (i, 0))
    expert = lambda a, b: pl.BlockSpec((1, a, b), lambda i, e: (e, 0, 0))
    return pl.pallas_call(
        functools.partial(_moe_ple_kernel, final_norm=final_norm),
        grid=(m // tm, N_EXPERTS),
        in_specs=[row(D_MODEL), row(PLE_DIM), _resident((1, D_MODEL)), _resident((D_MODEL, LANES)),
                  expert(D_MODEL, D_FF_EXPERT), expert(D_MODEL, D_FF_EXPERT), expert(D_FF_EXPERT, D_MODEL),
                  _resident((1, D_MODEL)), _resident((D_MODEL, D_MODEL)), _resident((PLE_DIM, D_MODEL)),
                  _resident((1, D_MODEL))],
        out_specs=row(D_MODEL),
        out_shape=jax.ShapeDtypeStruct((m, D_MODEL), F32),
        scratch_shapes=[pltpu.VMEM((tm, D_MODEL), BF16), pltpu.VMEM((tm, LANES), F32),
                        pltpu.VMEM((tm, D_MODEL), F32)],
        compiler_params=_cparams("parallel", "arbitrary"),
        name="moe_ple",
    )(h, p, g_ffn, wr, wg, wu, wd, g_ple, wpg, wpu, g_final)


def _log_sigmoid(z):
    return jnp.minimum(z, 0.0) - jnp.log1p(jnp.exp(-jnp.abs(z)))


def _fox_proj_kernel(x_ref, g_ref, wqt_ref, wk_ref, wv_ref, wf_ref, bf_ref, sel_ref, ones_ref,
                     k_out, v_out, lf_out, qt_out, vt_out, kaug_out, ft_out, carry):
    tm = x_ref.shape[1]

    @pl.when(pl.program_id(1) == 0)
    def _():
        carry[...] = jnp.zeros_like(carry)

    xn = _rms(x_ref[0], g_ref[...])
    xb = xn.astype(BF16)
    xnt = xn.T.astype(BF16)
    k = _dot(xb, wk_ref[...])
    v = _dot(xb, wv_ref[...])
    k_out[0] = k
    v_out[0] = v
    qt_out[0] = (_dot(wqt_ref[...], xnt) * (FOX_HD ** -0.5)).astype(BF16)
    vt_out[0, 0] = v.T.astype(BF16)
    lf = _log_sigmoid(_dot(xb, wf_ref[...]) + bf_ref[...])
    lf_out[0] = lf[:, :FOX_HEADS]
    f = _dot3_left(_tril(tm), lf) + carry[...]
    carry[...] = f[tm - 1:tm, :]
    ft_out[0] = f.T[:FOX_HEADS, :]
    p1, p2, p3 = _split3(f)
    aug = ((_dot(p1, sel_ref[0]) + _dot(p2, sel_ref[1])) + _dot(p3, sel_ref[2]) + ones_ref[...]).astype(BF16)
    kb = k.astype(BF16)
    for p in range(FOX_PAIRS):
        kaug_out[0, :, 2 * LANES * p:2 * LANES * p + LANES] = kb[:, LANES * p:LANES * (p + 1)]
        kaug_out[0, :, 2 * LANES * p + LANES:2 * LANES * (p + 1)] = aug[:, LANES * p:LANES * (p + 1)]


def _fox_aug_constants():
    h = jnp.arange(FOX_HEADS)
    sel = jnp.zeros((3, LANES, FOX_PAIRS * LANES), F32)
    for r in range(3):
        sel = sel.at[r, h, LANES * (h // 2) + 16 * (h % 2) + r].set(1.0)
    p = jnp.arange(FOX_PAIRS)
    ones = jnp.zeros((1, FOX_PAIRS * LANES), F32)
    for r in range(3):
        ones = ones.at[0, LANES * p + 32 + r].set(1.0)
    return sel.astype(BF16), ones


def fox_proj(x, g, wq, wk, wv, wf, bf):
    b, t, d = x.shape
    tm = min(ATTN_BLOCK, t)
    sel, ones = _fox_aug_constants()
    wf_pad = jnp.zeros((d, LANES), F32).at[:, :FOX_HEADS].set(wf).astype(BF16)
    bf_pad = jnp.zeros((1, LANES), F32).at[0, :FOX_HEADS].set(bf)
    tile = lambda w: pl.BlockSpec((1, tm, w), lambda bi, ti: (bi, ti, 0))
    return pl.pallas_call(
        _fox_proj_kernel,
        grid=(b, t // tm),
        in_specs=[tile(d), _resident((1, d)), _resident((d, d)), _resident((d, d)), _resident((d, d)),
                  _resident((d, LANES)), _resident((1, LANES)), _resident((3, LANES, FOX_PAIRS * LANES)),
                  _resident((1, FOX_PAIRS * LANES))],
        out_specs=[tile(d), tile(d), tile(FOX_HEADS),
                   pl.BlockSpec((1, d, tm), lambda bi, ti: (bi, 0, ti)),
                   pl.BlockSpec((1, 1, d, tm), lambda bi, ti: (bi, ti, 0, 0)),
                   tile(2 * d),
                   pl.BlockSpec((1, FOX_HEADS, tm), lambda bi, ti: (bi, 0, ti))],
        out_shape=[jax.ShapeDtypeStruct((b, t, d), F32), jax.ShapeDtypeStruct((b, t, d), F32),
                   jax.ShapeDtypeStruct((b, t, FOX_HEADS), F32),
                   jax.ShapeDtypeStruct((b, d, t), BF16),
                   jax.ShapeDtypeStruct((b, t // tm, d, tm), BF16),
                   jax.ShapeDtypeStruct((b, t, 2 * d), BF16),
                   jax.ShapeDtypeStruct((b, FOX_HEADS, t), F32)],
        scratch_shapes=[pltpu.VMEM((1, LANES), F32)],
        compiler_params=_cparams("arbitrary", "arbitrary"),
        name="fox_proj",
    )(x, g, wq.T.astype(BF16), wk.astype(BF16), wv.astype(BF16), wf_pad, bf_pad, sel, ones)


def _fox_attn_kernel(qt_ref, kaug_ref, vt_ref, ft_ref, ot_ref, qaug, m_s, l_s, acc):
    tq = qt_ref.shape[2]
    tk = vt_ref.shape[3]
    i = pl.program_id(2)
    zeros = lambda r: jnp.zeros((r, tq), BF16)
    minus = jnp.full((16, tq), -1.0, BF16)
    row = lax.broadcasted_iota(jnp.int32, (16, tq), 0)
    for h in range(2):
        qaug[h, 64 * h:64 * (h + 1)] = qt_ref[0, 64 * h:64 * (h + 1), :]
        qaug[h, 64 * (1 - h):64 * (2 - h)] = zeros(64)
        qaug[h, LANES + 16 * h:LANES + 16 * (h + 1)] = minus
        qaug[h, LANES + 16 * (1 - h):LANES + 16 * (2 - h)] = zeros(16)
        p1, p2, p3 = _split3(ft_ref[0, 0, h:h + 1, :])
        blk = jnp.where(row == 0, p1.astype(F32),
                        jnp.where(row == 1, p2.astype(F32), jnp.where(row == 2, p3.astype(F32), 0.0)))
        qaug[h, LANES + 32:LANES + 48] = blk.astype(BF16)
        qaug[h, LANES + 48:2 * LANES] = zeros(LANES - 48)
    m_s[...] = jnp.full(m_s.shape, -jnp.inf, F32)
    l_s[...] = jnp.zeros(l_s.shape, F32)
    acc[...] = jnp.zeros(acc.shape, F32)

    def block(j, masked):
        kj = kaug_ref[0, pl.ds(pl.multiple_of(j * tk, tk), tk), :]
        for h in range(2):
            st = _dot(kj, qaug[h])
            if masked:
                kpos = lax.broadcasted_iota(jnp.int32, st.shape, 0)
                qpos = lax.broadcasted_iota(jnp.int32, st.shape, 1)
                st = jnp.where(kpos <= qpos, st, -jnp.inf)
            m_prev = m_s[h]
            m_new = jnp.maximum(m_prev, jnp.max(st, axis=0, keepdims=True))
            alpha = jnp.exp(m_prev - m_new)
            p = jnp.exp(st - m_new)
            l_s[h] = alpha * l_s[h] + jnp.sum(p, axis=0, keepdims=True)
            acc[h] = alpha * acc[h] + _dot(vt_ref[0, j, 64 * h:64 * (h + 1), :], p.astype(BF16))
            m_s[h] = m_new

    def body(j, carry):
        block(j, False)
        return carry

    lax.fori_loop(0, i, body, 0)
    block(i, True)
    for h in range(2):
        ot_ref[0, 64 * h:64 * (h + 1), :] = (acc[h] / l_s[h]).astype(BF16)


def fox_attention(qt, kaug, vt, ft):
    b, d, t = qt.shape
    blk = vt.shape[3]
    ft = ft.reshape(b, FOX_PAIRS, 2, t)
    return pl.pallas_call(
        _fox_attn_kernel,
        grid=(b, FOX_PAIRS, t // blk),
        in_specs=[pl.BlockSpec((1, LANES, blk), lambda bi, p, i: (bi, p, i)),
                  pl.BlockSpec((1, t, 2 * LANES), lambda bi, p, i: (bi, 0, p)),
                  pl.BlockSpec((1, t // blk, LANES, blk), lambda bi, p, i: (bi, 0, p, 0)),
                  pl.BlockSpec((1, 1, 2, blk), lambda bi, p, i: (bi, p, 0, i))],
        out_specs=pl.BlockSpec((1, LANES, blk), lambda bi, p, i: (bi, p, i)),
        out_shape=jax.ShapeDtypeStruct((b, d, t), BF16),
        scratch_shapes=[pltpu.VMEM((2, 2 * LANES, blk), BF16), pltpu.VMEM((2, 1, blk), F32),
                        pltpu.VMEM((2, 1, blk), F32), pltpu.VMEM((2, FOX_HD, blk), F32)],
        compiler_params=_cparams("parallel", "parallel", "arbitrary"),
        name="fox_attention",
    )(qt, kaug, vt, ft)


def _out_proj_t_kernel(h_ref, ot_ref, w_ref, o_ref):
    o_ref[0] = h_ref[0] + pl.dot(ot_ref[0], w_ref[...], trans_a=True)


def out_proj_t(h, ot, w):
    b, t, d = h.shape
    tm = min(TOKEN_TILE, t)
    tile = pl.BlockSpec((1, tm, d), lambda bi, ti: (bi, ti, 0))
    return pl.pallas_call(
        _out_proj_t_kernel,
        grid=(b, t // tm),
        in_specs=[tile, pl.BlockSpec((1, d, tm), lambda bi, ti: (bi, 0, ti)), _resident((d, d))],
        out_specs=tile,
        out_shape=jax.ShapeDtypeStruct((b, t, d), F32),
        compiler_params=_cparams("parallel", "parallel"),
        name="fox_out_proj",
    )(h, ot, w.astype(BF16))


def _fox_sample_kernel(x_ref, g_ref, wq_ref, wk_ref, wv_ref, wf_ref, bf_ref, wo_ref, ck_ref, cv_ref, clf_ref,
                       h_out, k_out, v_out, lf_out):
    s = x_ref.shape[1]
    pl_len = ck_ref.shape[1]
    x = x_ref[0]
    xb = _rms(x, g_ref[...]).astype(BF16)
    q = _dot(xb, wq_ref[...]) * (FOX_HD ** -0.5)
    k = _dot(xb, wk_ref[...])
    v = _dot(xb, wv_ref[...])
    lf = _log_sigmoid(_dot(xb, wf_ref[...]) + bf_ref[...])
    k_out[0] = k
    v_out[0] = v
    lf_out[0] = lf[:, :FOX_HEADS]

    rows = s * FOX_HEADS
    rh = lax.broadcasted_iota(jnp.int32, (rows, D_MODEL), 0) % FOX_HEADS
    lh = lax.broadcasted_iota(jnp.int32, (rows, D_MODEL), 1) // FOX_HD
    head_lanes = rh == lh
    qbd = jnp.where(head_lanes, _rep_rows(q, FOX_HEADS), 0.0).astype(BF16)

    cblk = 256
    carry = jnp.zeros((1, LANES), F32)
    tri = _tril(cblk)
    f_parts = []
    for c in range(pl_len // cblk):
        fc = _dot3_left(tri, clf_ref[0, c * cblk:(c + 1) * cblk, :]) + carry
        carry = fc[cblk - 1:cblk, :]
        f_parts.append(fc)
    f_cache = jnp.concatenate(f_parts, axis=0)
    pad_rows = lambda a: jnp.concatenate([a, jnp.zeros((LANES - s, a.shape[1]), a.dtype)], axis=0)
    f_new = _dot3_left(_tril(LANES), pad_rows(lf)) + carry

    lane = lax.broadcasted_iota(jnp.int32, (rows, LANES), 1)
    row_head = lax.broadcasted_iota(jnp.int32, (rows, LANES), 0) % FOX_HEADS
    fq = jnp.sum(jnp.where(lane == row_head, _rep_rows(f_new[:s], FOX_HEADS), 0.0), axis=-1, keepdims=True)
    fk_cache = _tile_rows(f_cache.T[:FOX_HEADS, :], s)
    fk_new = _tile_rows(f_new.T[:FOX_HEADS, :], s)

    kc = 512
    qk = [_dot_nt(qbd, ck_ref[0, c * kc:(c + 1) * kc, :].astype(BF16)) for c in range(pl_len // kc)]
    s_cache = jnp.concatenate(qk, axis=1) + fq - fk_cache
    s_new = _dot_nt(qbd, pad_rows(k).astype(BF16)) + fq - fk_new
    qi = lax.broadcasted_iota(jnp.int32, (rows, LANES), 0) // FOX_HEADS
    kj = lax.broadcasted_iota(jnp.int32, (rows, LANES), 1)
    s_new = jnp.where(kj <= qi, s_new, -jnp.inf)
    m = jnp.maximum(jnp.max(s_cache, axis=-1, keepdims=True), jnp.max(s_new, axis=-1, keepdims=True))
    p_cache = jnp.exp(s_cache - m)
    p_new = jnp.exp(s_new - m)
    den = jnp.sum(p_cache, axis=-1, keepdims=True) + jnp.sum(p_new, axis=-1, keepdims=True)
    pv = _dot(p_new.astype(BF16), pad_rows(v).astype(BF16))
    pb = p_cache.astype(BF16)
    for c in range(pl_len // kc):
        pv = pv + _dot(pb[:, c * kc:(c + 1) * kc], cv_ref[0, c * kc:(c + 1) * kc, :].astype(BF16))
    o_heads = jnp.where(head_lanes, pv / den, 0.0).astype(BF16)
    pick = (lax.broadcasted_iota(jnp.int32, (s, rows), 1) // FOX_HEADS
            == lax.broadcasted_iota(jnp.int32, (s, rows), 0))
    o = _dot(jnp.where(pick, 1.0, 0.0).astype(BF16), o_heads)
    h_out[0] = x + _dot(o.astype(BF16), wo_ref[...])


def fox_sample(x, g, wq, wk, wv, wf, bf, wo, ck, cv, clf):
    b, s, d = x.shape
    p = ck.shape[1]
    wf_pad = jnp.zeros((d, LANES), F32).at[:, :FOX_HEADS].set(wf).astype(BF16)
    bf_pad = jnp.zeros((1, LANES), F32).at[0, :FOX_HEADS].set(bf)
    clf_pad = jnp.zeros((b, p, LANES), F32).at[:, :, :FOX_HEADS].set(clf)
    seq = lambda n, w: pl.BlockSpec((1, n, w), lambda bi: (bi, 0, 0))
    return pl.pallas_call(
        _fox_sample_kernel,
        grid=(b,),
        in_specs=[seq(s, d), _resident((1, d)), _resident((d, d)), _resident((d, d)), _resident((d, d)),
                  _resident((d, LANES)), _resident((1, LANES)), _resident((d, d)),
                  seq(p, d), seq(p, d), seq(p, LANES)],
        out_specs=[seq(s, d), seq(s, d), seq(s, d), seq(s, FOX_HEADS)],
        out_shape=[jax.ShapeDtypeStruct((b, s, d), F32)] * 3 + [jax.ShapeDtypeStruct((b, s, FOX_HEADS), F32)],
        compiler_params=_cparams("parallel"),
        name="fox_sample",
    )(x, g, wq.astype(BF16), wk.astype(BF16), wv.astype(BF16), wf_pad, bf_pad, wo.astype(BF16),
      ck.reshape(b, p, d), cv.reshape(b, p, d), clf_pad)


def _halo_specs(b, t, tm, d, halo):
    main = pl.BlockSpec((1, tm, d), lambda bi, ti: (bi, ti, 0))
    if t == tm:
        return [main]
    per = tm // halo
    return [main, pl.BlockSpec((1, halo, d), lambda bi, ti: (bi, jnp.maximum(ti * per - 1, 0), 0))]


def _conf_kernel(*refs, has_halo):
    if has_halo:
        x_ref, xh_ref, *refs = refs
    else:
        x_ref, *refs = refs
    (hist_ref, g_ref, w1_ref, b1_ref, wdw_ref, bdw_ref, gln_ref, bln_ref, w2_ref, b2_ref,
     o_ref, hs_ref, uext) = refs
    tm = x_ref.shape[1]

    def glu(rows):
        ag = _dot(_rms(rows, g_ref[...]).astype(BF16), w1_ref[...]) + b1_ref[...]
        return ag[:, :D_MODEL] * _sigmoid(ag[:, D_MODEL:])

    x = x_ref[0]
    if has_halo:
        prev = jnp.where(pl.program_id(1) == 0, hist_ref[0], glu(xh_ref[0]))
    else:
        prev = hist_ref[0]
    uext[0:CONF_HALO] = prev
    uext[CONF_HALO:] = glu(x)
    off = CONF_HALO - (CONF_K - 1)
    y = jnp.broadcast_to(bdw_ref[...], (tm, D_MODEL))
    for k in range(CONF_K):
        y = y + wdw_ref[k:k + 1, :] * uext[off + k:off + k + tm, :]
    hs_ref[0] = uext[tm + off:tm + CONF_HALO, :]
    mu = jnp.mean(y, axis=-1, keepdims=True)
    yc = y - mu
    var = jnp.mean(yc * yc, axis=-1, keepdims=True)
    v = _silu(yc * lax.rsqrt(var + EPS) * gln_ref[...] + bln_ref[...])
    o_ref[0] = x + _dot(v.astype(BF16), w2_ref[...]) + b2_ref[...]


def conformer(x, hist, g, w1, b1, wdw, bdw, gln, bln, w2, b2):
    b, t, d = x.shape
    tm = min(TOKEN_TILE, t)
    xs = _halo_specs(b, t, tm, d, CONF_HALO)
    hist_pad = jnp.pad(hist, ((0, 0), (CONF_HALO - (CONF_K - 1), 0), (0, 0)))
    per_b = lambda n: pl.BlockSpec((1, n, d), lambda bi, ti: (bi, 0, 0))
    return pl.pallas_call(
        functools.partial(_conf_kernel, has_halo=len(xs) == 2),
        grid=(b, t // tm),
        in_specs=xs + [per_b(CONF_HALO), _resident((1, d)), _resident((d, 2 * d)), _resident((1, 2 * d)),
                       _resident((CONF_K, d)), _resident((1, d)), _resident((1, d)), _resident((1, d)),
                       _resident((d, d)), _resident((1, d))],
        out_specs=[pl.BlockSpec((1, tm, d), lambda bi, ti: (bi, ti, 0)), per_b(CONF_K - 1)],
        out_shape=[jax.ShapeDtypeStruct((b, t, d), F32), jax.ShapeDtypeStruct((b, CONF_K - 1, d), F32)],
        scratch_shapes=[pltpu.VMEM((CONF_HALO + tm, d), F32)],
        compiler_params=_cparams("parallel", "arbitrary"),
        name="conformer",
    )(*([x] * len(xs)), hist_pad, g, w1.astype(BF16), b1.reshape(1, -1), wdw, bdw.reshape(1, -1),
      gln.reshape(1, -1), bln.reshape(1, -1), w2.astype(BF16), b2.reshape(1, -1))


def _pool_kernel(*refs, has_halo, pos0):
    if has_halo:
        x_ref, xh_ref, *refs = refs
    else:
        x_ref, *refs = refs
    hist_ref, g_ref, w_ref, b_ref, scale_ref, o_ref, hs_ref, ext = refs
    tm = x_ref.shape[1]
    i = pl.program_id(1)
    x = x_ref[0]
    xn = _rms(x, g_ref[...])
    if has_halo:
        prev = jnp.where(i == 0, hist_ref[0], _rms(xh_ref[0], g_ref[...]))
    else:
        prev = hist_ref[0]
    ext[0:POOL_HALO] = prev
    ext[POOL_HALO:] = xn
    hs_ref[0] = ext[tm + POOL_HALO - POOL_HIST:tm + POOL_HALO, :]
    pos = pos0 + i * tm + lax.broadcasted_iota(jnp.int32, (tm, 1), 0)
    ys = []
    for gi, wlen in enumerate(POOL_WINDOWS):
        lanes = slice(gi * POOL_GROUP, (gi + 1) * POOL_GROUP)
        s = ext[POOL_HALO:POOL_HALO + tm, lanes]
        for k in range(1, wlen):
            s = s + ext[POOL_HALO - k:POOL_HALO - k + tm, lanes]
        cnt = jnp.minimum(pos + 1, wlen).astype(F32)
        dg = s / cnt - xn[:, lanes]
        ys.append(_dot(dg.astype(BF16), w_ref[gi]))
    y = (jnp.concatenate(ys, axis=1) + b_ref[...]) * scale_ref[...]
    o_ref[0] = x + y


def pool_mixer(x, hist, pos0, g, w, bias, scale):
    b, t, d = x.shape
    tm = min(TOKEN_TILE, t)
    xs = _halo_specs(b, t, tm, d, POOL_HALO)
    hist_pad = jnp.pad(hist, ((0, 0), (POOL_HALO - POOL_HIST, 0), (0, 0)))
    per_b = lambda n: pl.BlockSpec((1, n, d), lambda bi, ti: (bi, 0, 0))
    ng = len(POOL_WINDOWS)
    return pl.pallas_call(
        functools.partial(_pool_kernel, has_halo=len(xs) == 2, pos0=pos0),
        grid=(b, t // tm),
        in_specs=xs + [per_b(POOL_HALO), _resident((1, d)), _resident((ng, POOL_GROUP, POOL_GROUP)),
                       _resident((1, d)), _resident((1, d))],
        out_specs=[pl.BlockSpec((1, tm, d), lambda bi, ti: (bi, ti, 0)), per_b(POOL_HIST)],
        out_shape=[jax.ShapeDtypeStruct((b, t, d), F32), jax.ShapeDtypeStruct((b, POOL_HIST, d), F32)],
        scratch_shapes=[pltpu.VMEM((POOL_HALO + tm, d), F32)],
        compiler_params=_cparams("parallel", "arbitrary"),
        name="pool_mixer",
    )(*([x] * len(xs)), hist_pad, g, w.astype(BF16), bias.reshape(1, -1), scale.reshape(1, -1))


def _ssd_kernel(*refs, has_halo, valid):
    if has_halo:
        x_ref, xh_ref, *refs = refs
    else:
        x_ref, *refs = refs
    (chist_ref, h0_ref, g_ref, wz_ref, wxbc_ref, wdt_ref, wconv_ref, bconv_ref, dtb_ref, a_ref, dskip_ref,
     gn_ref, wout_ref, exp_ref, expt_ref, o_ref, cs_out_ref, h_out_ref, ext, hstate) = refs
    L = x_ref.shape[1]
    i = pl.program_id(1)
    gw = SSD_HPG * SSD_HD
    gn = SSD_GROUPS * SSD_N

    @pl.when(i == 0)
    def _():
        hstate[...] = h0_ref[0]

    x = x_ref[0]
    xb = _rms(x, g_ref[...]).astype(BF16)
    z = _dot(xb, wz_ref[...])
    dtr = _dot(xb, wdt_ref[...])
    if has_halo:
        prev = jnp.where(i == 0, chist_ref[0], _dot(_rms(xh_ref[0], g_ref[...]).astype(BF16), wxbc_ref[...]))
    else:
        prev = chist_ref[0]
    ext[0:SSD_HALO] = prev
    ext[SSD_HALO:] = _dot(xb, wxbc_ref[...])
    off = SSD_HALO - (SSD_CONV_K - 1)
    y = jnp.broadcast_to(bconv_ref[...], (L, SSD_CONV_DIM))
    for k in range(SSD_CONV_K):
        y = y + wconv_ref[k:k + 1, :] * ext[off + k:off + k + L, :]
    cs_out_ref[0] = ext[valid + off:valid + SSD_HALO, :]
    xbc = _silu(y)
    xs = xbc[:, :SSD_INNER]
    bm = xbc[:, SSD_INNER:SSD_INNER + gn].astype(BF16)
    cm = xbc[:, SSD_INNER + gn:].astype(BF16)

    head_lane = lax.broadcasted_iota(jnp.int32, (L, LANES), 1) < SSD_HEADS
    live = head_lane
    if valid < L:
        live = live & (lax.broadcasted_iota(jnp.int32, (L, LANES), 0) < valid)
    dt = jnp.where(live, _softplus(dtr + dtb_ref[...]), 0.0)
    cs = _dot3_left(_tril(L), dt * a_ref[...])
    cst = cs.T
    dtt = dt.T
    to_end = jnp.exp(cs[L - 1:L, :] - cs) * dt
    td_x = _dot3(to_end, exp_ref[...])
    ecs_x = _dot3(jnp.exp(cs), exp_ref[...])
    chunk_decay = jnp.broadcast_to(jnp.exp(cst[:, L - 1:L]), (LANES, LANES))
    xs_b = xs.astype(BF16)
    xtd = xs * td_x
    r = lax.broadcasted_iota(jnp.int32, (L, L), 0)
    c = lax.broadcasted_iota(jnp.int32, (L, L), 1)
    causal = c <= r
    low_half = lax.broadcasted_iota(jnp.int32, (L, 2 * SSD_HD), 1) < SSD_HD

    y_groups = []
    for g in range(SSD_GROUPS):
        cg = cm[:, g * SSD_N:(g + 1) * SSD_N]
        bg = bm[:, g * SSD_N:(g + 1) * SSD_N]
        cb = _dot_nt(cg, bg)
        hg = hstate[g]
        y_off = _dot_nt(cg, hg.astype(BF16)) * ecs_x[:, g * gw:(g + 1) * gw]
        pairs = []
        for pr in range(SSD_HPG // 2):
            lo = g * gw + pr * 2 * SSD_HD
            xpair = xs_b[:, lo:lo + 2 * SSD_HD]
            acc = None
            for hh in range(2):
                e = g * SSD_HPG + 2 * pr + hh
                seg = cs[:, e:e + 1] - cst[e:e + 1, :]
                w = cb * jnp.exp(jnp.where(causal, seg, -jnp.inf)) * dtt[e:e + 1, :]
                xm = jnp.where(low_half == (hh == 0), xpair, jnp.zeros_like(xpair))
                term = _dot(w.astype(BF16), xm)
                acc = term if acc is None else acc + term
            pairs.append(acc)
        y_groups.append(jnp.concatenate(pairs, axis=1) + y_off)
        s_g = _dot(xtd[:, g * gw:(g + 1) * gw].T.astype(BF16), bg)
        hstate[g] = hg * _dot3_left(expt_ref[g], chunk_decay) + s_g

    ys = (jnp.concatenate(y_groups, axis=1) + dskip_ref[...] * xs) * _silu(z)
    normed = []
    for g in range(SSD_GROUPS):
        yg = ys[:, g * gw:(g + 1) * gw]
        normed.append(yg * lax.rsqrt(jnp.mean(yg * yg, axis=-1, keepdims=True) + EPS))
    yn = jnp.concatenate(normed, axis=1) * gn_ref[...]
    o_ref[0] = x + _dot(yn.astype(BF16), wout_ref[...])

    @pl.when(i == pl.num_programs(1) - 1)
    def _():
        h_out_ref[0] = hstate[...]


def _ssd_expand_constants():
    e = jnp.arange(SSD_HEADS)
    ch = jnp.arange(SSD_INNER)
    exp = jnp.zeros((LANES, SSD_INNER), F32).at[ch // SSD_HD, ch].set(1.0)
    gw = SSD_HPG * SSD_HD
    rows = jnp.arange(gw)
    expt = jnp.zeros((SSD_GROUPS, gw, LANES), F32)
    for g in range(SSD_GROUPS):
        expt = expt.at[g, rows, g * SSD_HPG + rows // SSD_HD].set(1.0)
    del e
    return exp.astype(BF16), expt.astype(BF16)


def ssd_mixer(x, conv_hist, h0, g, w_in, w_conv, b_conv, dt_bias, a_log, d_skip, g_norm, w_out):
    b, t, d = x.shape
    valid = t
    if t < LANES:
        x = jnp.pad(x, ((0, 0), (0, LANES - t), (0, 0)))
    tp = x.shape[1]
    L = min(SSD_CHUNK, tp)
    valid = L if tp > L else valid
    xs = _halo_specs(b, tp, L, d, SSD_HALO)
    gw = SSD_HPG * SSD_HD
    hist_pad = jnp.pad(conv_hist, ((0, 0), (SSD_HALO - (SSD_CONV_K - 1), 0), (0, 0)))
    wz = w_in[:, :SSD_INNER].astype(BF16)
    wxbc = w_in[:, SSD_INNER:SSD_INNER + SSD_CONV_DIM].astype(BF16)
    pad_heads = lambda v: jnp.zeros((v.shape[0], LANES), F32).at[:, :SSD_HEADS].set(v)
    wdt = pad_heads(w_in[:, SSD_INNER + SSD_CONV_DIM:]).astype(BF16)
    dtb = pad_heads(dt_bias.reshape(1, -1))
    a = pad_heads(-jnp.exp(a_log.astype(F32)).reshape(1, -1))
    dskip = jnp.repeat(d_skip, SSD_HD).reshape(1, -1)
    exp, expt = _ssd_expand_constants()
    per_b3 = lambda n, w: pl.BlockSpec((1, n, w), lambda bi, ti: (bi, 0, 0))
    state = pl.BlockSpec((1, SSD_GROUPS, gw, SSD_N), lambda bi, ti: (bi, 0, 0, 0))
    out, conv_new, h_new = pl.pallas_call(
        functools.partial(_ssd_kernel, has_halo=len(xs) == 2, valid=valid),
        grid=(b, tp // L),
        in_specs=xs + [per_b3(SSD_HALO, SSD_CONV_DIM), state, _resident((1, d)),
                       _resident((d, SSD_INNER)), _resident((d, SSD_CONV_DIM)), _resident((d, LANES)),
                       _resident((SSD_CONV_K, SSD_CONV_DIM)), _resident((1, SSD_CONV_DIM)),
                       _resident((1, LANES)), _resident((1, LANES)), _resident((1, SSD_INNER)),
                       _resident((1, SSD_INNER)), _resident((SSD_INNER, d)),
                       _resident((LANES, SSD_INNER)), _resident((SSD_GROUPS, gw, LANES))],
        out_specs=[pl.BlockSpec((1, L, d), lambda bi, ti: (bi, ti, 0)),
                   per_b3(SSD_CONV_K - 1, SSD_CONV_DIM), state],
        out_shape=[jax.ShapeDtypeStruct((b, tp, d), F32),
                   jax.ShapeDtypeStruct((b, SSD_CONV_K - 1, SSD_CONV_DIM), F32),
                   jax.ShapeDtypeStruct((b, SSD_GROUPS, gw, SSD_N), F32)],
        scratch_shapes=[pltpu.VMEM((SSD_HALO + L, SSD_CONV_DIM), F32), pltpu.VMEM((SSD_GROUPS, gw, SSD_N), F32)],
        compiler_params=_cparams("parallel", "arbitrary"),
        name="ssd_mixer",
    )(*([x] * len(xs)), hist_pad, h0.reshape(b, SSD_GROUPS, gw, SSD_N), g, wz, wxbc, wdt, w_conv,
      b_conv.reshape(1, -1), dtb, a, dskip, g_norm.reshape(1, -1), w_out.astype(BF16), exp, expt)
    return out[:, :t], conv_new, h_new.reshape(b, SSD_HEADS, SSD_HD, SSD_N)


def kernel(x_prompt, x_sample, cache_fox_k, cache_fox_v, cache_fox_logf, state_conf_conv, state_ssd_conv, state_ssd, state_pool, p_prompt, p_sample, g_mix, g_ffn, g_ple, g_final, fox_w_q, fox_w_k, fox_w_v, fox_w_f, fox_b_f, fox_w_o, conf_w_pw1, conf_b_pw1, conf_w_dw, conf_b_dw, conf_g_ln, conf_b_ln, conf_w_pw2, conf_b_pw2, ssd_w_in, ssd_w_conv, ssd_b_conv, ssd_dt_bias, ssd_a_log, ssd_d, ssd_g_norm, ssd_w_out, pool_w, pool_b, pool_scale, ffn_w_gate, ffn_w_up, ffn_w_down, moe_w_router, moe_w_gate, moe_w_up, moe_w_down, ple_w_up, ple_w_gate):
    b, t, d = x_prompt.shape
    bs, s, _ = x_sample.shape
    past = cache_fox_k.shape[1]
    row = lambda v: v.reshape(1, -1)
    bf = lambda w: w.astype(BF16)
    flat = lambda a: a.reshape(-1, a.shape[-1])
    hp, hs = x_prompt, x_sample

    def dense_ffn(h, p, i):
        j = i // 2
        out = ffn_ple(flat(h), flat(p[i]), row(g_ffn[i]), bf(ffn_w_gate[j]), bf(ffn_w_up[j]), bf(ffn_w_down[j]),
                      row(g_ple[i]), bf(ple_w_gate[i]), bf(ple_w_up[i]))
        return out.reshape(h.shape)

    def moe_ffn(h, p, i, final_norm):
        j = i // 2
        wr = jnp.zeros((d, LANES), F32).at[:, :N_EXPERTS].set(moe_w_router[j]).astype(BF16)
        out = moe_ple(flat(h), flat(p[i]), row(g_ffn[i]), wr, bf(moe_w_gate[j]), bf(moe_w_up[j]),
                      bf(moe_w_down[j]), row(g_ple[i]), bf(ple_w_gate[i]), bf(ple_w_up[i]), row(g_final),
                      final_norm)
        return out.reshape(h.shape)

    fox_k_p, fox_v_p, fox_lf_p, qt, vt, kaug, ft = fox_proj(hp, row(g_mix[0]), fox_w_q, fox_w_k, fox_w_v,
                                                            fox_w_f, fox_b_f)
    hp = out_proj_t(hp, fox_attention(qt, kaug, vt, ft), fox_w_o)
    hs, fox_k_s, fox_v_s, fox_lf_s = fox_sample(hs, row(g_mix[0]), fox_w_q, fox_w_k, fox_w_v, fox_w_f, fox_b_f,
                                                fox_w_o, cache_fox_k, cache_fox_v, cache_fox_logf)
    hp = dense_ffn(hp, p_prompt, 0)
    hs = dense_ffn(hs, p_sample, 0)

    conf_args = (row(g_mix[1]), conf_w_pw1, conf_b_pw1, conf_w_dw, conf_b_dw, conf_g_ln, conf_b_ln,
                 conf_w_pw2, conf_b_pw2)
    hp, conf_p = conformer(hp, jnp.zeros((b, CONF_K - 1, d), F32), *conf_args)
    hs, conf_s = conformer(hs, state_conf_conv, *conf_args)
    hp = moe_ffn(hp, p_prompt, 1, False)
    hs = moe_ffn(hs, p_sample, 1, False)

    ssd_args = (row(g_mix[2]), ssd_w_in, ssd_w_conv, ssd_b_conv, ssd_dt_bias, ssd_a_log, ssd_d, ssd_g_norm,
                ssd_w_out)
    hp, ssdc_p, ssdh_p = ssd_mixer(hp, jnp.zeros((b, SSD_CONV_K - 1, SSD_CONV_DIM), F32),
                                   jnp.zeros((b, SSD_HEADS, SSD_HD, SSD_N), F32), *ssd_args)
    hs, ssdc_s, ssdh_s = ssd_mixer(hs, state_ssd_conv, state_ssd, *ssd_args)
    hp = dense_ffn(hp, p_prompt, 2)
    hs = dense_ffn(hs, p_sample, 2)

    pool_args = (row(g_mix[3]), pool_w, pool_b, pool_scale)
    hp, pool_p = pool_mixer(hp, jnp.zeros((b, POOL_HIST, d), F32), 0, *pool_args)
    hs, pool_s = pool_mixer(hs, state_pool, past, *pool_args)
    y_prompt = moe_ffn(hp, p_prompt, 3, True)
    y_sample = moe_ffn(hs, p_sample, 3, True)

    heads = lambda a: a.reshape(a.shape[0], a.shape[1], FOX_HEADS, FOX_HD)
    return (y_prompt, y_sample, heads(fox_k_p), heads(fox_v_p), fox_lf_p, heads(fox_k_s), heads(fox_v_s), fox_lf_s,
            conf_p, conf_s, ssdc_p, ssdc_s, ssdh_p, ssdh_s, pool_p, pool_s)
```

```python
import functools

import jax
import jax.numpy as jnp
from jax import lax
from jax.experimental import pallas as pl
from jax.experimental.pallas import tpu as pltpu

F32 = jnp.float32
BF16 = jnp.bfloat16

D_MODEL = 1024
EPS = 1e-6
PLE_DIM = 256
FOX_HEADS = 16
FOX_HD = 64
FOX_PAIRS = FOX_HEADS // 2
FOX_VROWS = FOX_HD + 16
CONF_K = 31
CONF_HALO = 32
SSD_INNER = 2048
SSD_HD = 64
SSD_HEADS = 32
SSD_GROUPS = 4
SSD_HPG = SSD_HEADS // SSD_GROUPS
SSD_N = 128
SSD_CONV_K = 4
SSD_CONV_DIM = SSD_INNER + 2 * SSD_GROUPS * SSD_N
SSD_HALO = 8
POOL_WINDOWS = (2, 4, 8, 16)
POOL_GROUP = D_MODEL // len(POOL_WINDOWS)
POOL_HIST = 15
POOL_HALO = 16
N_EXPERTS = 8
D_FF = 2816
D_FF_EXPERT = 1408
LANES = 128

VMEM_LIMIT_BYTES = 56 * 1024 * 1024
TOKEN_TILE = 512
ATTN_BLOCK = 512
ATTN_STRIP = 256
ATTN_KEYS = 512
SSD_CHUNK = 256
FFN_CHUNK = 256


def _cparams(*sem):
    return pltpu.CompilerParams(dimension_semantics=sem, vmem_limit_bytes=VMEM_LIMIT_BYTES)


def _resident(shape):
    n = len(shape)
    return pl.BlockSpec(shape, lambda *_: (0,) * n, pipeline_mode=pl.Buffered(1))


def _dot(a, b):
    return jnp.dot(a, b, preferred_element_type=F32)


def _dot_nt(a, b):
    return lax.dot_general(a, b, (((1,), (1,)), ((), ())), preferred_element_type=F32)


def _rms(x, g):
    return x * lax.rsqrt(jnp.mean(x * x, axis=-1, keepdims=True) + EPS) * g


def _sigmoid(x):
    return 1.0 / (1.0 + jnp.exp(-x))


def _silu(x):
    return x * _sigmoid(x)


def _softplus(x):
    return jnp.maximum(x, 0.0) + jnp.log1p(jnp.exp(-jnp.abs(x)))


def _split3(x):
    p1 = x.astype(BF16)
    r1 = x - p1.astype(F32)
    p2 = r1.astype(BF16)
    r2 = r1 - p2.astype(F32)
    return p1, p2, r2.astype(BF16)


def _dot3(a_f32, b_bf16):
    p1, p2, p3 = _split3(a_f32)
    return (_dot(p1, b_bf16) + _dot(p2, b_bf16)) + _dot(p3, b_bf16)


def _dot3_left(a_bf16, b_f32):
    p1, p2, p3 = _split3(b_f32)
    return (_dot(a_bf16, p1) + _dot(a_bf16, p2)) + _dot(a_bf16, p3)


def _rep_rows(a, n):
    s, w = a.shape
    return jnp.broadcast_to(a[:, None, :], (s, n, w)).reshape(s * n, w)


def _tile_rows(a, n):
    h, w = a.shape
    return jnp.broadcast_to(a[None], (n, h, w)).reshape(n * h, w)


def _tril(n):
    r = lax.broadcasted_iota(jnp.int32, (n, n), 0)
    c = lax.broadcasted_iota(jnp.int32, (n, n), 1)
    return jnp.where(c <= r, 1.0, 0.0).astype(BF16)


def _ple_apply(h1, p, g_ple, w_gate, w_up):
    hn = _rms(h1, g_ple).astype(BF16)
    gate = _sigmoid(_dot(hn, w_gate))
    return h1 + gate * _dot(p.astype(BF16), w_up)


def _ffn_ple_kernel(h_ref, p_ref, gf_ref, wg_ref, wu_ref, wd_ref, gp_ref, wpg_ref, wpu_ref, o_ref):
    x = h_ref[...]
    xn = _rms(x, gf_ref[...]).astype(BF16)
    acc = x
    for c in range(D_FF // FFN_CHUNK):
        sl = slice(c * FFN_CHUNK, (c + 1) * FFN_CHUNK)
        hh = (_silu(_dot(xn, wg_ref[:, sl])) * _dot(xn, wu_ref[:, sl])).astype(BF16)
        acc = acc + _dot(hh, wd_ref[sl, :])
    o_ref[...] = _ple_apply(acc, p_ref[...], gp_ref[...], wpg_ref[...], wpu_ref[...])


def ffn_ple(h, p, g_ffn, wg, wu, wd, g_ple, wpg, wpu):
    m = h.shape[0]
    tm = min(TOKEN_TILE, m)
    row = lambda w: pl.BlockSpec((tm, w), lambda i: (i, 0))
    return pl.pallas_call(
        _ffn_ple_kernel,
        grid=(m // tm,),
        in_specs=[row(D_MODEL), row(PLE_DIM), _resident((1, D_MODEL)),
                  _resident((D_MODEL, D_FF)), _resident((D_MODEL, D_FF)), _resident((D_FF, D_MODEL)),
                  _resident((1, D_MODEL)), _resident((D_MODEL, D_MODEL)), _resident((PLE_DIM, D_MODEL))],
        out_specs=row(D_MODEL),
        out_shape=jax.ShapeDtypeStruct((m, D_MODEL), F32),
        compiler_params=_cparams("parallel"),
        name="ffn_ple",
    )(h, p, g_ffn, wg, wu, wd, g_ple, wpg, wpu)


def _moe_ple_kernel(h_ref, p_ref, gf_ref, wr_ref, wg_ref, wu_ref, wd_ref, gp_ref, wpg_ref, wpu_ref, gfin_ref,
                    o_ref, xn_s, comb_s, acc_s, *, final_norm):
    e = pl.program_id(1)

    @pl.when(e == 0)
    def _():
        x = h_ref[...]
        xn = _rms(x, gf_ref[...]).astype(BF16)
        xn_s[...] = xn
        acc_s[...] = x
        logits = _dot(xn, wr_ref[...])
        lane = lax.broadcasted_iota(jnp.int32, logits.shape, 1).astype(F32)
        neg = jnp.float32(-jnp.inf)
        lg = jnp.where(lane < N_EXPERTS, logits, neg)
        m1 = jnp.max(lg, axis=-1, keepdims=True)
        i1 = jnp.min(jnp.where(lg == m1, lane, float(LANES)), axis=-1, keepdims=True)
        lg2 = jnp.where(lane == i1, neg, lg)
        m2 = jnp.max(lg2, axis=-1, keepdims=True)
        i2 = jnp.min(jnp.where(lg2 == m2, lane, float(LANES)), axis=-1, keepdims=True)
        e2 = jnp.exp(m2 - m1)
        den = 1.0 + e2
        comb_s[...] = jnp.where(lane == i1, 1.0 / den, jnp.where(lane == i2, e2 / den, 0.0))

    xn = xn_s[...]
    lane = lax.broadcasted_iota(jnp.int32, comb_s.shape, 1)
    ce = jnp.sum(jnp.where(lane == e, comb_s[...], 0.0), axis=-1, keepdims=True)
    hh = (_silu(_dot(xn, wg_ref[0])) * _dot(xn, wu_ref[0])).astype(BF16)
    acc_s[...] += ce * _dot(hh, wd_ref[0])

    @pl.when(e == N_EXPERTS - 1)
    def _():
        h2 = _ple_apply(acc_s[...], p_ref[...], gp_ref[...], wpg_ref[...], wpu_ref[...])
        if final_norm:
            h2 = _rms(h2, gfin_ref[...])
        o_ref[...] = h2


def moe_ple(h, p, g_ffn, wr, wg, wu, wd, g_ple, wpg, wpu, g_final, final_norm):
    m = h.shape[0]
    tm = min(TOKEN_TILE, m)
    row = lambda w: pl.BlockSpec((tm, w), lambda i, e: (i, 0))
    expert = lambda a, b: pl.BlockSpec((1, a, b), lambda i, e: (e, 0, 0))
    return pl.pallas_call(
        functools.partial(_moe_ple_kernel, final_norm=final_norm),
        grid=(m // tm, N_EXPERTS),
        in_specs=[row(D_MODEL), row(PLE_DIM), _resident((1, D_MODEL)), _resident((D_MODEL, LANES)),
                  expert(D_MODEL, D_FF_EXPERT), expert(D_MODEL, D_FF_EXPERT), expert(D_FF_EXPERT, D_MODEL),
                  _resident((1, D_MODEL)), _resident((D_MODEL, D_MODEL)), _resident((PLE_DIM, D_MODEL)),
                  _resident((1, D_MODEL))],
        out_specs=row(D_MODEL),
        out_shape=jax.ShapeDtypeStruct((m, D_MODEL), F32),
        scratch_shapes=[pltpu.VMEM((tm, D_MODEL), BF16), pltpu.VMEM((tm, LANES), F32),
                        pltpu.VMEM((tm, D_MODEL), F32)],
        compiler_params=_cparams("parallel", "arbitrary"),
        name="moe_ple",
    )(h, p, g_ffn, wr, wg, wu, wd, g_ple, wpg, wpu, g_final)


def _log_sigmoid(z):
    return jnp.minimum(z, 0.0) - jnp.log1p(jnp.exp(-jnp.abs(z)))


def _fox_proj_kernel(x_ref, g_ref, wqt_ref, wk_ref, wv_ref, wf_ref, bf_ref, sel_ref, ones_ref,
                     k_out, v_out, lf_out, qt_out, vt_out, kaug_out, ft_out, carry):
    tm = x_ref.shape[1]

    @pl.when(pl.program_id(1) == 0)
    def _():
        carry[...] = jnp.zeros_like(carry)

    xn = _rms(x_ref[0], g_ref[...])
    xb = xn.astype(BF16)
    xnt = xn.T.astype(BF16)
    k = _dot(xb, wk_ref[...])
    v = _dot(xb, wv_ref[...])
    k_out[0] = k
    v_out[0] = v
    qt_out[0] = (_dot(wqt_ref[...], xnt) * (FOX_HD ** -0.5)).astype(BF16)
    vt = v.T.astype(BF16)
    ones_rows = jnp.ones((FOX_VROWS - FOX_HD, tm), BF16)
    for h in range(FOX_HEADS):
        vt_out[0, 0, FOX_VROWS * h:FOX_VROWS * h + FOX_HD, :] = vt[FOX_HD * h:FOX_HD * (h + 1), :]
        vt_out[0, 0, FOX_VROWS * h + FOX_HD:FOX_VROWS * (h + 1), :] = ones_rows
    lf = _log_sigmoid(_dot(xb, wf_ref[...]) + bf_ref[...])
    lf_out[0] = lf[:, :FOX_HEADS]
    f = _dot3_left(_tril(tm), lf) + carry[...]
    carry[...] = f[tm - 1:tm, :]
    ft_out[0] = f.T[:FOX_HEADS, :]
    p1, p2, p3 = _split3(f)
    aug = ((_dot(p1, sel_ref[0]) + _dot(p2, sel_ref[1])) + _dot(p3, sel_ref[2]) + ones_ref[...]).astype(BF16)
    kb = k.astype(BF16)
    for p in range(FOX_PAIRS):
        kaug_out[0, :, 2 * LANES * p:2 * LANES * p + LANES] = kb[:, LANES * p:LANES * (p + 1)]
        kaug_out[0, :, 2 * LANES * p + LANES:2 * LANES * (p + 1)] = aug[:, LANES * p:LANES * (p + 1)]


def _fox_aug_constants():
    h = jnp.arange(FOX_HEADS)
    sel = jnp.zeros((3, LANES, FOX_PAIRS * LANES), F32)
    for r in range(3):
        sel = sel.at[r, h, LANES * (h // 2) + 16 * (h % 2) + r].set(1.0)
    p = jnp.arange(FOX_PAIRS)
    ones = jnp.zeros((1, FOX_PAIRS * LANES), F32)
    for r in range(3):
        ones = ones.at[0, LANES * p + 32 + r].set(1.0)
    return sel.astype(BF16), ones


def fox_proj(x, g, wq, wk, wv, wf, bf):
    b, t, d = x.shape
    tm = min(ATTN_BLOCK, t)
    sel, ones = _fox_aug_constants()
    wf_pad = jnp.zeros((d, LANES), F32).at[:, :FOX_HEADS].set(wf).astype(BF16)
    bf_pad = jnp.zeros((1, LANES), F32).at[0, :FOX_HEADS].set(bf)
    tile = lambda w: pl.BlockSpec((1, tm, w), lambda bi, ti: (bi, ti, 0))
    return pl.pallas_call(
        _fox_proj_kernel,
        grid=(b, t // tm),
        in_specs=[tile(d), _resident((1, d)), _resident((d, d)), _resident((d, d)), _resident((d, d)),
                  _resident((d, LANES)), _resident((1, LANES)), _resident((3, LANES, FOX_PAIRS * LANES)),
                  _resident((1, FOX_PAIRS * LANES))],
        out_specs=[tile(d), tile(d), tile(FOX_HEADS),
                   pl.BlockSpec((1, d, tm), lambda bi, ti: (bi, 0, ti)),
                   pl.BlockSpec((1, 1, FOX_HEADS * FOX_VROWS, tm), lambda bi, ti: (bi, ti, 0, 0)),
                   tile(2 * d),
                   pl.BlockSpec((1, FOX_HEADS, tm), lambda bi, ti: (bi, 0, ti))],
        out_shape=[jax.ShapeDtypeStruct((b, t, d), F32), jax.ShapeDtypeStruct((b, t, d), F32),
                   jax.ShapeDtypeStruct((b, t, FOX_HEADS), F32),
                   jax.ShapeDtypeStruct((b, d, t), BF16),
                   jax.ShapeDtypeStruct((b, t // tm, FOX_HEADS * FOX_VROWS, tm), BF16),
                   jax.ShapeDtypeStruct((b, t, 2 * d), BF16),
                   jax.ShapeDtypeStruct((b, FOX_HEADS, t), F32)],
        scratch_shapes=[pltpu.VMEM((1, LANES), F32)],
        compiler_params=_cparams("arbitrary", "arbitrary"),
        name="fox_proj",
    )(x, g, wq.T.astype(BF16), wk.astype(BF16), wv.astype(BF16), wf_pad, bf_pad, sel, ones)


def _fox_attn_kernel(qt_ref, kaug_ref, vt_ref, ft_ref, ot_ref, qaug, m_s, acc, s_buf, p_pend, a_pend):
    tq = qt_ref.shape[2]
    tk = vt_ref.shape[3]
    i = pl.program_id(2)
    zeros = lambda r: jnp.zeros((r, tq), BF16)
    minus = jnp.full((16, tq), -1.0, BF16)
    row = lax.broadcasted_iota(jnp.int32, (16, tq), 0)
    for h in range(2):
        qaug[h, 64 * h:64 * (h + 1)] = qt_ref[0, 64 * h:64 * (h + 1), :]
        qaug[h, 64 * (1 - h):64 * (2 - h)] = zeros(64)
        qaug[h, LANES + 16 * h:LANES + 16 * (h + 1)] = minus
        qaug[h, LANES + 16 * (1 - h):LANES + 16 * (2 - h)] = zeros(16)
        p1, p2, p3 = _split3(ft_ref[0, 0, h:h + 1, :])
        blk = jnp.where(row == 0, p1.astype(F32),
                        jnp.where(row == 1, p2.astype(F32), jnp.where(row == 2, p3.astype(F32), 0.0)))
        qaug[h, LANES + 32:LANES + 48] = blk.astype(BF16)
        qaug[h, LANES + 48:2 * LANES] = zeros(LANES - 48)
    m_s[...] = jnp.full(m_s.shape, -jnp.inf, F32)
    acc[...] = jnp.zeros(acc.shape, F32)

    chains = [(h, c) for h in range(2) for c in range(tq // ATTN_STRIP)]

    def qk(n, j, slot):
        h, c = chains[n]
        qa = qaug[h, :, c * ATTN_STRIP:(c + 1) * ATTN_STRIP]
        for r in range(tk // ATTN_KEYS):
            kj = kaug_ref[0, pl.ds(pl.multiple_of(j * tk + r * ATTN_KEYS, ATTN_KEYS), ATTN_KEYS), :]
            s_buf[slot, n, r * ATTN_KEYS:(r + 1) * ATTN_KEYS, :] = _dot(kj, qa)

    def softmax(n, slot, masked):
        h, c = chains[n]
        qs = slice(c * ATTN_STRIP, (c + 1) * ATTN_STRIP)
        st = s_buf[slot, n]
        if masked:
            kpos = lax.broadcasted_iota(jnp.int32, st.shape, 0)
            qpos = lax.broadcasted_iota(jnp.int32, st.shape, 1) + c * ATTN_STRIP
            st = jnp.where(kpos <= qpos, st, -jnp.inf)
        m_prev = m_s[h, :, qs]
        m_new = jnp.maximum(m_prev, jnp.max(st, axis=0, keepdims=True))
        alpha = jnp.exp(m_prev - m_new)
        p = jnp.exp(st - m_new)
        m_s[h, :, qs] = m_new
        return p.astype(BF16), alpha

    def pv(n, j, p, alpha):
        h, c = chains[n]
        qs = slice(c * ATTN_STRIP, (c + 1) * ATTN_STRIP)
        acc[h, :, qs] = alpha * acc[h, :, qs] + _dot(vt_ref[0, j, FOX_VROWS * h:FOX_VROWS * (h + 1), :], p)

    last = len(chains) - 1

    def block(j, slot, masked, prefetch=True, pending=None):
        jn = jnp.maximum(j - 1, 0)
        pa = {}
        if prefetch:
            for n in range(len(chains)):
                qk(n, jn, 1 - slot)
        for n in range(len(chains)):
            pa[n] = softmax(n, slot, masked)
            if n >= 1:
                pv(n - 1, j, *pa.pop(n - 1))
        pv(last, j, *pa.pop(last))

    for n in range(len(chains)):
        qk(n, i, 0)
    block(i, 0, True)

    def pair(u, carry):
        j = i - 1 - 2 * u
        block(j, 1, False, pending=j + 1)
        block(j - 1, 0, False, pending=j)
        return carry

    lax.fori_loop(0, i // 2, pair, 0)

    @pl.when(i % 2 == 1)
    def _():
        block(0, 1, False, prefetch=False, pending=1)

    for h in range(2):
        ot_ref[0, 64 * h:64 * (h + 1), :] = (acc[h, :FOX_HD] / acc[h, FOX_HD:FOX_HD + 1]).astype(BF16)


def fox_attention(qt, kaug, vt, ft):
    b, d, t = qt.shape
    blk = vt.shape[3]
    ft = ft.reshape(b, FOX_PAIRS, 2, t)
    return pl.pallas_call(
        _fox_attn_kernel,
        grid=(b, FOX_PAIRS, t // blk),
        in_specs=[pl.BlockSpec((1, LANES, blk), lambda bi, p, i: (bi, p, i)),
                  pl.BlockSpec((1, t, 2 * LANES), lambda bi, p, i: (bi, 0, p)),
                  pl.BlockSpec((1, t // blk, 2 * FOX_VROWS, blk), lambda bi, p, i: (bi, 0, p, 0)),
                  pl.BlockSpec((1, 1, 2, blk), lambda bi, p, i: (bi, p, 0, i))],
        out_specs=pl.BlockSpec((1, LANES, blk), lambda bi, p, i: (bi, p, i)),
        out_shape=jax.ShapeDtypeStruct((b, d, t), BF16),
        scratch_shapes=[pltpu.VMEM((2, 2 * LANES, blk), BF16), pltpu.VMEM((2, 1, blk), F32),
                        pltpu.VMEM((2, FOX_VROWS, blk), F32),
                        pltpu.VMEM((2, 2 * blk // ATTN_STRIP, blk, ATTN_STRIP), F32),
                        pltpu.VMEM((blk, ATTN_STRIP), BF16), pltpu.VMEM((1, ATTN_STRIP), F32)],
        compiler_params=_cparams("parallel", "parallel", "arbitrary"),
        name="fox_attention",
    )(qt, kaug, vt, ft)


def _out_proj_t_kernel(h_ref, ot_ref, w_ref, o_ref):
    o_ref[0] = h_ref[0] + pl.dot(ot_ref[0], w_ref[...], trans_a=True)


def out_proj_t(h, ot, w):
    b, t, d = h.shape
    tm = min(TOKEN_TILE, t)
    tile = pl.BlockSpec((1, tm, d), lambda bi, ti: (bi, ti, 0))
    return pl.pallas_call(
        _out_proj_t_kernel,
        grid=(b, t // tm),
        in_specs=[tile, pl.BlockSpec((1, d, tm), lambda bi, ti: (bi, 0, ti)), _resident((d, d))],
        out_specs=tile,
        out_shape=jax.ShapeDtypeStruct((b, t, d), F32),
        compiler_params=_cparams("parallel", "parallel"),
        name="fox_out_proj",
    )(h, ot, w.astype(BF16))


def _fox_sample_kernel(x_ref, g_ref, wq_ref, wk_ref, wv_ref, wf_ref, bf_ref, wo_ref, ck_ref, cv_ref, clf_ref,
                       h_out, k_out, v_out, lf_out):
    s = x_ref.shape[1]
    pl_len = ck_ref.shape[1]
    x = x_ref[0]
    xb = _rms(x, g_ref[...]).astype(BF16)
    q = _dot(xb, wq_ref[...]) * (FOX_HD ** -0.5)
    k = _dot(xb, wk_ref[...])
    v = _dot(xb, wv_ref[...])
    lf = _log_sigmoid(_dot(xb, wf_ref[...]) + bf_ref[...])
    k_out[0] = k
    v_out[0] = v
    lf_out[0] = lf[:, :FOX_HEADS]

    rows = s * FOX_HEADS
    rh = lax.broadcasted_iota(jnp.int32, (rows, D_MODEL), 0) % FOX_HEADS
    lh = lax.broadcasted_iota(jnp.int32, (rows, D_MODEL), 1) // FOX_HD
    head_lanes = rh == lh
    qbd = jnp.where(head_lanes, _rep_rows(q, FOX_HEADS), 0.0).astype(BF16)

    cblk = 256
    carry = jnp.zeros((1, LANES), F32)
    tri = _tril(cblk)
    f_parts = []
    for c in range(pl_len // cblk):
        fc = _dot3_left(tri, clf_ref[0, c * cblk:(c + 1) * cblk, :]) + carry
        carry = fc[cblk - 1:cblk, :]
        f_parts.append(fc)
    f_cache = jnp.concatenate(f_parts, axis=0)
    pad_rows = lambda a: jnp.concatenate([a, jnp.zeros((LANES - s, a.shape[1]), a.dtype)], axis=0)
    f_new = _dot3_left(_tril(LANES), pad_rows(lf)) + carry

    lane = lax.broadcasted_iota(jnp.int32, (rows, LANES), 1)
    row_head = lax.broadcasted_iota(jnp.int32, (rows, LANES), 0) % FOX_HEADS
    fq = jnp.sum(jnp.where(lane == row_head, _rep_rows(f_new[:s], FOX_HEADS), 0.0), axis=-1, keepdims=True)
    fk_cache = _tile_rows(f_cache.T[:FOX_HEADS, :], s)
    fk_new = _tile_rows(f_new.T[:FOX_HEADS, :], s)

    kc = 512
    qk = [_dot_nt(qbd, ck_ref[0, c * kc:(c + 1) * kc, :].astype(BF16)) for c in range(pl_len // kc)]
    s_cache = jnp.concatenate(qk, axis=1) + fq - fk_cache
    s_new = _dot_nt(qbd, pad_rows(k).astype(BF16)) + fq - fk_new
    qi = lax.broadcasted_iota(jnp.int32, (rows, LANES), 0) // FOX_HEADS
    kj = lax.broadcasted_iota(jnp.int32, (rows, LANES), 1)
    s_new = jnp.where(kj <= qi, s_new, -jnp.inf)
    m = jnp.maximum(jnp.max(s_cache, axis=-1, keepdims=True), jnp.max(s_new, axis=-1, keepdims=True))
    p_cache = jnp.exp(s_cache - m)
    p_new = jnp.exp(s_new - m)
    den = jnp.sum(p_cache, axis=-1, keepdims=True) + jnp.sum(p_new, axis=-1, keepdims=True)
    pv = _dot(p_new.astype(BF16), pad_rows(v).astype(BF16))
    pb = p_cache.astype(BF16)
    for c in range(pl_len // kc):
        pv = pv + _dot(pb[:, c * kc:(c + 1) * kc], cv_ref[0, c * kc:(c + 1) * kc, :].astype(BF16))
    o_heads = jnp.where(head_lanes, pv / den, 0.0).astype(BF16)
    pick = (lax.broadcasted_iota(jnp.int32, (s, rows), 1) // FOX_HEADS
            == lax.broadcasted_iota(jnp.int32, (s, rows), 0))
    o = _dot(jnp.where(pick, 1.0, 0.0).astype(BF16), o_heads)
    h_out[0] = x + _dot(o.astype(BF16), wo_ref[...])


def fox_sample(x, g, wq, wk, wv, wf, bf, wo, ck, cv, clf):
    b, s, d = x.shape
    p = ck.shape[1]
    wf_pad = jnp.zeros((d, LANES), F32).at[:, :FOX_HEADS].set(wf).astype(BF16)
    bf_pad = jnp.zeros((1, LANES), F32).at[0, :FOX_HEADS].set(bf)
    clf_pad = jnp.zeros((b, p, LANES), F32).at[:, :, :FOX_HEADS].set(clf)
    seq = lambda n, w: pl.BlockSpec((1, n, w), lambda bi: (bi, 0, 0))
    return pl.pallas_call(
        _fox_sample_kernel,
        grid=(b,),
        in_specs=[seq(s, d), _resident((1, d)), _resident((d, d)), _resident((d, d)), _resident((d, d)),
                  _resident((d, LANES)), _resident((1, LANES)), _resident((d, d)),
                  seq(p, d), seq(p, d), seq(p, LANES)],
        out_specs=[seq(s, d), seq(s, d), seq(s, d), seq(s, FOX_HEADS)],
        out_shape=[jax.ShapeDtypeStruct((b, s, d), F32)] * 3 + [jax.ShapeDtypeStruct((b, s, FOX_HEADS), F32)],
        compiler_params=_cparams("parallel"),
        name="fox_sample",
    )(x, g, wq.astype(BF16), wk.astype(BF16), wv.astype(BF16), wf_pad, bf_pad, wo.astype(BF16),
      ck.reshape(b, p, d), cv.reshape(b, p, d), clf_pad)


def _halo_specs(b, t, tm, d, halo):
    main = pl.BlockSpec((1, tm, d), lambda bi, ti: (bi, ti, 0))
    if t == tm:
        return [main]
    per = tm // halo
    return [main, pl.BlockSpec((1, halo, d), lambda bi, ti: (bi, jnp.maximum(ti * per - 1, 0), 0))]


def _conf_kernel(*refs, has_halo):
    if has_halo:
        x_ref, xh_ref, *refs = refs
    else:
        x_ref, *refs = refs
    (hist_ref, g_ref, w1_ref, b1_ref, wdw_ref, bdw_ref, gln_ref, bln_ref, w2_ref, b2_ref,
     o_ref, hs_ref, uext) = refs
    tm = x_ref.shape[1]

    def glu(rows):
        ag = _dot(_rms(rows, g_ref[...]).astype(BF16), w1_ref[...]) + b1_ref[...]
        return ag[:, :D_MODEL] * _sigmoid(ag[:, D_MODEL:])

    x = x_ref[0]
    if has_halo:
        prev = jnp.where(pl.program_id(1) == 0, hist_ref[0], glu(xh_ref[0]))
    else:
        prev = hist_ref[0]
    uext[0:CONF_HALO] = prev
    uext[CONF_HALO:] = glu(x)
    off = CONF_HALO - (CONF_K - 1)
    y = jnp.broadcast_to(bdw_ref[...], (tm, D_MODEL))
    for k in range(CONF_K):
        y = y + wdw_ref[k:k + 1, :] * uext[off + k:off + k + tm, :]
    hs_ref[0] = uext[tm + off:tm + CONF_HALO, :]
    mu = jnp.mean(y, axis=-1, keepdims=True)
    yc = y - mu
    var = jnp.mean(yc * yc, axis=-1, keepdims=True)
    v = _silu(yc * lax.rsqrt(var + EPS) * gln_ref[...] + bln_ref[...])
    o_ref[0] = x + _dot(v.astype(BF16), w2_ref[...]) + b2_ref[...]


def conformer(x, hist, g, w1, b1, wdw, bdw, gln, bln, w2, b2):
    b, t, d = x.shape
    tm = min(TOKEN_TILE, t)
    xs = _halo_specs(b, t, tm, d, CONF_HALO)
    hist_pad = jnp.pad(hist, ((0, 0), (CONF_HALO - (CONF_K - 1), 0), (0, 0)))
    per_b = lambda n: pl.BlockSpec((1, n, d), lambda bi, ti: (bi, 0, 0))
    return pl.pallas_call(
        functools.partial(_conf_kernel, has_halo=len(xs) == 2),
        grid=(b, t // tm),
        in_specs=xs + [per_b(CONF_HALO), _resident((1, d)), _resident((d, 2 * d)), _resident((1, 2 * d)),
                       _resident((CONF_K, d)), _resident((1, d)), _resident((1, d)), _resident((1, d)),
                       _resident((d, d)), _resident((1, d))],
        out_specs=[pl.BlockSpec((1, tm, d), lambda bi, ti: (bi, ti, 0)), per_b(CONF_K - 1)],
        out_shape=[jax.ShapeDtypeStruct((b, t, d), F32), jax.ShapeDtypeStruct((b, CONF_K - 1, d), F32)],
        scratch_shapes=[pltpu.VMEM((CONF_HALO + tm, d), F32)],
        compiler_params=_cparams("parallel", "arbitrary"),
        name="conformer",
    )(*([x] * len(xs)), hist_pad, g, w1.astype(BF16), b1.reshape(1, -1), wdw, bdw.reshape(1, -1),
      gln.reshape(1, -1), bln.reshape(1, -1), w2.astype(BF16), b2.reshape(1, -1))


def _pool_kernel(*refs, has_halo, pos0):
    if has_halo:
        x_ref, xh_ref, *refs = refs
    else:
        x_ref, *refs = refs
    hist_ref, g_ref, w_ref, b_ref, scale_ref, o_ref, hs_ref, ext = refs
    tm = x_ref.shape[1]
    i = pl.program_id(1)
    x = x_ref[0]
    xn = _rms(x, g_ref[...])
    if has_halo:
        prev = jnp.where(i == 0, hist_ref[0], _rms(xh_ref[0], g_ref[...]))
    else:
        prev = hist_ref[0]
    ext[0:POOL_HALO] = prev
    ext[POOL_HALO:] = xn
    hs_ref[0] = ext[tm + POOL_HALO - POOL_HIST:tm + POOL_HALO, :]
    pos = pos0 + i * tm + lax.broadcasted_iota(jnp.int32, (tm, 1), 0)
    ys = []
    for gi, wlen in enumerate(POOL_WINDOWS):
        lanes = slice(gi * POOL_GROUP, (gi + 1) * POOL_GROUP)
        s = ext[POOL_HALO:POOL_HALO + tm, lanes]
        for k in range(1, wlen):
            s = s + ext[POOL_HALO - k:POOL_HALO - k + tm, lanes]
        cnt = jnp.minimum(pos + 1, wlen).astype(F32)
        dg = s / cnt - xn[:, lanes]
        ys.append(_dot(dg.astype(BF16), w_ref[gi]))
    y = (jnp.concatenate(ys, axis=1) + b_ref[...]) * scale_ref[...]
    o_ref[0] = x + y


def pool_mixer(x, hist, pos0, g, w, bias, scale):
    b, t, d = x.shape
    tm = min(TOKEN_TILE, t)
    xs = _halo_specs(b, t, tm, d, POOL_HALO)
    hist_pad = jnp.pad(hist, ((0, 0), (POOL_HALO - POOL_HIST, 0), (0, 0)))
    per_b = lambda n: pl.BlockSpec((1, n, d), lambda bi, ti: (bi, 0, 0))
    ng = len(POOL_WINDOWS)
    return pl.pallas_call(
        functools.partial(_pool_kernel, has_halo=len(xs) == 2, pos0=pos0),
        grid=(b, t // tm),
        in_specs=xs + [per_b(POOL_HALO), _resident((1, d)), _resident((ng, POOL_GROUP, POOL_GROUP)),
                       _resident((1, d)), _resident((1, d))],
        out_specs=[pl.BlockSpec((1, tm, d), lambda bi, ti: (bi, ti, 0)), per_b(POOL_HIST)],
        out_shape=[jax.ShapeDtypeStruct((b, t, d), F32), jax.ShapeDtypeStruct((b, POOL_HIST, d), F32)],
        scratch_shapes=[pltpu.VMEM((POOL_HALO + tm, d), F32)],
        compiler_params=_cparams("parallel", "arbitrary"),
        name="pool_mixer",
    )(*([x] * len(xs)), hist_pad, g, w.astype(BF16), bias.reshape(1, -1), scale.reshape(1, -1))


def _ssd_kernel(*refs, has_halo, valid):
    if has_halo:
        x_ref, xh_ref, *refs = refs
    else:
        x_ref, *refs = refs
    (chist_ref, h0_ref, g_ref, wz_ref, wxbc_ref, wdt_ref, wconv_ref, bconv_ref, dtb_ref, a_ref, dskip_ref,
     gn_ref, wout_ref, exp_ref, expt_ref, o_ref, cs_out_ref, h_out_ref, ext, hstate) = refs
    L = x_ref.shape[1]
    i = pl.program_id(1)
    gw = SSD_HPG * SSD_HD
    gn = SSD_GROUPS * SSD_N

    @pl.when(i == 0)
    def _():
        hstate[...] = h0_ref[0]

    x = x_ref[0]
    xb = _rms(x, g_ref[...]).astype(BF16)
    z = _dot(xb, wz_ref[...])
    dtr = _dot(xb, wdt_ref[...])
    if has_halo:
        prev = jnp.where(i == 0, chist_ref[0], _dot(_rms(xh_ref[0], g_ref[...]).astype(BF16), wxbc_ref[...]))
    else:
        prev = chist_ref[0]
    ext[0:SSD_HALO] = prev
    ext[SSD_HALO:] = _dot(xb, wxbc_ref[...])
    off = SSD_HALO - (SSD_CONV_K - 1)
    y = jnp.broadcast_to(bconv_ref[...], (L, SSD_CONV_DIM))
    for k in range(SSD_CONV_K):
        y = y + wconv_ref[k:k + 1, :] * ext[off + k:off + k + L, :]
    cs_out_ref[0] = ext[valid + off:valid + SSD_HALO, :]
    xbc = _silu(y)
    xs = xbc[:, :SSD_INNER]
    bm = xbc[:, SSD_INNER:SSD_INNER + gn].astype(BF16)
    cm = xbc[:, SSD_INNER + gn:].astype(BF16)

    head_lane = lax.broadcasted_iota(jnp.int32, (L, LANES), 1) < SSD_HEADS
    live = head_lane
    if valid < L:
        live = live & (lax.broadcasted_iota(jnp.int32, (L, LANES), 0) < valid)
    dt = jnp.where(live, _softplus(dtr + dtb_ref[...]), 0.0)
    cs = _dot3_left(_tril(L), dt * a_ref[...])
    cst = cs.T
    dtt = dt.T
    to_end = jnp.exp(cs[L - 1:L, :] - cs) * dt
    td_x = _dot3(to_end, exp_ref[...])
    ecs_x = _dot3(jnp.exp(cs), exp_ref[...])
    chunk_decay = jnp.broadcast_to(jnp.exp(cst[:, L - 1:L]), (LANES, LANES))
    xs_b = xs.astype(BF16)
    xtd = xs * td_x
    r = lax.broadcasted_iota(jnp.int32, (L, L), 0)
    c = lax.broadcasted_iota(jnp.int32, (L, L), 1)
    causal = c <= r
    low_half = lax.broadcasted_iota(jnp.int32, (L, 2 * SSD_HD), 1) < SSD_HD

    y_groups = []
    for g in range(SSD_GROUPS):
        cg = cm[:, g * SSD_N:(g + 1) * SSD_N]
        bg = bm[:, g * SSD_N:(g + 1) * SSD_N]
        cb = _dot_nt(cg, bg)
        hg = hstate[g]
        y_off = _dot_nt(cg, hg.astype(BF16)) * ecs_x[:, g * gw:(g + 1) * gw]
        pairs = []
        for pr in range(SSD_HPG // 2):
            lo = g * gw + pr * 2 * SSD_HD
            xpair = xs_b[:, lo:lo + 2 * SSD_HD]
            acc = None
            for hh in range(2):
                e = g * SSD_HPG + 2 * pr + hh
                seg = cs[:, e:e + 1] - cst[e:e + 1, :]
                w = cb * jnp.exp(jnp.where(causal, seg, -jnp.inf)) * dtt[e:e + 1, :]
                xm = jnp.where(low_half == (hh == 0), xpair, jnp.zeros_like(xpair))
                term = _dot(w.astype(BF16), xm)
                acc = term if acc is None else acc + term
            pairs.append(acc)
        y_groups.append(jnp.concatenate(pairs, axis=1) + y_off)
        s_g = _dot(xtd[:, g * gw:(g + 1) * gw].T.astype(BF16), bg)
        hstate[g] = hg * _dot3_left(expt_ref[g], chunk_decay) + s_g

    ys = (jnp.concatenate(y_groups, axis=1) + dskip_ref[...] * xs) * _silu(z)
    normed = []
    for g in range(SSD_GROUPS):
        yg = ys[:, g * gw:(g + 1) * gw]
        normed.append(yg * lax.rsqrt(jnp.mean(yg * yg, axis=-1, keepdims=True) + EPS))
    yn = jnp.concatenate(normed, axis=1) * gn_ref[...]
    o_ref[0] = x + _dot(yn.astype(BF16), wout_ref[...])

    @pl.when(i == pl.num_programs(1) - 1)
    def _():
        h_out_ref[0] = hstate[...]


def _ssd_expand_constants():
    e = jnp.arange(SSD_HEADS)
    ch = jnp.arange(SSD_INNER)
    exp = jnp.zeros((LANES, SSD_INNER), F32).at[ch // SSD_HD, ch].set(1.0)
    gw = SSD_HPG * SSD_HD
    rows = jnp.arange(gw)
    expt = jnp.zeros((SSD_GROUPS, gw, LANES), F32)
    for g in range(SSD_GROUPS):
        expt = expt.at[g, rows, g * SSD_HPG + rows // SSD_HD].set(1.0)
    del e
    return exp.astype(BF16), expt.astype(BF16)


def ssd_mixer(x, conv_hist, h0, g, w_in, w_conv, b_conv, dt_bias, a_log, d_skip, g_norm, w_out):
    b, t, d = x.shape
    valid = t
    if t < LANES:
        x = jnp.pad(x, ((0, 0), (0, LANES - t), (0, 0)))
    tp = x.shape[1]
    L = min(SSD_CHUNK, tp)
    valid = L if tp > L else valid
    xs = _halo_specs(b, tp, L, d, SSD_HALO)
    gw = SSD_HPG * SSD_HD
    hist_pad = jnp.pad(conv_hist, ((0, 0), (SSD_HALO - (SSD_CONV_K - 1), 0), (0, 0)))
    wz = w_in[:, :SSD_INNER].astype(BF16)
    wxbc = w_in[:, SSD_INNER:SSD_INNER + SSD_CONV_DIM].astype(BF16)
    pad_heads = lambda v: jnp.zeros((v.shape[0], LANES), F32).at[:, :SSD_HEADS].set(v)
    wdt = pad_heads(w_in[:, SSD_INNER + SSD_CONV_DIM:]).astype(BF16)
    dtb = pad_heads(dt_bias.reshape(1, -1))
    a = pad_heads(-jnp.exp(a_log.astype(F32)).reshape(1, -1))
    dskip = jnp.repeat(d_skip, SSD_HD).reshape(1, -1)
    exp, expt = _ssd_expand_constants()
    per_b3 = lambda n, w: pl.BlockSpec((1, n, w), lambda bi, ti: (bi, 0, 0))
    state = pl.BlockSpec((1, SSD_GROUPS, gw, SSD_N), lambda bi, ti: (bi, 0, 0, 0))
    out, conv_new, h_new = pl.pallas_call(
        functools.partial(_ssd_kernel, has_halo=len(xs) == 2, valid=valid),
        grid=(b, tp // L),
        in_specs=xs + [per_b3(SSD_HALO, SSD_CONV_DIM), state, _resident((1, d)),
                       _resident((d, SSD_INNER)), _resident((d, SSD_CONV_DIM)), _resident((d, LANES)),
                       _resident((SSD_CONV_K, SSD_CONV_DIM)), _resident((1, SSD_CONV_DIM)),
                       _resident((1, LANES)), _resident((1, LANES)), _resident((1, SSD_INNER)),
                       _resident((1, SSD_INNER)), _resident((SSD_INNER, d)),
                       _resident((LANES, SSD_INNER)), _resident((SSD_GROUPS, gw, LANES))],
        out_specs=[pl.BlockSpec((1, L, d), lambda bi, ti: (bi, ti, 0)),
                   per_b3(SSD_CONV_K - 1, SSD_CONV_DIM), state],
        out_shape=[jax.ShapeDtypeStruct((b, tp, d), F32),
                   jax.ShapeDtypeStruct((b, SSD_CONV_K - 1, SSD_CONV_DIM), F32),
                   jax.ShapeDtypeStruct((b, SSD_GROUPS, gw, SSD_N), F32)],
        scratch_shapes=[pltpu.VMEM((SSD_HALO + L, SSD_CONV_DIM), F32), pltpu.VMEM((SSD_GROUPS, gw, SSD_N), F32)],
        compiler_params=_cparams("parallel", "arbitrary"),
        name="ssd_mixer",
    )(*([x] * len(xs)), hist_pad, h0.reshape(b, SSD_GROUPS, gw, SSD_N), g, wz, wxbc, wdt, w_conv,
      b_conv.reshape(1, -1), dtb, a, dskip, g_norm.reshape(1, -1), w_out.astype(BF16), exp, expt)
    return out[:, :t], conv_new, h_new.reshape(b, SSD_HEADS, SSD_HD, SSD_N)


def kernel(x_prompt, x_sample, cache_fox_k, cache_fox_v, cache_fox_logf, state_conf_conv, state_ssd_conv, state_ssd, state_pool, p_prompt, p_sample, g_mix, g_ffn, g_ple, g_final, fox_w_q, fox_w_k, fox_w_v, fox_w_f, fox_b_f, fox_w_o, conf_w_pw1, conf_b_pw1, conf_w_dw, conf_b_dw, conf_g_ln, conf_b_ln, conf_w_pw2, conf_b_pw2, ssd_w_in, ssd_w_conv, ssd_b_conv, ssd_dt_bias, ssd_a_log, ssd_d, ssd_g_norm, ssd_w_out, pool_w, pool_b, pool_scale, ffn_w_gate, ffn_w_up, ffn_w_down, moe_w_router, moe_w_gate, moe_w_up, moe_w_down, ple_w_up, ple_w_gate):
    b, t, d = x_prompt.shape
    bs, s, _ = x_sample.shape
    past = cache_fox_k.shape[1]
    row = lambda v: v.reshape(1, -1)
    bf = lambda w: w.astype(BF16)
    flat = lambda a: a.reshape(-1, a.shape[-1])
    hp, hs = x_prompt, x_sample

    def dense_ffn(h, p, i):
        j = i // 2
        out = ffn_ple(flat(h), flat(p[i]), row(g_ffn[i]), bf(ffn_w_gate[j]), bf(ffn_w_up[j]), bf(ffn_w_down[j]),
                      row(g_ple[i]), bf(ple_w_gate[i]), bf(ple_w_up[i]))
        return out.reshape(h.shape)

    def moe_ffn(h, p, i, final_norm):
        j = i // 2
        wr = jnp.zeros((d, LANES), F32).at[:, :N_EXPERTS].set(moe_w_router[j]).astype(BF16)
        out = moe_ple(flat(h), flat(p[i]), row(g_ffn[i]), wr, bf(moe_w_gate[j]), bf(moe_w_up[j]),
                      bf(moe_w_down[j]), row(g_ple[i]), bf(ple_w_gate[i]), bf(ple_w_up[i]), row(g_final),
                      final_norm)
        return out.reshape(h.shape)

    fox_k_p, fox_v_p, fox_lf_p, qt, vt, kaug, ft = fox_proj(hp, row(g_mix[0]), fox_w_q, fox_w_k, fox_w_v,
                                                            fox_w_f, fox_b_f)
    hp = out_proj_t(hp, fox_attention(qt, kaug, vt, ft), fox_w_o)
    hs, fox_k_s, fox_v_s, fox_lf_s = fox_sample(hs, row(g_mix[0]), fox_w_q, fox_w_k, fox_w_v, fox_w_f, fox_b_f,
                                                fox_w_o, cache_fox_k, cache_fox_v, cache_fox_logf)
    hp = dense_ffn(hp, p_prompt, 0)
    hs = dense_ffn(hs, p_sample, 0)

    conf_args = (row(g_mix[1]), conf_w_pw1, conf_b_pw1, conf_w_dw, conf_b_dw, conf_g_ln, conf_b_ln,
                 conf_w_pw2, conf_b_pw2)
    hp, conf_p = conformer(hp, jnp.zeros((b, CONF_K - 1, d), F32), *conf_args)
    hs, conf_s = conformer(hs, state_conf_conv, *conf_args)
    hp = moe_ffn(hp, p_prompt, 1, False)
    hs = moe_ffn(hs, p_sample, 1, False)

    ssd_args = (row(g_mix[2]), ssd_w_in, ssd_w_conv, ssd_b_conv, ssd_dt_bias, ssd_a_log, ssd_d, ssd_g_norm,
                ssd_w_out)
    hp, ssdc_p, ssdh_p = ssd_mixer(hp, jnp.zeros((b, SSD_CONV_K - 1, SSD_CONV_DIM), F32),
                                   jnp.zeros((b, SSD_HEADS, SSD_HD, SSD_N), F32), *ssd_args)
    hs, ssdc_s, ssdh_s = ssd_mixer(hs, state_ssd_conv, state_ssd, *ssd_args)
    hp = dense_ffn(hp, p_prompt, 2)
    hs = dense_ffn(hs, p_sample, 2)

    pool_args = (row(g_mix[3]), pool_w, pool_b, pool_scale)
    hp, pool_p = pool_mixer(hp, jnp.zeros((b, POOL_HIST, d), F32), 0, *pool_args)
    hs, pool_s = pool_mixer(hs, state_pool, past, *pool_args)
    y_prompt = moe_ffn(hp, p_prompt, 3, True)
    y_sample = moe_ffn(hs, p_sample, 3, True)

    heads = lambda a: a.reshape(a.shape[0], a.shape[1], FOX_HEADS, FOX_HD)
    return (y_prompt, y_sample, heads(fox_k_p), heads(fox_v_p), fox_lf_p, heads(fox_k_s), heads(fox_v_s), fox_lf_s,
            conf_p, conf_s, ssdc_p, ssdc_s, ssdh_p, ssdh_s, pool_p, pool_s)
```

```python
import functools

import jax
import jax.numpy as jnp
from jax import lax
from jax.experimental import pallas as pl
from jax.experimental.pallas import tpu as pltpu

F32 = jnp.float32
BF16 = jnp.bfloat16

D_MODEL = 1024
EPS = 1e-6
PLE_DIM = 256
FOX_HEADS = 16
FOX_HD = 64
FOX_PAIRS = FOX_HEADS // 2
FOX_VROWS = FOX_HD + 16
CONF_K = 31
CONF_HALO = 32
SSD_INNER = 2048
SSD_HD = 64
SSD_HEADS = 32
SSD_GROUPS = 4
SSD_HPG = SSD_HEADS // SSD_GROUPS
SSD_N = 128
SSD_CONV_K = 4
SSD_CONV_DIM = SSD_INNER + 2 * SSD_GROUPS * SSD_N
SSD_HALO = 8
POOL_WINDOWS = (2, 4, 8, 16)
POOL_GROUP = D_MODEL // len(POOL_WINDOWS)
POOL_HIST = 15
POOL_HALO = 16
N_EXPERTS = 8
D_FF = 2816
D_FF_EXPERT = 1408
LANES = 128

VMEM_LIMIT_BYTES = 56 * 1024 * 1024
TOKEN_TILE = 512
ATTN_BLOCK = 512
ATTN_STRIP = 256
ATTN_KEYS = 512
NORM_SLACK = 1.01
DEAD_LOGIT_GAP = 106.0
SSD_CHUNK = 256
FFN_CHUNK = 256
MOE_CHUNK = LANES


def _cparams(*sem):
    return pltpu.CompilerParams(dimension_semantics=sem, vmem_limit_bytes=VMEM_LIMIT_BYTES)


def _resident(shape):
    n = len(shape)
    return pl.BlockSpec(shape, lambda *_: (0,) * n, pipeline_mode=pl.Buffered(1))


def _dot(a, b):
    return jnp.dot(a, b, preferred_element_type=F32)


def _dot_nt(a, b):
    return lax.dot_general(a, b, (((1,), (1,)), ((), ())), preferred_element_type=F32)


def _rms(x, g):
    return x * lax.rsqrt(jnp.mean(x * x, axis=-1, keepdims=True) + EPS) * g


def _sigmoid(x):
    return 1.0 / (1.0 + jnp.exp(-x))


def _silu(x):
    return x * _sigmoid(x)


def _softplus(x):
    return jnp.maximum(x, 0.0) + jnp.log1p(jnp.exp(-jnp.abs(x)))


def _split3(x):
    p1 = x.astype(BF16)
    r1 = x - p1.astype(F32)
    p2 = r1.astype(BF16)
    r2 = r1 - p2.astype(F32)
    return p1, p2, r2.astype(BF16)


def _dot3(a_f32, b_bf16):
    p1, p2, p3 = _split3(a_f32)
    return (_dot(p1, b_bf16) + _dot(p2, b_bf16)) + _dot(p3, b_bf16)


def _dot3_left(a_bf16, b_f32):
    p1, p2, p3 = _split3(b_f32)
    return (_dot(a_bf16, p1) + _dot(a_bf16, p2)) + _dot(a_bf16, p3)


def _rep_rows(a, n):
    s, w = a.shape
    return jnp.broadcast_to(a[:, None, :], (s, n, w)).reshape(s * n, w)


def _tile_rows(a, n):
    h, w = a.shape
    return jnp.broadcast_to(a[None], (n, h, w)).reshape(n * h, w)


def _tril(n):
    r = lax.broadcasted_iota(jnp.int32, (n, n), 0)
    c = lax.broadcasted_iota(jnp.int32, (n, n), 1)
    return jnp.where(c <= r, 1.0, 0.0).astype(BF16)


def _ple_apply(h1, p, g_ple, w_gate, w_up):
    hn = _rms(h1, g_ple).astype(BF16)
    gate = _sigmoid(_dot(hn, w_gate))
    return h1 + gate * _dot(p.astype(BF16), w_up)


def _ffn_ple_kernel(h_ref, p_ref, gf_ref, wg_ref, wu_ref, wd_ref, gp_ref, wpg_ref, wpu_ref, o_ref):
    x = h_ref[...]
    xn = _rms(x, gf_ref[...]).astype(BF16)
    acc = x
    for c in range(D_FF // FFN_CHUNK):
        sl = slice(c * FFN_CHUNK, (c + 1) * FFN_CHUNK)
        hh = (_silu(_dot(xn, wg_ref[:, sl])) * _dot(xn, wu_ref[:, sl])).astype(BF16)
        acc = acc + _dot(hh, wd_ref[sl, :])
    o_ref[...] = _ple_apply(acc, p_ref[...], gp_ref[...], wpg_ref[...], wpu_ref[...])


def ffn_ple(h, p, g_ffn, wg, wu, wd, g_ple, wpg, wpu):
    m = h.shape[0]
    tm = min(TOKEN_TILE, m)
    row = lambda w: pl.BlockSpec((tm, w), lambda i: (i, 0))
    return pl.pallas_call(
        _ffn_ple_kernel,
        grid=(m // tm,),
        in_specs=[row(D_MODEL), row(PLE_DIM), _resident((1, D_MODEL)),
                  _resident((D_MODEL, D_FF)), _resident((D_MODEL, D_FF)), _resident((D_FF, D_MODEL)),
                  _resident((1, D_MODEL)), _resident((D_MODEL, D_MODEL)), _resident((PLE_DIM, D_MODEL))],
        out_specs=row(D_MODEL),
        out_shape=jax.ShapeDtypeStruct((m, D_MODEL), F32),
        compiler_params=_cparams("parallel"),
        name="ffn_ple",
    )(h, p, g_ffn, wg, wu, wd, g_ple, wpg, wpu)


def _moe_ple_kernel(h_ref, p_ref, gf_ref, wr_ref, wg_ref, wu_ref, wd_ref, gp_ref, wpg_ref, wpu_ref, gfin_ref,
                    o_ref, xn_s, comb_s, combt_s, acc_s, *, final_norm):
    e = pl.program_id(1)

    @pl.when(e == 0)
    def _():
        x = h_ref[...]
        xn = _rms(x, gf_ref[...]).astype(BF16)
        xn_s[...] = xn
        acc_s[...] = x
        logits = _dot(xn, wr_ref[...])
        lane = lax.broadcasted_iota(jnp.int32, logits.shape, 1).astype(F32)
        neg = jnp.float32(-jnp.inf)
        lg = jnp.where(lane < N_EXPERTS, logits, neg)
        m1 = jnp.max(lg, axis=-1, keepdims=True)
        i1 = jnp.min(jnp.where(lg == m1, lane, float(LANES)), axis=-1, keepdims=True)
        lg2 = jnp.where(lane == i1, neg, lg)
        m2 = jnp.max(lg2, axis=-1, keepdims=True)
        i2 = jnp.min(jnp.where(lg2 == m2, lane, float(LANES)), axis=-1, keepdims=True)
        e2 = jnp.exp(m2 - m1)
        den = 1.0 + e2
        comb = jnp.where(lane == i1, 1.0 / den, jnp.where(lane == i2, e2 / den, 0.0))
        comb_s[...] = comb
        combt_s[...] = comb.T[:N_EXPERTS, :]

    tm = xn_s.shape[0]
    lane = lax.broadcasted_iota(jnp.int32, comb_s.shape, 1)
    gate_col = jnp.sum(jnp.where(lane == e, comb_s[...], 0.0), axis=-1, keepdims=True)
    gate_row = combt_s[pl.ds(e, 1), :]
    sel_col = jnp.where(gate_col > 0.0, 1.0, 0.0)
    sel_row = jnp.where(gate_row > 0.0, 1.0, 0.0)
    r = lax.broadcasted_iota(jnp.int32, (tm, tm), 0)
    c = lax.broadcasted_iota(jnp.int32, (tm, tm), 1)
    upper = jnp.where(r <= c, 1.0, 0.0).astype(BF16)
    rank_row = _dot(jnp.broadcast_to(sel_row, (8, tm)).astype(BF16), upper)[0:1, :]
    rank_col = _dot(_tril(tm), jnp.broadcast_to(sel_col, (tm, LANES)).astype(BF16))
    n_chunks = (jnp.max(rank_row).astype(jnp.int32) + (MOE_CHUNK - 1)) // MOE_CHUNK
    slot_row = lax.broadcasted_iota(jnp.int32, (MOE_CHUNK, tm), 0).astype(F32) + 1.0
    slot_col = lax.broadcasted_iota(jnp.int32, (tm, LANES), 1).astype(F32) + 1.0

    def chunk(ci, carry):
        base = (ci * MOE_CHUNK).astype(F32)
        gather = jnp.where((sel_row > 0.0) & (rank_row == slot_row + base), 1.0, 0.0).astype(BF16)
        xg = _dot(gather, xn_s[...]).astype(BF16)
        hh = (_silu(_dot(xg, wg_ref[0])) * _dot(xg, wu_ref[0])).astype(BF16)
        y = _dot(hh, wd_ref[0])
        scatter = jnp.where((sel_col > 0.0) & (rank_col == slot_col + base), 1.0, 0.0).astype(BF16)
        acc_s[...] += gate_col * _dot(scatter, y.astype(BF16))
        return carry

    lax.fori_loop(0, n_chunks, chunk, 0)

    @pl.when(e == N_EXPERTS - 1)
    def _():
        h2 = _ple_apply(acc_s[...], p_ref[...], gp_ref[...], wpg_ref[...], wpu_ref[...])
        if final_norm:
            h2 = _rms(h2, gfin_ref[...])
        o_ref[...] = h2


def moe_ple(h, p, g_ffn, wr, wg, wu, wd, g_ple, wpg, wpu, g_final, final_norm):
    m = h.shape[0]
    tm = min(TOKEN_TILE, m)
    row = lambda w: pl.BlockSpec((tm, w), lambda i, e: (i, 0))
    expert = lambda a, b: pl.BlockSpec((1, a, b), lambda i, e: (e, 0, 0))
    return pl.pallas_call(
        functools.partial(_moe_ple_kernel, final_norm=final_norm),
        grid=(m // tm, N_EXPERTS),
        in_specs=[row(D_MODEL), row(PLE_DIM), _resident((1, D_MODEL)), _resident((D_MODEL, LANES)),
                  expert(D_MODEL, D_FF_EXPERT), expert(D_MODEL, D_FF_EXPERT), expert(D_FF_EXPERT, D_MODEL),
                  _resident((1, D_MODEL)), _resident((D_MODEL, D_MODEL)), _resident((PLE_DIM, D_MODEL)),
                  _resident((1, D_MODEL))],
        out_specs=row(D_MODEL),
        out_shape=jax.ShapeDtypeStruct((m, D_MODEL), F32),
        scratch_shapes=[pltpu.VMEM((tm, D_MODEL), BF16), pltpu.VMEM((tm, LANES), F32),
                        pltpu.VMEM((N_EXPERTS, tm), F32), pltpu.VMEM((tm, D_MODEL), F32)],
        compiler_params=_cparams("parallel", "arbitrary"),
        name="moe_ple",
    )(h, p, g_ffn, wr, wg, wu, wd, g_ple, wpg, wpu, g_final)


def _log_sigmoid(z):
    return jnp.minimum(z, 0.0) - jnp.log1p(jnp.exp(-jnp.abs(z)))


def _fox_proj_kernel(x_ref, g_ref, wqt_ref, wk_ref, wv_ref, wf_ref, bf_ref, sel_ref, ones_ref, hsel_ref,
                     k_out, v_out, lf_out, qt_out, vt_out, kaug_out, ft_out, qn_out, kn_out, carry):
    tm = x_ref.shape[1]

    @pl.when(pl.program_id(1) == 0)
    def _():
        carry[...] = jnp.zeros_like(carry)

    xn = _rms(x_ref[0], g_ref[...])
    xb = xn.astype(BF16)
    xnt = xn.T.astype(BF16)
    k = _dot(xb, wk_ref[...])
    v = _dot(xb, wv_ref[...])
    k_out[0] = k
    v_out[0] = v
    qtb = (_dot(wqt_ref[...], xnt) * (FOX_HD ** -0.5)).astype(BF16)
    qt_out[0] = qtb
    kb = k.astype(BF16)
    qsq = qtb.astype(F32)
    ksq = kb.astype(F32)
    qn2 = _dot(hsel_ref[...], (qsq * qsq).astype(BF16))
    kn2 = _dot_nt((ksq * ksq).astype(BF16), hsel_ref[...])
    qn_out[0, 0] = jnp.broadcast_to(jnp.max(qn2, axis=1, keepdims=True), (LANES, LANES))[:FOX_HEADS]
    kn_out[0, 0] = jnp.broadcast_to(jnp.max(kn2, axis=0, keepdims=True), (8, LANES))
    vt = v.T.astype(BF16)
    ones_rows = jnp.ones((FOX_VROWS - FOX_HD, tm), BF16)
    for h in range(FOX_HEADS):
        vt_out[0, 0, FOX_VROWS * h:FOX_VROWS * h + FOX_HD, :] = vt[FOX_HD * h:FOX_HD * (h + 1), :]
        vt_out[0, 0, FOX_VROWS * h + FOX_HD:FOX_VROWS * (h + 1), :] = ones_rows
    lf = _log_sigmoid(_dot(xb, wf_ref[...]) + bf_ref[...])
    lf_out[0] = lf[:, :FOX_HEADS]
    f = _dot3_left(_tril(tm), lf) + carry[...]
    carry[...] = f[tm - 1:tm, :]
    ft_out[0] = f.T[:FOX_HEADS, :]
    p1, p2, p3 = _split3(f)
    aug = ((_dot(p1, sel_ref[0]) + _dot(p2, sel_ref[1])) + _dot(p3, sel_ref[2]) + ones_ref[...]).astype(BF16)
    for p in range(FOX_PAIRS):
        kaug_out[0, :, 2 * LANES * p:2 * LANES * p + LANES] = kb[:, LANES * p:LANES * (p + 1)]
        kaug_out[0, :, 2 * LANES * p + LANES:2 * LANES * (p + 1)] = aug[:, LANES * p:LANES * (p + 1)]


def _fox_aug_constants():
    h = jnp.arange(FOX_HEADS)
    sel = jnp.zeros((3, LANES, FOX_PAIRS * LANES), F32)
    for r in range(3):
        sel = sel.at[r, h, LANES * (h // 2) + 16 * (h % 2) + r].set(1.0)
    p = jnp.arange(FOX_PAIRS)
    ones = jnp.zeros((1, FOX_PAIRS * LANES), F32)
    for r in range(3):
        ones = ones.at[0, LANES * p + 32 + r].set(1.0)
    ch = jnp.arange(D_MODEL)
    hsel = jnp.zeros((LANES, D_MODEL), F32).at[ch // FOX_HD, ch].set(1.0)
    return sel.astype(BF16), ones, hsel.astype(BF16)


def fox_proj(x, g, wq, wk, wv, wf, bf):
    b, t, d = x.shape
    tm = min(ATTN_BLOCK, t)
    sel, ones, hsel = _fox_aug_constants()
    wf_pad = jnp.zeros((d, LANES), F32).at[:, :FOX_HEADS].set(wf).astype(BF16)
    bf_pad = jnp.zeros((1, LANES), F32).at[0, :FOX_HEADS].set(bf)
    tile = lambda w: pl.BlockSpec((1, tm, w), lambda bi, ti: (bi, ti, 0))
    return pl.pallas_call(
        _fox_proj_kernel,
        grid=(b, t // tm),
        in_specs=[tile(d), _resident((1, d)), _resident((d, d)), _resident((d, d)), _resident((d, d)),
                  _resident((d, LANES)), _resident((1, LANES)), _resident((3, LANES, FOX_PAIRS * LANES)),
                  _resident((1, FOX_PAIRS * LANES)), _resident((LANES, d))],
        out_specs=[tile(d), tile(d), tile(FOX_HEADS),
                   pl.BlockSpec((1, d, tm), lambda bi, ti: (bi, 0, ti)),
                   pl.BlockSpec((1, 1, FOX_HEADS * FOX_VROWS, tm), lambda bi, ti: (bi, ti, 0, 0)),
                   tile(2 * d),
                   pl.BlockSpec((1, FOX_HEADS, tm), lambda bi, ti: (bi, 0, ti)),
                   pl.BlockSpec((1, 1, FOX_HEADS, LANES), lambda bi, ti: (bi, ti, 0, 0)),
                   pl.BlockSpec((1, 1, 8, LANES), lambda bi, ti: (bi, ti, 0, 0))],
        out_shape=[jax.ShapeDtypeStruct((b, t, d), F32), jax.ShapeDtypeStruct((b, t, d), F32),
                   jax.ShapeDtypeStruct((b, t, FOX_HEADS), F32),
                   jax.ShapeDtypeStruct((b, d, t), BF16),
                   jax.ShapeDtypeStruct((b, t // tm, FOX_HEADS * FOX_VROWS, tm), BF16),
                   jax.ShapeDtypeStruct((b, t, 2 * d), BF16),
                   jax.ShapeDtypeStruct((b, FOX_HEADS, t), F32),
                   jax.ShapeDtypeStruct((b, t // tm, FOX_HEADS, LANES), F32),
                   jax.ShapeDtypeStruct((b, t // tm, 8, LANES), F32)],
        scratch_shapes=[pltpu.VMEM((1, LANES), F32)],
        compiler_params=_cparams("arbitrary", "arbitrary"),
        name="fox_proj",
    )(x, g, wq.T.astype(BF16), wk.astype(BF16), wv.astype(BF16), wf_pad, bf_pad, sel, ones, hsel)


def fox_dead_block_bounds(qn2, kn2, ft, blk):
    b, nb = qn2.shape[:2]
    qn = jnp.sqrt(qn2[..., 0] * NORM_SLACK).transpose(0, 2, 1)
    kn = jnp.sqrt(kn2[:, :, 0, :FOX_HEADS] * NORM_SLACK).transpose(0, 2, 1)
    kn_upto = lax.cummax(kn, axis=2)
    fq_first = ft[:, :, ::blk]
    fk_last = ft[:, :, blk - 1::blk]
    ub = qn[:, :, :, None] * kn_upto[:, :, None, :] + fq_first[:, :, :, None] - fk_last[:, :, None, :]
    ub = jnp.pad(ub, ((0, 0), (0, 0), (0, 0), (0, LANES - nb)), constant_values=-jnp.inf)
    return ub.reshape(b, FOX_PAIRS, 2, nb, LANES).transpose(0, 1, 3, 2, 4)


def _fox_attn_kernel(qt_ref, kaug_ref, vt_ref, ft_ref, ub_ref, ot_ref, qaug, m_s, acc, s_buf):
    tq = qt_ref.shape[2]
    tk = vt_ref.shape[3]
    i = pl.program_id(2)
    zeros = lambda r: jnp.zeros((r, tq), BF16)
    minus = jnp.full((16, tq), -1.0, BF16)
    row = lax.broadcasted_iota(jnp.int32, (16, tq), 0)
    for h in range(2):
        qaug[h, 64 * h:64 * (h + 1)] = qt_ref[0, 64 * h:64 * (h + 1), :]
        qaug[h, 64 * (1 - h):64 * (2 - h)] = zeros(64)
        qaug[h, LANES + 16 * h:LANES + 16 * (h + 1)] = minus
        qaug[h, LANES + 16 * (1 - h):LANES + 16 * (2 - h)] = zeros(16)
        p1, p2, p3 = _split3(ft_ref[0, 0, h:h + 1, :])
        blk = jnp.where(row == 0, p1.astype(F32),
                        jnp.where(row == 1, p2.astype(F32), jnp.where(row == 2, p3.astype(F32), 0.0)))
        qaug[h, LANES + 32:LANES + 48] = blk.astype(BF16)
        qaug[h, LANES + 48:2 * LANES] = zeros(LANES - 48)
    m_s[...] = jnp.full(m_s.shape, -jnp.inf, F32)
    acc[...] = jnp.zeros(acc.shape, F32)

    chains = [(h, c) for h in range(2) for c in range(tq // ATTN_STRIP)]

    def qk(n, j, slot):
        h, c = chains[n]
        qa = qaug[h, :, c * ATTN_STRIP:(c + 1) * ATTN_STRIP]
        for r in range(tk // ATTN_KEYS):
            kj = kaug_ref[0, pl.ds(pl.multiple_of(j * tk + r * ATTN_KEYS, ATTN_KEYS), ATTN_KEYS), :]
            s_buf[slot, n, r * ATTN_KEYS:(r + 1) * ATTN_KEYS, :] = _dot(kj, qa)

    def softmax(n, slot, masked):
        h, c = chains[n]
        qs = slice(c * ATTN_STRIP, (c + 1) * ATTN_STRIP)
        st = s_buf[slot, n]
        if masked:
            kpos = lax.broadcasted_iota(jnp.int32, st.shape, 0)
            qpos = lax.broadcasted_iota(jnp.int32, st.shape, 1) + c * ATTN_STRIP
            st = jnp.where(kpos <= qpos, st, -jnp.inf)
        m_prev = m_s[h, :, qs]
        m_new = jnp.maximum(m_prev, jnp.max(st, axis=0, keepdims=True))
        alpha = jnp.exp(m_prev - m_new)
        p = jnp.exp(st - m_new)
        m_s[h, :, qs] = m_new
        return p.astype(BF16), alpha

    def pv(n, j, p, alpha):
        h, c = chains[n]
        qs = slice(c * ATTN_STRIP, (c + 1) * ATTN_STRIP)
        acc[h, :, qs] = alpha * acc[h, :, qs] + _dot(vt_ref[0, j, FOX_VROWS * h:FOX_VROWS * (h + 1), :], p)

    def block(j, slot, masked, prefetch=True):
        jn = jnp.maximum(j - 1, 0)
        pa = {}
        if prefetch:
            for n in range(len(chains)):
                qk(n, jn, 1 - slot)
        for n in range(len(chains)):
            pa[n] = softmax(n, slot, masked)
            if n >= 1:
                pv(n - 1, j, *pa.pop(n - 1))
        pv(len(chains) - 1, j, *pa.pop(len(chains) - 1))

    for n in range(len(chains)):
        qk(n, i, 0)
    block(i, 0, True)

    lane = lax.broadcasted_iota(jnp.int32, (1, LANES), 1)
    live = lane < 0
    for h in range(2):
        m_low = jnp.min(m_s[h], axis=1, keepdims=True)
        live = live | (ub_ref[0, 0, 0, h:h + 1, :] >= m_low - DEAD_LOGIT_GAP)
    n_live = jnp.sum(jnp.where(live & (lane < i), 1.0, 0.0)).astype(jnp.int32)

    def pair(u, carry):
        j = i - 1 - 2 * u
        block(j, 1, False)
        block(j - 1, 0, False)
        return carry

    lax.fori_loop(0, n_live // 2, pair, 0)

    @pl.when(n_live % 2 == 1)
    def _():
        block(i - n_live, 1, False, prefetch=False)

    for h in range(2):
        ot_ref[0, 64 * h:64 * (h + 1), :] = (acc[h, :FOX_HD] / acc[h, FOX_HD:FOX_HD + 1]).astype(BF16)


def fox_attention(qt, kaug, vt, ft, ub):
    b, d, t = qt.shape
    blk = vt.shape[3]
    assert t // blk <= LANES
    ft = ft.reshape(b, FOX_PAIRS, 2, t)
    return pl.pallas_call(
        _fox_attn_kernel,
        grid=(b, FOX_PAIRS, t // blk),
        in_specs=[pl.BlockSpec((1, LANES, blk), lambda bi, p, i: (bi, p, i)),
                  pl.BlockSpec((1, t, 2 * LANES), lambda bi, p, i: (bi, 0, p)),
                  pl.BlockSpec((1, t // blk, 2 * FOX_VROWS, blk), lambda bi, p, i: (bi, 0, p, 0)),
                  pl.BlockSpec((1, 1, 2, blk), lambda bi, p, i: (bi, p, 0, i)),
                  pl.BlockSpec((1, 1, 1, 2, LANES), lambda bi, p, i: (bi, p, i, 0, 0))],
        out_specs=pl.BlockSpec((1, LANES, blk), lambda bi, p, i: (bi, p, i)),
        out_shape=jax.ShapeDtypeStruct((b, d, t), BF16),
        scratch_shapes=[pltpu.VMEM((2, 2 * LANES, blk), BF16), pltpu.VMEM((2, 1, blk), F32),
                        pltpu.VMEM((2, FOX_VROWS, blk), F32),
                        pltpu.VMEM((2, 2 * blk // ATTN_STRIP, blk, ATTN_STRIP), F32)],
        compiler_params=_cparams("parallel", "parallel", "arbitrary"),
        name="fox_attention",
    )(qt, kaug, vt, ft, ub)


def _out_proj_t_kernel(h_ref, ot_ref, w_ref, o_ref):
    o_ref[0] = h_ref[0] + pl.dot(ot_ref[0], w_ref[...], trans_a=True)


def out_proj_t(h, ot, w):
    b, t, d = h.shape
    tm = min(TOKEN_TILE, t)
    tile = pl.BlockSpec((1, tm, d), lambda bi, ti: (bi, ti, 0))
    return pl.pallas_call(
        _out_proj_t_kernel,
        grid=(b, t // tm),
        in_specs=[tile, pl.BlockSpec((1, d, tm), lambda bi, ti: (bi, 0, ti)), _resident((d, d))],
        out_specs=tile,
        out_shape=jax.ShapeDtypeStruct((b, t, d), F32),
        compiler_params=_cparams("parallel", "parallel"),
        name="fox_out_proj",
    )(h, ot, w.astype(BF16))


def _fox_sample_kernel(x_ref, g_ref, wq_ref, wk_ref, wv_ref, wf_ref, bf_ref, wo_ref, ck_ref, cv_ref, clf_ref,
                       h_out, k_out, v_out, lf_out):
    s = x_ref.shape[1]
    pl_len = ck_ref.shape[1]
    x = x_ref[0]
    xb = _rms(x, g_ref[...]).astype(BF16)
    q = _dot(xb, wq_ref[...]) * (FOX_HD ** -0.5)
    k = _dot(xb, wk_ref[...])
    v = _dot(xb, wv_ref[...])
    lf = _log_sigmoid(_dot(xb, wf_ref[...]) + bf_ref[...])
    k_out[0] = k
    v_out[0] = v
    lf_out[0] = lf[:, :FOX_HEADS]

    rows = s * FOX_HEADS
    rh = lax.broadcasted_iota(jnp.int32, (rows, D_MODEL), 0) % FOX_HEADS
    lh = lax.broadcasted_iota(jnp.int32, (rows, D_MODEL), 1) // FOX_HD
    head_lanes = rh == lh
    qbd = jnp.where(head_lanes, _rep_rows(q, FOX_HEADS), 0.0).astype(BF16)

    cblk = 256
    carry = jnp.zeros((1, LANES), F32)
    tri = _tril(cblk)
    f_parts = []
    for c in range(pl_len // cblk):
        fc = _dot3_left(tri, clf_ref[0, c * cblk:(c + 1) * cblk, :]) + carry
        carry = fc[cblk - 1:cblk, :]
        f_parts.append(fc)
    f_cache = jnp.concatenate(f_parts, axis=0)
    pad_rows = lambda a: jnp.concatenate([a, jnp.zeros((LANES - s, a.shape[1]), a.dtype)], axis=0)
    f_new = _dot3_left(_tril(LANES), pad_rows(lf)) + carry

    lane = lax.broadcasted_iota(jnp.int32, (rows, LANES), 1)
    row_head = lax.broadcasted_iota(jnp.int32, (rows, LANES), 0) % FOX_HEADS
    fq = jnp.sum(jnp.where(lane == row_head, _rep_rows(f_new[:s], FOX_HEADS), 0.0), axis=-1, keepdims=True)
    fk_cache = _tile_rows(f_cache.T[:FOX_HEADS, :], s)
    fk_new = _tile_rows(f_new.T[:FOX_HEADS, :], s)

    kc = 512
    qk = [_dot_nt(qbd, ck_ref[0, c * kc:(c + 1) * kc, :].astype(BF16)) for c in range(pl_len // kc)]
    s_cache = jnp.concatenate(qk, axis=1) + fq - fk_cache
    s_new = _dot_nt(qbd, pad_rows(k).astype(BF16)) + fq - fk_new
    qi = lax.broadcasted_iota(jnp.int32, (rows, LANES), 0) // FOX_HEADS
    kj = lax.broadcasted_iota(jnp.int32, (rows, LANES), 1)
    s_new = jnp.where(kj <= qi, s_new, -jnp.inf)
    m = jnp.maximum(jnp.max(s_cache, axis=-1, keepdims=True), jnp.max(s_new, axis=-1, keepdims=True))
    p_cache = jnp.exp(s_cache - m)
    p_new = jnp.exp(s_new - m)
    den = jnp.sum(p_cache, axis=-1, keepdims=True) + jnp.sum(p_new, axis=-1, keepdims=True)
    pv = _dot(p_new.astype(BF16), pad_rows(v).astype(BF16))
    pb = p_cache.astype(BF16)
    for c in range(pl_len // kc):
        pv = pv + _dot(pb[:, c * kc:(c + 1) * kc], cv_ref[0, c * kc:(c + 1) * kc, :].astype(BF16))
    o_heads = jnp.where(head_lanes, pv / den, 0.0).astype(BF16)
    pick = (lax.broadcasted_iota(jnp.int32, (s, rows), 1) // FOX_HEADS
            == lax.broadcasted_iota(jnp.int32, (s, rows), 0))
    o = _dot(jnp.where(pick, 1.0, 0.0).astype(BF16), o_heads)
    h_out[0] = x + _dot(o.astype(BF16), wo_ref[...])


def fox_sample(x, g, wq, wk, wv, wf, bf, wo, ck, cv, clf):
    b, s, d = x.shape
    p = ck.shape[1]
    wf_pad = jnp.zeros((d, LANES), F32).at[:, :FOX_HEADS].set(wf).astype(BF16)
    bf_pad = jnp.zeros((1, LANES), F32).at[0, :FOX_HEADS].set(bf)
    clf_pad = jnp.zeros((b, p, LANES), F32).at[:, :, :FOX_HEADS].set(clf)
    seq = lambda n, w: pl.BlockSpec((1, n, w), lambda bi: (bi, 0, 0))
    return pl.pallas_call(
        _fox_sample_kernel,
        grid=(b,),
        in_specs=[seq(s, d), _resident((1, d)), _resident((d, d)), _resident((d, d)), _resident((d, d)),
                  _resident((d, LANES)), _resident((1, LANES)), _resident((d, d)),
                  seq(p, d), seq(p, d), seq(p, LANES)],
        out_specs=[seq(s, d), seq(s, d), seq(s, d), seq(s, FOX_HEADS)],
        out_shape=[jax.ShapeDtypeStruct((b, s, d), F32)] * 3 + [jax.ShapeDtypeStruct((b, s, FOX_HEADS), F32)],
        compiler_params=_cparams("parallel"),
        name="fox_sample",
    )(x, g, wq.astype(BF16), wk.astype(BF16), wv.astype(BF16), wf_pad, bf_pad, wo.astype(BF16),
      ck.reshape(b, p, d), cv.reshape(b, p, d), clf_pad)


def _halo_specs(b, t, tm, d, halo):
    main = pl.BlockSpec((1, tm, d), lambda bi, ti: (bi, ti, 0))
    if t == tm:
        return [main]
    per = tm // halo
    return [main, pl.BlockSpec((1, halo, d), lambda bi, ti: (bi, jnp.maximum(ti * per - 1, 0), 0))]


def _conf_kernel(*refs, has_halo):
    if has_halo:
        x_ref, xh_ref, *refs = refs
    else:
        x_ref, *refs = refs
    (hist_ref, g_ref, w1_ref, b1_ref, wdw_ref, bdw_ref, gln_ref, bln_ref, w2_ref, b2_ref,
     o_ref, hs_ref, uext, ushift) = refs
    tm = x_ref.shape[1]

    def glu(rows):
        ag = _dot(_rms(rows, g_ref[...]).astype(BF16), w1_ref[...]) + b1_ref[...]
        return ag[:, :D_MODEL] * _sigmoid(ag[:, D_MODEL:])

    x = x_ref[0]
    if has_halo:
        prev = jnp.where(pl.program_id(1) == 0, hist_ref[0], glu(xh_ref[0]))
    else:
        prev = hist_ref[0]
    uext[0:CONF_HALO] = prev
    uext[CONF_HALO:CONF_HALO + tm] = glu(x)
    uext[CONF_HALO + tm:] = jnp.zeros((8, D_MODEL), F32)
    off = CONF_HALO - (CONF_K - 1)
    y = jnp.broadcast_to(bdw_ref[...], (tm, D_MODEL))
    for s in range(8):
        ushift[...] = uext[s:s + tm + CONF_HALO, :]
        for k in range(CONF_K):
            if (off + k) % 8 == s:
                a = off + k - s
                y = y + wdw_ref[k:k + 1, :] * ushift[a:a + tm, :]
    hs_ref[0] = uext[tm + off:tm + CONF_HALO, :]
    mu = jnp.mean(y, axis=-1, keepdims=True)
    yc = y - mu
    var = jnp.mean(yc * yc, axis=-1, keepdims=True)
    v = _silu(yc * lax.rsqrt(var + EPS) * gln_ref[...] + bln_ref[...])
    o_ref[0] = x + _dot(v.astype(BF16), w2_ref[...]) + b2_ref[...]


def conformer(x, hist, g, w1, b1, wdw, bdw, gln, bln, w2, b2):
    b, t, d = x.shape
    tm = min(TOKEN_TILE, t)
    xs = _halo_specs(b, t, tm, d, CONF_HALO)
    hist_pad = jnp.pad(hist, ((0, 0), (CONF_HALO - (CONF_K - 1), 0), (0, 0)))
    per_b = lambda n: pl.BlockSpec((1, n, d), lambda bi, ti: (bi, 0, 0))
    return pl.pallas_call(
        functools.partial(_conf_kernel, has_halo=len(xs) == 2),
        grid=(b, t // tm),
        in_specs=xs + [per_b(CONF_HALO), _resident((1, d)), _resident((d, 2 * d)), _resident((1, 2 * d)),
                       _resident((CONF_K, d)), _resident((1, d)), _resident((1, d)), _resident((1, d)),
                       _resident((d, d)), _resident((1, d))],
        out_specs=[pl.BlockSpec((1, tm, d), lambda bi, ti: (bi, ti, 0)), per_b(CONF_K - 1)],
        out_shape=[jax.ShapeDtypeStruct((b, t, d), F32), jax.ShapeDtypeStruct((b, CONF_K - 1, d), F32)],
        scratch_shapes=[pltpu.VMEM((CONF_HALO + tm + 8, d), F32),
                        pltpu.VMEM((CONF_HALO + tm, d), F32)],
        compiler_params=_cparams("parallel", "arbitrary"),
        name="conformer",
    )(*([x] * len(xs)), hist_pad, g, w1.astype(BF16), b1.reshape(1, -1), wdw, bdw.reshape(1, -1),
      gln.reshape(1, -1), bln.reshape(1, -1), w2.astype(BF16), b2.reshape(1, -1))


def _pool_kernel(*refs, has_halo, pos0):
    if has_halo:
        x_ref, xh_ref, *refs = refs
    else:
        x_ref, *refs = refs
    hist_ref, g_ref, w_ref, b_ref, scale_ref, o_ref, hs_ref, ext = refs
    tm = x_ref.shape[1]
    i = pl.program_id(1)
    x = x_ref[0]
    xn = _rms(x, g_ref[...])
    if has_halo:
        prev = jnp.where(i == 0, hist_ref[0], _rms(xh_ref[0], g_ref[...]))
    else:
        prev = hist_ref[0]
    ext[0:POOL_HALO] = prev
    ext[POOL_HALO:] = xn
    hs_ref[0] = ext[tm + POOL_HALO - POOL_HIST:tm + POOL_HALO, :]
    pos = pos0 + i * tm + lax.broadcasted_iota(jnp.int32, (tm, 1), 0)
    ys = []
    for gi, wlen in enumerate(POOL_WINDOWS):
        lanes = slice(gi * POOL_GROUP, (gi + 1) * POOL_GROUP)
        s = ext[POOL_HALO:POOL_HALO + tm, lanes]
        for k in range(1, wlen):
            s = s + ext[POOL_HALO - k:POOL_HALO - k + tm, lanes]
        cnt = jnp.minimum(pos + 1, wlen).astype(F32)
        dg = s / cnt - xn[:, lanes]
        ys.append(_dot(dg.astype(BF16), w_ref[gi]))
    y = (jnp.concatenate(ys, axis=1) + b_ref[...]) * scale_ref[...]
    o_ref[0] = x + y


def pool_mixer(x, hist, pos0, g, w, bias, scale):
    b, t, d = x.shape
    tm = min(TOKEN_TILE, t)
    xs = _halo_specs(b, t, tm, d, POOL_HALO)
    hist_pad = jnp.pad(hist, ((0, 0), (POOL_HALO - POOL_HIST, 0), (0, 0)))
    per_b = lambda n: pl.BlockSpec((1, n, d), lambda bi, ti: (bi, 0, 0))
    ng = len(POOL_WINDOWS)
    return pl.pallas_call(
        functools.partial(_pool_kernel, has_halo=len(xs) == 2, pos0=pos0),
        grid=(b, t // tm),
        in_specs=xs + [per_b(POOL_HALO), _resident((1, d)), _resident((ng, POOL_GROUP, POOL_GROUP)),
                       _resident((1, d)), _resident((1, d))],
        out_specs=[pl.BlockSpec((1, tm, d), lambda bi, ti: (bi, ti, 0)), per_b(POOL_HIST)],
        out_shape=[jax.ShapeDtypeStruct((b, t, d), F32), jax.ShapeDtypeStruct((b, POOL_HIST, d), F32)],
        scratch_shapes=[pltpu.VMEM((POOL_HALO + tm, d), F32)],
        compiler_params=_cparams("parallel", "arbitrary"),
        name="pool_mixer",
    )(*([x] * len(xs)), hist_pad, g, w.astype(BF16), bias.reshape(1, -1), scale.reshape(1, -1))


def _ssd_kernel(*refs, has_halo, valid):
    if has_halo:
        x_ref, xh_ref, *refs = refs
    else:
        x_ref, *refs = refs
    (chist_ref, h0_ref, g_ref, wz_ref, wxbc_ref, wdt_ref, wconv_ref, bconv_ref, dtb_ref, a_ref, dskip_ref,
     gn_ref, wout_ref, exp_ref, expt_ref, o_ref, cs_out_ref, h_out_ref, ext, hstate) = refs
    L = x_ref.shape[1]
    i = pl.program_id(1)
    gw = SSD_HPG * SSD_HD
    gn = SSD_GROUPS * SSD_N

    @pl.when(i == 0)
    def _():
        hstate[...] = h0_ref[0]

    x = x_ref[0]
    xb = _rms(x, g_ref[...]).astype(BF16)
    z = _dot(xb, wz_ref[...])
    dtr = _dot(xb, wdt_ref[...])
    if has_halo:
        prev = jnp.where(i == 0, chist_ref[0], _dot(_rms(xh_ref[0], g_ref[...]).astype(BF16), wxbc_ref[...]))
    else:
        prev = chist_ref[0]
    ext[0:SSD_HALO] = prev
    ext[SSD_HALO:] = _dot(xb, wxbc_ref[...])
    off = SSD_HALO - (SSD_CONV_K - 1)
    y = jnp.broadcast_to(bconv_ref[...], (L, SSD_CONV_DIM))
    for k in range(SSD_CONV_K):
        y = y + wconv_ref[k:k + 1, :] * ext[off + k:off + k + L, :]
    cs_out_ref[0] = ext[valid + off:valid + SSD_HALO, :]
    xbc = _silu(y)
    xs = xbc[:, :SSD_INNER]
    bm = xbc[:, SSD_INNER:SSD_INNER + gn].astype(BF16)
    cm = xbc[:, SSD_INNER + gn:].astype(BF16)

    head_lane = lax.broadcasted_iota(jnp.int32, (L, LANES), 1) < SSD_HEADS
    live = head_lane
    if valid < L:
        live = live & (lax.broadcasted_iota(jnp.int32, (L, LANES), 0) < valid)
    dt = jnp.where(live, _softplus(dtr + dtb_ref[...]), 0.0)
    cs = _dot3_left(_tril(L), dt * a_ref[...])
    cst = cs.T
    dtt = dt.T
    to_end = jnp.exp(cs[L - 1:L, :] - cs) * dt
    td_x = _dot3(to_end, exp_ref[...])
    ecs_x = _dot3(jnp.exp(cs), exp_ref[...])
    chunk_decay = jnp.broadcast_to(jnp.exp(cst[:, L - 1:L]), (LANES, LANES))
    xs_b = xs.astype(BF16)
    xtd = xs * td_x
    r = lax.broadcasted_iota(jnp.int32, (L, L), 0)
    c = lax.broadcasted_iota(jnp.int32, (L, L), 1)
    causal = c <= r
    low_half = lax.broadcasted_iota(jnp.int32, (L, 2 * SSD_HD), 1) < SSD_HD

    y_groups = []
    for g in range(SSD_GROUPS):
        cg = cm[:, g * SSD_N:(g + 1) * SSD_N]
        bg = bm[:, g * SSD_N:(g + 1) * SSD_N]
        cb = _dot_nt(cg, bg)
        hg = hstate[g]
        y_off = _dot_nt(cg, hg.astype(BF16)) * ecs_x[:, g * gw:(g + 1) * gw]
        pairs = []
        for pr in range(SSD_HPG // 2):
            lo = g * gw + pr * 2 * SSD_HD
            xpair = xs_b[:, lo:lo + 2 * SSD_HD]
            acc = None
            for hh in range(2):
                e = g * SSD_HPG + 2 * pr + hh
                seg = cs[:, e:e + 1] - cst[e:e + 1, :]
                w = cb * jnp.exp(jnp.where(causal, seg, -jnp.inf)) * dtt[e:e + 1, :]
                xm = jnp.where(low_half == (hh == 0), xpair, jnp.zeros_like(xpair))
                term = _dot(w.astype(BF16), xm)
                acc = term if acc is None else acc + term
            pairs.append(acc)
        y_groups.append(jnp.concatenate(pairs, axis=1) + y_off)
        s_g = _dot(xtd[:, g * gw:(g + 1) * gw].T.astype(BF16), bg)
        hstate[g] = hg * _dot3_left(expt_ref[g], chunk_decay) + s_g

    ys = (jnp.concatenate(y_groups, axis=1) + dskip_ref[...] * xs) * _silu(z)
    normed = []
    for g in range(SSD_GROUPS):
        yg = ys[:, g * gw:(g + 1) * gw]
        normed.append(yg * lax.rsqrt(jnp.mean(yg * yg, axis=-1, keepdims=True) + EPS))
    yn = jnp.concatenate(normed, axis=1) * gn_ref[...]
    o_ref[0] = x + _dot(yn.astype(BF16), wout_ref[...])

    @pl.when(i == pl.num_programs(1) - 1)
    def _():
        h_out_ref[0] = hstate[...]


def _ssd_expand_constants():
    e = jnp.arange(SSD_HEADS)
    ch = jnp.arange(SSD_INNER)
    exp = jnp.zeros((LANES, SSD_INNER), F32).at[ch // SSD_HD, ch].set(1.0)
    gw = SSD_HPG * SSD_HD
    rows = jnp.arange(gw)
    expt = jnp.zeros((SSD_GROUPS, gw, LANES), F32)
    for g in range(SSD_GROUPS):
        expt = expt.at[g, rows, g * SSD_HPG + rows // SSD_HD].set(1.0)
    del e
    return exp.astype(BF16), expt.astype(BF16)


def ssd_mixer(x, conv_hist, h0, g, w_in, w_conv, b_conv, dt_bias, a_log, d_skip, g_norm, w_out):
    b, t, d = x.shape
    valid = t
    if t < LANES:
        x = jnp.pad(x, ((0, 0), (0, LANES - t), (0, 0)))
    tp = x.shape[1]
    L = min(SSD_CHUNK, tp)
    valid = L if tp > L else valid
    xs = _halo_specs(b, tp, L, d, SSD_HALO)
    gw = SSD_HPG * SSD_HD
    hist_pad = jnp.pad(conv_hist, ((0, 0), (SSD_HALO - (SSD_CONV_K - 1), 0), (0, 0)))
    wz = w_in[:, :SSD_INNER].astype(BF16)
    wxbc = w_in[:, SSD_INNER:SSD_INNER + SSD_CONV_DIM].astype(BF16)
    pad_heads = lambda v: jnp.zeros((v.shape[0], LANES), F32).at[:, :SSD_HEADS].set(v)
    wdt = pad_heads(w_in[:, SSD_INNER + SSD_CONV_DIM:]).astype(BF16)
    dtb = pad_heads(dt_bias.reshape(1, -1))
    a = pad_heads(-jnp.exp(a_log.astype(F32)).reshape(1, -1))
    dskip = jnp.repeat(d_skip, SSD_HD).reshape(1, -1)
    exp, expt = _ssd_expand_constants()
    per_b3 = lambda n, w: pl.BlockSpec((1, n, w), lambda bi, ti: (bi, 0, 0))
    state = pl.BlockSpec((1, SSD_GROUPS, gw, SSD_N), lambda bi, ti: (bi, 0, 0, 0))
    out, conv_new, h_new = pl.pallas_call(
        functools.partial(_ssd_kernel, has_halo=len(xs) == 2, valid=valid),
        grid=(b, tp // L),
        in_specs=xs + [per_b3(SSD_HALO, SSD_CONV_DIM), state, _resident((1, d)),
                       _resident((d, SSD_INNER)), _resident((d, SSD_CONV_DIM)), _resident((d, LANES)),
                       _resident((SSD_CONV_K, SSD_CONV_DIM)), _resident((1, SSD_CONV_DIM)),
                       _resident((1, LANES)), _resident((1, LANES)), _resident((1, SSD_INNER)),
                       _resident((1, SSD_INNER)), _resident((SSD_INNER, d)),
                       _resident((LANES, SSD_INNER)), _resident((SSD_GROUPS, gw, LANES))],
        out_specs=[pl.BlockSpec((1, L, d), lambda bi, ti: (bi, ti, 0)),
                   per_b3(SSD_CONV_K - 1, SSD_CONV_DIM), state],
        out_shape=[jax.ShapeDtypeStruct((b, tp, d), F32),
                   jax.ShapeDtypeStruct((b, SSD_CONV_K - 1, SSD_CONV_DIM), F32),
                   jax.ShapeDtypeStruct((b, SSD_GROUPS, gw, SSD_N), F32)],
        scratch_shapes=[pltpu.VMEM((SSD_HALO + L, SSD_CONV_DIM), F32), pltpu.VMEM((SSD_GROUPS, gw, SSD_N), F32)],
        compiler_params=_cparams("parallel", "arbitrary"),
        name="ssd_mixer",
    )(*([x] * len(xs)), hist_pad, h0.reshape(b, SSD_GROUPS, gw, SSD_N), g, wz, wxbc, wdt, w_conv,
      b_conv.reshape(1, -1), dtb, a, dskip, g_norm.reshape(1, -1), w_out.astype(BF16), exp, expt)
    return out[:, :t], conv_new, h_new.reshape(b, SSD_HEADS, SSD_HD, SSD_N)


def kernel(x_prompt, x_sample, cache_fox_k, cache_fox_v, cache_fox_logf, state_conf_conv, state_ssd_conv, state_ssd, state_pool, p_prompt, p_sample, g_mix, g_ffn, g_ple, g_final, fox_w_q, fox_w_k, fox_w_v, fox_w_f, fox_b_f, fox_w_o, conf_w_pw1, conf_b_pw1, conf_w_dw, conf_b_dw, conf_g_ln, conf_b_ln, conf_w_pw2, conf_b_pw2, ssd_w_in, ssd_w_conv, ssd_b_conv, ssd_dt_bias, ssd_a_log, ssd_d, ssd_g_norm, ssd_w_out, pool_w, pool_b, pool_scale, ffn_w_gate, ffn_w_up, ffn_w_down, moe_w_router, moe_w_gate, moe_w_up, moe_w_down, ple_w_up, ple_w_gate):
    b, t, d = x_prompt.shape
    bs, s, _ = x_sample.shape
    past = cache_fox_k.shape[1]
    row = lambda v: v.reshape(1, -1)
    bf = lambda w: w.astype(BF16)
    flat = lambda a: a.reshape(-1, a.shape[-1])
    hp, hs = x_prompt, x_sample

    def dense_ffn(h, p, i):
        j = i // 2
        out = ffn_ple(flat(h), flat(p[i]), row(g_ffn[i]), bf(ffn_w_gate[j]), bf(ffn_w_up[j]), bf(ffn_w_down[j]),
                      row(g_ple[i]), bf(ple_w_gate[i]), bf(ple_w_up[i]))
        return out.reshape(h.shape)

    def moe_ffn(h, p, i, final_norm):
        j = i // 2
        wr = jnp.zeros((d, LANES), F32).at[:, :N_EXPERTS].set(moe_w_router[j]).astype(BF16)
        out = moe_ple(flat(h), flat(p[i]), row(g_ffn[i]), wr, bf(moe_w_gate[j]), bf(moe_w_up[j]),
                      bf(moe_w_down[j]), row(g_ple[i]), bf(ple_w_gate[i]), bf(ple_w_up[i]), row(g_final),
                      final_norm)
        return out.reshape(h.shape)

    fox_k_p, fox_v_p, fox_lf_p, qt, vt, kaug, ft, qn2, kn2 = fox_proj(hp, row(g_mix[0]), fox_w_q, fox_w_k,
                                                                      fox_w_v, fox_w_f, fox_b_f)
    ub = fox_dead_block_bounds(qn2, kn2, ft, vt.shape[3])
    hp = out_proj_t(hp, fox_attention(qt, kaug, vt, ft, ub), fox_w_o)
    hs, fox_k_s, fox_v_s, fox_lf_s = fox_sample(hs, row(g_mix[0]), fox_w_q, fox_w_k, fox_w_v, fox_w_f, fox_b_f,
                                                fox_w_o, cache_fox_k, cache_fox_v, cache_fox_logf)
    hp = dense_ffn(hp, p_prompt, 0)
    hs = dense_ffn(hs, p_sample, 0)

    conf_args = (row(g_mix[1]), conf_w_pw1, conf_b_pw1, conf_w_dw, conf_b_dw, conf_g_ln, conf_b_ln,
                 conf_w_pw2, conf_b_pw2)
    hp, conf_p = conformer(hp, jnp.zeros((b, CONF_K - 1, d), F32), *conf_args)
    hs, conf_s = conformer(hs, state_conf_conv, *conf_args)
    hp = moe_ffn(hp, p_prompt, 1, False)
    hs = moe_ffn(hs, p_sample, 1, False)

    ssd_args = (row(g_mix[2]), ssd_w_in, ssd_w_conv, ssd_b_conv, ssd_dt_bias, ssd_a_log, ssd_d, ssd_g_norm,
                ssd_w_out)
    hp, ssdc_p, ssdh_p = ssd_mixer(hp, jnp.zeros((b, SSD_CONV_K - 1, SSD_CONV_DIM), F32),
                                   jnp.zeros((b, SSD_HEADS, SSD_HD, SSD_N), F32), *ssd_args)
    hs, ssdc_s, ssdh_s = ssd_mixer(hs, state_ssd_conv, state_ssd, *ssd_args)
    hp = dense_ffn(hp, p_prompt, 2)
    hs = dense_ffn(hs, p_sample, 2)

    pool_args = (row(g_mix[3]), pool_w, pool_b, pool_scale)
    hp, pool_p = pool_mixer(hp, jnp.zeros((b, POOL_HIST, d), F32), 0, *pool_args)
    hs, pool_s = pool_mixer(hs, state_pool, past, *pool_args)
    y_prompt = moe_ffn(hp, p_prompt, 3, True)
    y_sample = moe_ffn(hs, p_sample, 3, True)

    heads = lambda a: a.reshape(a.shape[0], a.shape[1], FOX_HEADS, FOX_HD)
    return (y_prompt, y_sample, heads(fox_k_p), heads(fox_v_p), fox_lf_p, heads(fox_k_s), heads(fox_v_s), fox_lf_s,
            conf_p, conf_s, ssdc_p, ssdc_s, ssdh_p, ssdh_s, pool_p, pool_s)
```

```python
import functools

import jax
import jax.numpy as jnp
import numpy as np
from jax import lax
from jax.experimental import pallas as pl
from jax.experimental.pallas import tpu as pltpu

F32 = jnp.float32
BF16 = jnp.bfloat16

D_MODEL = 1024
EPS = 1e-6
PLE_DIM = 256
FOX_HEADS = 16
FOX_HD = 64
FOX_PAIRS = FOX_HEADS // 2
FOX_VROWS = FOX_HD + 16
CONF_K = 31
CONF_HALO = 32
SSD_INNER = 2048
SSD_HD = 64
SSD_HEADS = 32
SSD_GROUPS = 4
SSD_HPG = SSD_HEADS // SSD_GROUPS
SSD_N = 128
SSD_CONV_K = 4
SSD_CONV_DIM = SSD_INNER + 2 * SSD_GROUPS * SSD_N
SSD_HALO = 8
POOL_WINDOWS = (2, 4, 8, 16)
POOL_GROUP = D_MODEL // len(POOL_WINDOWS)
POOL_HIST = 15
POOL_HALO = 16
N_EXPERTS = 8
D_FF = 2816
D_FF_EXPERT = 1408
LANES = 128

VMEM_LIMIT_BYTES = 56 * 1024 * 1024
TOKEN_TILE = 512
ATTN_BLOCK = 512
ATTN_STRIP = 256
ATTN_KEYS = 512
NORM_SLACK = 1.01
DEAD_LOGIT_GAP = 106.0
SSD_CHUNK = 256
FFN_CHUNK = 256
MOE_CHUNK = 160
MOE_SCATTER = 256


def _cparams(*sem):
    return pltpu.CompilerParams(dimension_semantics=sem, vmem_limit_bytes=VMEM_LIMIT_BYTES)


def _resident(shape):
    n = len(shape)
    return pl.BlockSpec(shape, lambda *_: (0,) * n, pipeline_mode=pl.Buffered(1))


def _dot(a, b):
    return jnp.dot(a, b, preferred_element_type=F32)


def _dot_nt(a, b):
    return lax.dot_general(a, b, (((1,), (1,)), ((), ())), preferred_element_type=F32)


def _rms(x, g):
    return x * lax.rsqrt(jnp.mean(x * x, axis=-1, keepdims=True) + EPS) * g


def _sigmoid(x):
    return 1.0 / (1.0 + jnp.exp(-x))


def _silu(x):
    return x * _sigmoid(x)


def _softplus(x):
    return jnp.maximum(x, 0.0) + jnp.log1p(jnp.exp(-jnp.abs(x)))


def _split3(x):
    p1 = x.astype(BF16)
    r1 = x - p1.astype(F32)
    p2 = r1.astype(BF16)
    r2 = r1 - p2.astype(F32)
    return p1, p2, r2.astype(BF16)


def _dot3(a_f32, b_bf16):
    p1, p2, p3 = _split3(a_f32)
    return (_dot(p1, b_bf16) + _dot(p2, b_bf16)) + _dot(p3, b_bf16)


def _dot3_left(a_bf16, b_f32):
    p1, p2, p3 = _split3(b_f32)
    return (_dot(a_bf16, p1) + _dot(a_bf16, p2)) + _dot(a_bf16, p3)


def _rep_rows(a, n):
    s, w = a.shape
    return jnp.broadcast_to(a[:, None, :], (s, n, w)).reshape(s * n, w)


def _tile_rows(a, n):
    h, w = a.shape
    return jnp.broadcast_to(a[None], (n, h, w)).reshape(n * h, w)


def _tril(n):
    r = lax.broadcasted_iota(jnp.int32, (n, n), 0)
    c = lax.broadcasted_iota(jnp.int32, (n, n), 1)
    return jnp.where(c <= r, 1.0, 0.0).astype(BF16)


def _ple_apply(h1, p, g_ple, w_gate, w_up):
    hn = _rms(h1, g_ple).astype(BF16)
    gate = _sigmoid(_dot(hn, w_gate))
    return h1 + gate * _dot(p.astype(BF16), w_up)


def _ffn_ple_kernel(h_ref, p_ref, gf_ref, wg_ref, wu_ref, wd_ref, gp_ref, wpg_ref, wpu_ref, o_ref):
    x = h_ref[...]
    xn = _rms(x, gf_ref[...]).astype(BF16)
    acc = x
    for c in range(D_FF // FFN_CHUNK):
        sl = slice(c * FFN_CHUNK, (c + 1) * FFN_CHUNK)
        hh = (_silu(_dot(xn, wg_ref[:, sl])) * _dot(xn, wu_ref[:, sl])).astype(BF16)
        acc = acc + _dot(hh, wd_ref[sl, :])
    o_ref[...] = _ple_apply(acc, p_ref[...], gp_ref[...], wpg_ref[...], wpu_ref[...])


def ffn_ple(h, p, g_ffn, wg, wu, wd, g_ple, wpg, wpu):
    m = h.shape[0]
    tm = min(TOKEN_TILE, m)
    row = lambda w: pl.BlockSpec((tm, w), lambda i: (i, 0))
    return pl.pallas_call(
        _ffn_ple_kernel,
        grid=(m // tm,),
        in_specs=[row(D_MODEL), row(PLE_DIM), _resident((1, D_MODEL)),
                  _resident((D_MODEL, D_FF)), _resident((D_MODEL, D_FF)), _resident((D_FF, D_MODEL)),
                  _resident((1, D_MODEL)), _resident((D_MODEL, D_MODEL)), _resident((PLE_DIM, D_MODEL))],
        out_specs=row(D_MODEL),
        out_shape=jax.ShapeDtypeStruct((m, D_MODEL), F32),
        compiler_params=_cparams("parallel"),
        name="ffn_ple",
    )(h, p, g_ffn, wg, wu, wd, g_ple, wpg, wpu)


def _moe_ple_kernel(h_ref, p_ref, gf_ref, wr_ref, wg_ref, wu_ref, wd_ref, gp_ref, wpg_ref, wpu_ref, gfin_ref,
                    upper_ref, lower_ref, o_ref, xn_s, comb_s, combt_s, acc_s, *, final_norm):
    e = pl.program_id(1)

    @pl.when(e == 0)
    def _():
        x = h_ref[...]
        xn = _rms(x, gf_ref[...]).astype(BF16)
        xn_s[...] = xn
        acc_s[...] = x
        logits = _dot(xn, wr_ref[...])
        lane = lax.broadcasted_iota(jnp.int32, logits.shape, 1).astype(F32)
        neg = jnp.float32(-jnp.inf)
        lg = jnp.where(lane < N_EXPERTS, logits, neg)
        m1 = jnp.max(lg, axis=-1, keepdims=True)
        i1 = jnp.min(jnp.where(lg == m1, lane, float(LANES)), axis=-1, keepdims=True)
        lg2 = jnp.where(lane == i1, neg, lg)
        m2 = jnp.max(lg2, axis=-1, keepdims=True)
        i2 = jnp.min(jnp.where(lg2 == m2, lane, float(LANES)), axis=-1, keepdims=True)
        e2 = jnp.exp(m2 - m1)
        den = 1.0 + e2
        comb = jnp.where(lane == i1, 1.0 / den, jnp.where(lane == i2, e2 / den, 0.0))
        comb_s[...] = comb
        combt_s[...] = comb.T[:N_EXPERTS, :]

    tm = xn_s.shape[0]
    lane = lax.broadcasted_iota(jnp.int32, comb_s.shape, 1)
    gate_col = jnp.sum(jnp.where(lane == e, comb_s[...], 0.0), axis=-1, keepdims=True)
    gate_row = combt_s[pl.ds(e, 1), :]
    sel_row = jnp.where(gate_row > 0.0, 1.0, 0.0)
    sel_col = jnp.where(gate_col > 0.0, 1.0, 0.0)
    rank_row = _dot(jnp.broadcast_to(sel_row, (8, tm)).astype(BF16), upper_ref[...])[0:1, :]
    rank_col = _dot(lower_ref[...], jnp.broadcast_to(sel_col, (tm, MOE_SCATTER)).astype(BF16))
    n_chunks = (jnp.max(rank_row).astype(jnp.int32) + (MOE_CHUNK - 1)) // MOE_CHUNK
    slot_row = lax.broadcasted_iota(jnp.int32, (MOE_CHUNK, tm), 0).astype(F32) + 1.0
    slot_lane = lax.broadcasted_iota(jnp.int32, (tm, MOE_SCATTER), 1)
    slot_col = slot_lane.astype(F32) + 1.0
    routed_col = (sel_col > 0.0) & (slot_lane < MOE_CHUNK)

    def chunk(ci, carry):
        base = (ci * MOE_CHUNK).astype(F32)
        gather = jnp.where((sel_row > 0.0) & (rank_row == slot_row + base), 1.0, 0.0).astype(BF16)
        xg = _dot(gather, xn_s[...]).astype(BF16)
        hh = (_silu(_dot(xg, wg_ref[0])) * _dot(xg, wu_ref[0])).astype(BF16)
        y = _dot(hh, wd_ref[0]).astype(BF16)
        y = jnp.concatenate([y, jnp.zeros((MOE_SCATTER - MOE_CHUNK, D_MODEL), BF16)], axis=0)
        scatter = jnp.where(routed_col & (rank_col == slot_col + base), 1.0, 0.0).astype(BF16)
        acc_s[...] += gate_col * _dot(scatter, y)
        return carry

    lax.fori_loop(0, n_chunks, chunk, 0)

    @pl.when(e == N_EXPERTS - 1)
    def _():
        h2 = _ple_apply(acc_s[...], p_ref[...], gp_ref[...], wpg_ref[...], wpu_ref[...])
        if final_norm:
            h2 = _rms(h2, gfin_ref[...])
        o_ref[...] = h2


def moe_ple(h, p, g_ffn, wr, wg, wu, wd, g_ple, wpg, wpu, g_final, final_norm):
    m = h.shape[0]
    tm = min(TOKEN_TILE, m)
    row = lambda w: pl.BlockSpec((tm, w), lambda i, e: (i, 0))
    expert = lambda a, b: pl.BlockSpec((1, a, b), lambda i, e: (e, 0, 0))
    upper = jnp.asarray(np.triu(np.ones((tm, tm), np.float32)), BF16)
    lower = jnp.asarray(np.tril(np.ones((tm, tm), np.float32)), BF16)
    return pl.pallas_call(
        functools.partial(_moe_ple_kernel, final_norm=final_norm),
        grid=(m // tm, N_EXPERTS),
        in_specs=[row(D_MODEL), row(PLE_DIM), _resident((1, D_MODEL)), _resident((D_MODEL, LANES)),
                  expert(D_MODEL, D_FF_EXPERT), expert(D_MODEL, D_FF_EXPERT), expert(D_FF_EXPERT, D_MODEL),
                  _resident((1, D_MODEL)), _resident((D_MODEL, D_MODEL)), _resident((PLE_DIM, D_MODEL)),
                  _resident((1, D_MODEL)), _resident((tm, tm)), _resident((tm, tm))],
        out_specs=row(D_MODEL),
        out_shape=jax.ShapeDtypeStruct((m, D_MODEL), F32),
        scratch_shapes=[pltpu.VMEM((tm, D_MODEL), BF16), pltpu.VMEM((tm, LANES), F32),
                        pltpu.VMEM((N_EXPERTS, tm), F32), pltpu.VMEM((tm, D_MODEL), F32)],
        compiler_params=_cparams("parallel", "arbitrary"),
        name="moe_ple",
    )(h, p, g_ffn, wr, wg, wu, wd, g_ple, wpg, wpu, g_final, upper, lower)


def _log_sigmoid(z):
    return jnp.minimum(z, 0.0) - jnp.log1p(jnp.exp(-jnp.abs(z)))


def _fox_proj_kernel(x_ref, g_ref, wqt_ref, wk_ref, wv_ref, wf_ref, bf_ref, sel_ref, ones_ref, hsel_ref,
                     k_out, v_out, lf_out, qt_out, vt_out, kaug_out, ft_out, qn_out, kn_out, carry):
    tm = x_ref.shape[1]

    @pl.when(pl.program_id(1) == 0)
    def _():
        carry[...] = jnp.zeros_like(carry)

    xn = _rms(x_ref[0], g_ref[...])
    xb = xn.astype(BF16)
    xnt = xn.T.astype(BF16)
    k = _dot(xb, wk_ref[...])
    v = _dot(xb, wv_ref[...])
    k_out[0] = k
    v_out[0] = v
    qtb = (_dot(wqt_ref[...], xnt) * (FOX_HD ** -0.5)).astype(BF16)
    qt_out[0] = qtb
    kb = k.astype(BF16)
    qsq = qtb.astype(F32)
    ksq = kb.astype(F32)
    qn2 = _dot(hsel_ref[...], (qsq * qsq).astype(BF16))
    kn2 = _dot_nt((ksq * ksq).astype(BF16), hsel_ref[...])
    qn_out[0, 0] = jnp.broadcast_to(jnp.max(qn2, axis=1, keepdims=True), (LANES, LANES))[:FOX_HEADS]
    kn_out[0, 0] = jnp.broadcast_to(jnp.max(kn2, axis=0, keepdims=True), (8, LANES))
    vt = v.T.astype(BF16)
    ones_rows = jnp.ones((FOX_VROWS - FOX_HD, tm), BF16)
    for h in range(FOX_HEADS):
        vt_out[0, 0, FOX_VROWS * h:FOX_VROWS * h + FOX_HD, :] = vt[FOX_HD * h:FOX_HD * (h + 1), :]
        vt_out[0, 0, FOX_VROWS * h + FOX_HD:FOX_VROWS * (h + 1), :] = ones_rows
    lf = _log_sigmoid(_dot(xb, wf_ref[...]) + bf_ref[...])
    lf_out[0] = lf[:, :FOX_HEADS]
    f = _dot3_left(_tril(tm), lf) + carry[...]
    carry[...] = f[tm - 1:tm, :]
    ft_out[0] = f.T[:FOX_HEADS, :]
    p1, p2, p3 = _split3(f)
    aug = ((_dot(p1, sel_ref[0]) + _dot(p2, sel_ref[1])) + _dot(p3, sel_ref[2]) + ones_ref[...]).astype(BF16)
    for p in range(FOX_PAIRS):
        kaug_out[0, :, 2 * LANES * p:2 * LANES * p + LANES] = kb[:, LANES * p:LANES * (p + 1)]
        kaug_out[0, :, 2 * LANES * p + LANES:2 * LANES * (p + 1)] = aug[:, LANES * p:LANES * (p + 1)]


def _fox_aug_constants():
    h = np.arange(FOX_HEADS)
    sel = np.zeros((3, LANES, FOX_PAIRS * LANES), np.float32)
    for r in range(3):
        sel[r, h, LANES * (h // 2) + 16 * (h % 2) + r] = 1.0
    p = np.arange(FOX_PAIRS)
    ones = np.zeros((1, FOX_PAIRS * LANES), np.float32)
    for r in range(3):
        ones[0, LANES * p + 32 + r] = 1.0
    ch = np.arange(D_MODEL)
    hsel = np.zeros((LANES, D_MODEL), np.float32)
    hsel[ch // FOX_HD, ch] = 1.0
    return jnp.asarray(sel, BF16), jnp.asarray(ones), jnp.asarray(hsel, BF16)


def fox_proj(x, g, wq, wk, wv, wf, bf):
    b, t, d = x.shape
    tm = min(ATTN_BLOCK, t)
    sel, ones, hsel = _fox_aug_constants()
    wf_pad = jnp.zeros((d, LANES), F32).at[:, :FOX_HEADS].set(wf).astype(BF16)
    bf_pad = jnp.zeros((1, LANES), F32).at[0, :FOX_HEADS].set(bf)
    tile = lambda w: pl.BlockSpec((1, tm, w), lambda bi, ti: (bi, ti, 0))
    return pl.pallas_call(
        _fox_proj_kernel,
        grid=(b, t // tm),
        in_specs=[tile(d), _resident((1, d)), _resident((d, d)), _resident((d, d)), _resident((d, d)),
                  _resident((d, LANES)), _resident((1, LANES)), _resident((3, LANES, FOX_PAIRS * LANES)),
                  _resident((1, FOX_PAIRS * LANES)), _resident((LANES, d))],
        out_specs=[tile(d), tile(d), tile(FOX_HEADS),
                   pl.BlockSpec((1, d, tm), lambda bi, ti: (bi, 0, ti)),
                   pl.BlockSpec((1, 1, FOX_HEADS * FOX_VROWS, tm), lambda bi, ti: (bi, ti, 0, 0)),
                   tile(2 * d),
                   pl.BlockSpec((1, FOX_HEADS, tm), lambda bi, ti: (bi, 0, ti)),
                   pl.BlockSpec((1, 1, FOX_HEADS, LANES), lambda bi, ti: (bi, ti, 0, 0)),
                   pl.BlockSpec((1, 1, 8, LANES), lambda bi, ti: (bi, ti, 0, 0))],
        out_shape=[jax.ShapeDtypeStruct((b, t, d), F32), jax.ShapeDtypeStruct((b, t, d), F32),
                   jax.ShapeDtypeStruct((b, t, FOX_HEADS), F32),
                   jax.ShapeDtypeStruct((b, d, t), BF16),
                   jax.ShapeDtypeStruct((b, t // tm, FOX_HEADS * FOX_VROWS, tm), BF16),
                   jax.ShapeDtypeStruct((b, t, 2 * d), BF16),
                   jax.ShapeDtypeStruct((b, FOX_HEADS, t), F32),
                   jax.ShapeDtypeStruct((b, t // tm, FOX_HEADS, LANES), F32),
                   jax.ShapeDtypeStruct((b, t // tm, 8, LANES), F32)],
        scratch_shapes=[pltpu.VMEM((1, LANES), F32)],
        compiler_params=_cparams("arbitrary", "arbitrary"),
        name="fox_proj",
    )(x, g, wq.T.astype(BF16), wk.astype(BF16), wv.astype(BF16), wf_pad, bf_pad, sel, ones, hsel)


def fox_dead_block_bounds(qn2, kn2, ft, blk):
    b, nb = qn2.shape[:2]
    qn = jnp.sqrt(qn2[..., 0] * NORM_SLACK).transpose(0, 2, 1)
    kn = jnp.sqrt(kn2[:, :, 0, :FOX_HEADS] * NORM_SLACK).transpose(0, 2, 1)
    kn_upto = lax.cummax(kn, axis=2)
    fq_first = ft[:, :, ::blk]
    fk_last = ft[:, :, blk - 1::blk]
    ub = qn[:, :, :, None] * kn_upto[:, :, None, :] + fq_first[:, :, :, None] - fk_last[:, :, None, :]
    ub = jnp.pad(ub, ((0, 0), (0, 0), (0, 0), (0, LANES - nb)), constant_values=-jnp.inf)
    return ub.reshape(b, FOX_PAIRS, 2, nb, LANES).transpose(0, 1, 3, 2, 4)


def _fox_attn_kernel(qt_ref, kaug_ref, vt_ref, ft_ref, ub_ref, ot_ref, qaug, m_s, acc, s_buf):
    tq = qt_ref.shape[2]
    tk = vt_ref.shape[3]
    i = pl.program_id(2)
    zeros = lambda r: jnp.zeros((r, tq), BF16)
    minus = jnp.full((16, tq), -1.0, BF16)
    row = lax.broadcasted_iota(jnp.int32, (16, tq), 0)
    for h in range(2):
        qaug[h, 64 * h:64 * (h + 1)] = qt_ref[0, 64 * h:64 * (h + 1), :]
        qaug[h, 64 * (1 - h):64 * (2 - h)] = zeros(64)
        qaug[h, LANES + 16 * h:LANES + 16 * (h + 1)] = minus
        qaug[h, LANES + 16 * (1 - h):LANES + 16 * (2 - h)] = zeros(16)
        p1, p2, p3 = _split3(ft_ref[0, 0, h:h + 1, :])
        blk = jnp.where(row == 0, p1.astype(F32),
                        jnp.where(row == 1, p2.astype(F32), jnp.where(row == 2, p3.astype(F32), 0.0)))
        qaug[h, LANES + 32:LANES + 48] = blk.astype(BF16)
        qaug[h, LANES + 48:2 * LANES] = zeros(LANES - 48)
    m_s[...] = jnp.full(m_s.shape, -jnp.inf, F32)
    acc[...] = jnp.zeros(acc.shape, F32)

    chains = [(h, c) for h in range(2) for c in range(tq // ATTN_STRIP)]

    def qk(n, j, slot):
        h, c = chains[n]
        qa = qaug[h, :, c * ATTN_STRIP:(c + 1) * ATTN_STRIP]
        for r in range(tk // ATTN_KEYS):
            kj = kaug_ref[0, pl.ds(pl.multiple_of(j * tk + r * ATTN_KEYS, ATTN_KEYS), ATTN_KEYS), :]
            s_buf[slot, n, r * ATTN_KEYS:(r + 1) * ATTN_KEYS, :] = _dot(kj, qa)

    def softmax(n, slot, masked):
        h, c = chains[n]
        qs = slice(c * ATTN_STRIP, (c + 1) * ATTN_STRIP)
        st = s_buf[slot, n]
        if masked:
            kpos = lax.broadcasted_iota(jnp.int32, st.shape, 0)
            qpos = lax.broadcasted_iota(jnp.int32, st.shape, 1) + c * ATTN_STRIP
            st = jnp.where(kpos <= qpos, st, -jnp.inf)
        m_prev = m_s[h, :, qs]
        m_new = jnp.maximum(m_prev, jnp.max(st, axis=0, keepdims=True))
        alpha = jnp.exp(m_prev - m_new)
        p = jnp.exp(st - m_new)
        m_s[h, :, qs] = m_new
        return p.astype(BF16), alpha

    def pv(n, j, p, alpha):
        h, c = chains[n]
        qs = slice(c * ATTN_STRIP, (c + 1) * ATTN_STRIP)
        acc[h, :, qs] = alpha * acc[h, :, qs] + _dot(vt_ref[0, j, FOX_VROWS * h:FOX_VROWS * (h + 1), :], p)

    def block(j, slot, masked, prefetch=True):
        jn = jnp.maximum(j - 1, 0)
        pa = {}
        if prefetch:
            for n in range(len(chains)):
                qk(n, jn, 1 - slot)
        for n in range(len(chains)):
            pa[n] = softmax(n, slot, masked)
            if n >= 1:
                pv(n - 1, j, *pa.pop(n - 1))
        pv(len(chains) - 1, j, *pa.pop(len(chains) - 1))

    for n in range(len(chains)):
        qk(n, i, 0)
    block(i, 0, True)

    lane = lax.broadcasted_iota(jnp.int32, (1, LANES), 1)
    live = lane < 0
    for h in range(2):
        m_low = jnp.min(m_s[h], axis=1, keepdims=True)
        live = live | (ub_ref[0, 0, 0, h:h + 1, :] >= m_low - DEAD_LOGIT_GAP)
    n_live = jnp.sum(jnp.where(live & (lane < i), 1.0, 0.0)).astype(jnp.int32)

    def pair(u, carry):
        j = i - 1 - 2 * u
        block(j, 1, False)
        block(j - 1, 0, False)
        return carry

    lax.fori_loop(0, n_live // 2, pair, 0)

    @pl.when(n_live % 2 == 1)
    def _():
        block(i - n_live, 1, False, prefetch=False)

    for h in range(2):
        ot_ref[0, 64 * h:64 * (h + 1), :] = (acc[h, :FOX_HD] / acc[h, FOX_HD:FOX_HD + 1]).astype(BF16)


def fox_attention(qt, kaug, vt, ft, ub):
    b, d, t = qt.shape
    blk = vt.shape[3]
    assert t // blk <= LANES
    ft = ft.reshape(b, FOX_PAIRS, 2, t)
    return pl.pallas_call(
        _fox_attn_kernel,
        grid=(b, FOX_PAIRS, t // blk),
        in_specs=[pl.BlockSpec((1, LANES, blk), lambda bi, p, i: (bi, p, i)),
                  pl.BlockSpec((1, t, 2 * LANES), lambda bi, p, i: (bi, 0, p)),
                  pl.BlockSpec((1, t // blk, 2 * FOX_VROWS, blk), lambda bi, p, i: (bi, 0, p, 0)),
                  pl.BlockSpec((1, 1, 2, blk), lambda bi, p, i: (bi, p, 0, i)),
                  pl.BlockSpec((1, 1, 1, 2, LANES), lambda bi, p, i: (bi, p, i, 0, 0))],
        out_specs=pl.BlockSpec((1, LANES, blk), lambda bi, p, i: (bi, p, i)),
        out_shape=jax.ShapeDtypeStruct((b, d, t), BF16),
        scratch_shapes=[pltpu.VMEM((2, 2 * LANES, blk), BF16), pltpu.VMEM((2, 1, blk), F32),
                        pltpu.VMEM((2, FOX_VROWS, blk), F32),
                        pltpu.VMEM((2, 2 * blk // ATTN_STRIP, blk, ATTN_STRIP), F32)],
        compiler_params=_cparams("parallel", "parallel", "arbitrary"),
        name="fox_attention",
    )(qt, kaug, vt, ft, ub)


def _out_proj_t_kernel(h_ref, ot_ref, w_ref, o_ref):
    o_ref[0] = h_ref[0] + pl.dot(ot_ref[0], w_ref[...], trans_a=True)


def out_proj_t(h, ot, w):
    b, t, d = h.shape
    tm = min(TOKEN_TILE, t)
    tile = pl.BlockSpec((1, tm, d), lambda bi, ti: (bi, ti, 0))
    return pl.pallas_call(
        _out_proj_t_kernel,
        grid=(b, t // tm),
        in_specs=[tile, pl.BlockSpec((1, d, tm), lambda bi, ti: (bi, 0, ti)), _resident((d, d))],
        out_specs=tile,
        out_shape=jax.ShapeDtypeStruct((b, t, d), F32),
        compiler_params=_cparams("parallel", "parallel"),
        name="fox_out_proj",
    )(h, ot, w.astype(BF16))


def _fox_sample_kernel(x_ref, g_ref, wq_ref, wk_ref, wv_ref, wf_ref, bf_ref, wo_ref, ck_ref, cv_ref, clf_ref,
                       h_out, k_out, v_out, lf_out):
    s = x_ref.shape[1]
    pl_len = ck_ref.shape[1]
    x = x_ref[0]
    xb = _rms(x, g_ref[...]).astype(BF16)
    q = _dot(xb, wq_ref[...]) * (FOX_HD ** -0.5)
    k = _dot(xb, wk_ref[...])
    v = _dot(xb, wv_ref[...])
    lf = _log_sigmoid(_dot(xb, wf_ref[...]) + bf_ref[...])
    k_out[0] = k
    v_out[0] = v
    lf_out[0] = lf[:, :FOX_HEADS]

    rows = s * FOX_HEADS
    rh = lax.broadcasted_iota(jnp.int32, (rows, D_MODEL), 0) % FOX_HEADS
    lh = lax.broadcasted_iota(jnp.int32, (rows, D_MODEL), 1) // FOX_HD
    head_lanes = rh == lh
    qbd = jnp.where(head_lanes, _rep_rows(q, FOX_HEADS), 0.0).astype(BF16)

    cblk = 256
    carry = jnp.zeros((1, LANES), F32)
    tri = _tril(cblk)
    f_parts = []
    for c in range(pl_len // cblk):
        fc = _dot3_left(tri, clf_ref[0, c * cblk:(c + 1) * cblk, :]) + carry
        carry = fc[cblk - 1:cblk, :]
        f_parts.append(fc)
    f_cache = jnp.concatenate(f_parts, axis=0)
    pad_rows = lambda a: jnp.concatenate([a, jnp.zeros((LANES - s, a.shape[1]), a.dtype)], axis=0)
    f_new = _dot3_left(_tril(LANES), pad_rows(lf)) + carry

    lane = lax.broadcasted_iota(jnp.int32, (rows, LANES), 1)
    row_head = lax.broadcasted_iota(jnp.int32, (rows, LANES), 0) % FOX_HEADS
    fq = jnp.sum(jnp.where(lane == row_head, _rep_rows(f_new[:s], FOX_HEADS), 0.0), axis=-1, keepdims=True)
    fk_cache = _tile_rows(f_cache.T[:FOX_HEADS, :], s)
    fk_new = _tile_rows(f_new.T[:FOX_HEADS, :], s)

    kc = 512
    qk = [_dot_nt(qbd, ck_ref[0, c * kc:(c + 1) * kc, :].astype(BF16)) for c in range(pl_len // kc)]
    s_cache = jnp.concatenate(qk, axis=1) + fq - fk_cache
    s_new = _dot_nt(qbd, pad_rows(k).astype(BF16)) + fq - fk_new
    qi = lax.broadcasted_iota(jnp.int32, (rows, LANES), 0) // FOX_HEADS
    kj = lax.broadcasted_iota(jnp.int32, (rows, LANES), 1)
    s_new = jnp.where(kj <= qi, s_new, -jnp.inf)
    m = jnp.maximum(jnp.max(s_cache, axis=-1, keepdims=True), jnp.max(s_new, axis=-1, keepdims=True))
    p_cache = jnp.exp(s_cache - m)
    p_new = jnp.exp(s_new - m)
    den = jnp.sum(p_cache, axis=-1, keepdims=True) + jnp.sum(p_new, axis=-1, keepdims=True)
    pv = _dot(p_new.astype(BF16), pad_rows(v).astype(BF16))
    pb = p_cache.astype(BF16)
    for c in range(pl_len // kc):
        pv = pv + _dot(pb[:, c * kc:(c + 1) * kc], cv_ref[0, c * kc:(c + 1) * kc, :].astype(BF16))
    o_heads = jnp.where(head_lanes, pv / den, 0.0).astype(BF16)
    pick = (lax.broadcasted_iota(jnp.int32, (s, rows), 1) // FOX_HEADS
            == lax.broadcasted_iota(jnp.int32, (s, rows), 0))
    o = _dot(jnp.where(pick, 1.0, 0.0).astype(BF16), o_heads)
    h_out[0] = x + _dot(o.astype(BF16), wo_ref[...])


def fox_sample(x, g, wq, wk, wv, wf, bf, wo, ck, cv, clf):
    b, s, d = x.shape
    p = ck.shape[1]
    wf_pad = jnp.zeros((d, LANES), F32).at[:, :FOX_HEADS].set(wf).astype(BF16)
    bf_pad = jnp.zeros((1, LANES), F32).at[0, :FOX_HEADS].set(bf)
    clf_pad = jnp.zeros((b, p, LANES), F32).at[:, :, :FOX_HEADS].set(clf)
    seq = lambda n, w: pl.BlockSpec((1, n, w), lambda bi: (bi, 0, 0))
    return pl.pallas_call(
        _fox_sample_kernel,
        grid=(b,),
        in_specs=[seq(s, d), _resident((1, d)), _resident((d, d)), _resident((d, d)), _resident((d, d)),
                  _resident((d, LANES)), _resident((1, LANES)), _resident((d, d)),
                  seq(p, d), seq(p, d), seq(p, LANES)],
        out_specs=[seq(s, d), seq(s, d), seq(s, d), seq(s, FOX_HEADS)],
        out_shape=[jax.ShapeDtypeStruct((b, s, d), F32)] * 3 + [jax.ShapeDtypeStruct((b, s, FOX_HEADS), F32)],
        compiler_params=_cparams("parallel"),
        name="fox_sample",
    )(x, g, wq.astype(BF16), wk.astype(BF16), wv.astype(BF16), wf_pad, bf_pad, wo.astype(BF16),
      ck.reshape(b, p, d), cv.reshape(b, p, d), clf_pad)


def _halo_specs(b, t, tm, d, halo):
    main = pl.BlockSpec((1, tm, d), lambda bi, ti: (bi, ti, 0))
    if t == tm:
        return [main]
    per = tm // halo
    return [main, pl.BlockSpec((1, halo, d), lambda bi, ti: (bi, jnp.maximum(ti * per - 1, 0), 0))]


def _conf_kernel(*refs, has_halo):
    if has_halo:
        x_ref, xh_ref, *refs = refs
    else:
        x_ref, *refs = refs
    (hist_ref, g_ref, w1_ref, b1_ref, wdw_ref, bdw_ref, gln_ref, bln_ref, w2_ref, b2_ref,
     o_ref, hs_ref, uext, ushift) = refs
    tm = x_ref.shape[1]

    def glu(rows):
        ag = _dot(_rms(rows, g_ref[...]).astype(BF16), w1_ref[...]) + b1_ref[...]
        return ag[:, :D_MODEL] * _sigmoid(ag[:, D_MODEL:])

    x = x_ref[0]
    if has_halo:
        prev = jnp.where(pl.program_id(1) == 0, hist_ref[0], glu(xh_ref[0]))
    else:
        prev = hist_ref[0]
    uext[0:CONF_HALO] = prev
    uext[CONF_HALO:CONF_HALO + tm] = glu(x)
    uext[CONF_HALO + tm:] = jnp.zeros((8, D_MODEL), F32)
    off = CONF_HALO - (CONF_K - 1)
    y = jnp.broadcast_to(bdw_ref[...], (tm, D_MODEL))
    for s in range(8):
        ushift[...] = uext[s:s + tm + CONF_HALO, :]
        for k in range(CONF_K):
            if (off + k) % 8 == s:
                a = off + k - s
                y = y + wdw_ref[k:k + 1, :] * ushift[a:a + tm, :]
    hs_ref[0] = uext[tm + off:tm + CONF_HALO, :]
    mu = jnp.mean(y, axis=-1, keepdims=True)
    yc = y - mu
    var = jnp.mean(yc * yc, axis=-1, keepdims=True)
    v = _silu(yc * lax.rsqrt(var + EPS) * gln_ref[...] + bln_ref[...])
    o_ref[0] = x + _dot(v.astype(BF16), w2_ref[...]) + b2_ref[...]


def conformer(x, hist, g, w1, b1, wdw, bdw, gln, bln, w2, b2):
    b, t, d = x.shape
    tm = min(TOKEN_TILE, t)
    xs = _halo_specs(b, t, tm, d, CONF_HALO)
    hist_pad = jnp.pad(hist, ((0, 0), (CONF_HALO - (CONF_K - 1), 0), (0, 0)))
    per_b = lambda n: pl.BlockSpec((1, n, d), lambda bi, ti: (bi, 0, 0))
    return pl.pallas_call(
        functools.partial(_conf_kernel, has_halo=len(xs) == 2),
        grid=(b, t // tm),
        in_specs=xs + [per_b(CONF_HALO), _resident((1, d)), _resident((d, 2 * d)), _resident((1, 2 * d)),
                       _resident((CONF_K, d)), _resident((1, d)), _resident((1, d)), _resident((1, d)),
                       _resident((d, d)), _resident((1, d))],
        out_specs=[pl.BlockSpec((1, tm, d), lambda bi, ti: (bi, ti, 0)), per_b(CONF_K - 1)],
        out_shape=[jax.ShapeDtypeStruct((b, t, d), F32), jax.ShapeDtypeStruct((b, CONF_K - 1, d), F32)],
        scratch_shapes=[pltpu.VMEM((CONF_HALO + tm + 8, d), F32),
                        pltpu.VMEM((CONF_HALO + tm, d), F32)],
        compiler_params=_cparams("parallel", "arbitrary"),
        name="conformer",
    )(*([x] * len(xs)), hist_pad, g, w1.astype(BF16), b1.reshape(1, -1), wdw, bdw.reshape(1, -1),
      gln.reshape(1, -1), bln.reshape(1, -1), w2.astype(BF16), b2.reshape(1, -1))


def _pool_kernel(*refs, has_halo, pos0):
    if has_halo:
        x_ref, xh_ref, *refs = refs
    else:
        x_ref, *refs = refs
    hist_ref, g_ref, w_ref, b_ref, scale_ref, o_ref, hs_ref, ext = refs
    tm = x_ref.shape[1]
    i = pl.program_id(1)
    x = x_ref[0]
    xn = _rms(x, g_ref[...])
    if has_halo:
        prev = jnp.where(i == 0, hist_ref[0], _rms(xh_ref[0], g_ref[...]))
    else:
        prev = hist_ref[0]
    ext[0:POOL_HALO] = prev
    ext[POOL_HALO:] = xn
    hs_ref[0] = ext[tm + POOL_HALO - POOL_HIST:tm + POOL_HALO, :]
    pos = pos0 + i * tm + lax.broadcasted_iota(jnp.int32, (tm, 1), 0)
    ys = []
    for gi, wlen in enumerate(POOL_WINDOWS):
        lanes = slice(gi * POOL_GROUP, (gi + 1) * POOL_GROUP)
        s = ext[POOL_HALO:POOL_HALO + tm, lanes]
        for k in range(1, wlen):
            s = s + ext[POOL_HALO - k:POOL_HALO - k + tm, lanes]
        cnt = jnp.minimum(pos + 1, wlen).astype(F32)
        dg = s / cnt - xn[:, lanes]
        ys.append(_dot(dg.astype(BF16), w_ref[gi]))
    y = (jnp.concatenate(ys, axis=1) + b_ref[...]) * scale_ref[...]
    o_ref[0] = x + y


def pool_mixer(x, hist, pos0, g, w, bias, scale):
    b, t, d = x.shape
    tm = min(TOKEN_TILE, t)
    xs = _halo_specs(b, t, tm, d, POOL_HALO)
    hist_pad = jnp.pad(hist, ((0, 0), (POOL_HALO - POOL_HIST, 0), (0, 0)))
    per_b = lambda n: pl.BlockSpec((1, n, d), lambda bi, ti: (bi, 0, 0))
    ng = len(POOL_WINDOWS)
    return pl.pallas_call(
        functools.partial(_pool_kernel, has_halo=len(xs) == 2, pos0=pos0),
        grid=(b, t // tm),
        in_specs=xs + [per_b(POOL_HALO), _resident((1, d)), _resident((ng, POOL_GROUP, POOL_GROUP)),
                       _resident((1, d)), _resident((1, d))],
        out_specs=[pl.BlockSpec((1, tm, d), lambda bi, ti: (bi, ti, 0)), per_b(POOL_HIST)],
        out_shape=[jax.ShapeDtypeStruct((b, t, d), F32), jax.ShapeDtypeStruct((b, POOL_HIST, d), F32)],
        scratch_shapes=[pltpu.VMEM((POOL_HALO + tm, d), F32)],
        compiler_params=_cparams("parallel", "arbitrary"),
        name="pool_mixer",
    )(*([x] * len(xs)), hist_pad, g, w.astype(BF16), bias.reshape(1, -1), scale.reshape(1, -1))


def _ssd_kernel(*refs, has_halo, valid):
    if has_halo:
        x_ref, xh_ref, *refs = refs
    else:
        x_ref, *refs = refs
    (chist_ref, h0_ref, g_ref, wz_ref, wxbc_ref, wdt_ref, wconv_ref, bconv_ref, dtb_ref, a_ref, dskip_ref,
     gn_ref, wout_ref, exp_ref, expt_ref, o_ref, cs_out_ref, h_out_ref, ext, hstate) = refs
    L = x_ref.shape[1]
    i = pl.program_id(1)
    gw = SSD_HPG * SSD_HD
    gn = SSD_GROUPS * SSD_N

    @pl.when(i == 0)
    def _():
        hstate[...] = h0_ref[0]

    x = x_ref[0]
    xb = _rms(x, g_ref[...]).astype(BF16)
    z = _dot(xb, wz_ref[...])
    dtr = _dot(xb, wdt_ref[...])
    if has_halo:
        prev = jnp.where(i == 0, chist_ref[0], _dot(_rms(xh_ref[0], g_ref[...]).astype(BF16), wxbc_ref[...]))
    else:
        prev = chist_ref[0]
    ext[0:SSD_HALO] = prev
    ext[SSD_HALO:] = _dot(xb, wxbc_ref[...])
    off = SSD_HALO - (SSD_CONV_K - 1)
    y = jnp.broadcast_to(bconv_ref[...], (L, SSD_CONV_DIM))
    for k in range(SSD_CONV_K):
        y = y + wconv_ref[k:k + 1, :] * ext[off + k:off + k + L, :]
    cs_out_ref[0] = ext[valid + off:valid + SSD_HALO, :]
    xbc = _silu(y)
    xs = xbc[:, :SSD_INNER]
    bm = xbc[:, SSD_INNER:SSD_INNER + gn].astype(BF16)
    cm = xbc[:, SSD_INNER + gn:].astype(BF16)

    head_lane = lax.broadcasted_iota(jnp.int32, (L, LANES), 1) < SSD_HEADS
    live = head_lane
    if valid < L:
        live = live & (lax.broadcasted_iota(jnp.int32, (L, LANES), 0) < valid)
    dt = jnp.where(live, _softplus(dtr + dtb_ref[...]), 0.0)
    cs = _dot3_left(_tril(L), dt * a_ref[...])
    cst = cs.T
    dtt = dt.T
    to_end = jnp.exp(cs[L - 1:L, :] - cs) * dt
    td_x = _dot(to_end.astype(BF16), exp_ref[...])
    ecs_x = _dot3(jnp.exp(cs), exp_ref[...])
    chunk_decay = jnp.broadcast_to(jnp.exp(cst[:, L - 1:L]), (LANES, LANES))
    xs_b = xs.astype(BF16)
    xtd = xs * td_x
    r = lax.broadcasted_iota(jnp.int32, (L, L), 0)
    c = lax.broadcasted_iota(jnp.int32, (L, L), 1)
    causal = c <= r
    low_half = lax.broadcasted_iota(jnp.int32, (L, 2 * SSD_HD), 1) < SSD_HD

    y_groups = []
    for g in range(SSD_GROUPS):
        cg = cm[:, g * SSD_N:(g + 1) * SSD_N]
        bg = bm[:, g * SSD_N:(g + 1) * SSD_N]
        cb = _dot_nt(cg, bg)
        hg = hstate[g]
        y_off = _dot_nt(cg, hg.astype(BF16)) * ecs_x[:, g * gw:(g + 1) * gw]
        pairs = []
        for pr in range(SSD_HPG // 2):
            lo = g * gw + pr * 2 * SSD_HD
            xpair = xs_b[:, lo:lo + 2 * SSD_HD]
            acc = None
            for hh in range(2):
                e = g * SSD_HPG + 2 * pr + hh
                seg = cs[:, e:e + 1] - cst[e:e + 1, :]
                w = cb * jnp.exp(jnp.where(causal, seg, -jnp.inf)) * dtt[e:e + 1, :]
                xm = jnp.where(low_half == (hh == 0), xpair, jnp.zeros_like(xpair))
                term = _dot(w.astype(BF16), xm)
                acc = term if acc is None else acc + term
            pairs.append(acc)
        y_groups.append(jnp.concatenate(pairs, axis=1) + y_off)
        s_g = _dot(xtd[:, g * gw:(g + 1) * gw].T.astype(BF16), bg)
        hstate[g] = hg * _dot3_left(expt_ref[g], chunk_decay) + s_g

    ys = (jnp.concatenate(y_groups, axis=1) + dskip_ref[...] * xs) * _silu(z)
    normed = []
    for g in range(SSD_GROUPS):
        yg = ys[:, g * gw:(g + 1) * gw]
        normed.append(yg * lax.rsqrt(jnp.mean(yg * yg, axis=-1, keepdims=True) + EPS))
    yn = jnp.concatenate(normed, axis=1) * gn_ref[...]
    o_ref[0] = x + _dot(yn.astype(BF16), wout_ref[...])

    @pl.when(i == pl.num_programs(1) - 1)
    def _():
        h_out_ref[0] = hstate[...]


def _ssd_expand_constants():
    ch = np.arange(SSD_INNER)
    exp = np.zeros((LANES, SSD_INNER), np.float32)
    exp[ch // SSD_HD, ch] = 1.0
    gw = SSD_HPG * SSD_HD
    rows = np.arange(gw)
    expt = np.zeros((SSD_GROUPS, gw, LANES), np.float32)
    for g in range(SSD_GROUPS):
        expt[g, rows, g * SSD_HPG + rows // SSD_HD] = 1.0
    return jnp.asarray(exp, BF16), jnp.asarray(expt, BF16)


def ssd_mixer(x, conv_hist, h0, g, w_in, w_conv, b_conv, dt_bias, a_log, d_skip, g_norm, w_out):
    b, t, d = x.shape
    valid = t
    if t < LANES:
        x = jnp.pad(x, ((0, 0), (0, LANES - t), (0, 0)))
    tp = x.shape[1]
    L = min(SSD_CHUNK, tp)
    valid = L if tp > L else valid
    xs = _halo_specs(b, tp, L, d, SSD_HALO)
    gw = SSD_HPG * SSD_HD
    hist_pad = jnp.pad(conv_hist, ((0, 0), (SSD_HALO - (SSD_CONV_K - 1), 0), (0, 0)))
    wz = w_in[:, :SSD_INNER].astype(BF16)
    wxbc = w_in[:, SSD_INNER:SSD_INNER + SSD_CONV_DIM].astype(BF16)
    pad_heads = lambda v: jnp.zeros((v.shape[0], LANES), F32).at[:, :SSD_HEADS].set(v)
    wdt = pad_heads(w_in[:, SSD_INNER + SSD_CONV_DIM:]).astype(BF16)
    dtb = pad_heads(dt_bias.reshape(1, -1))
    a = pad_heads(-jnp.exp(a_log.astype(F32)).reshape(1, -1))
    dskip = jnp.repeat(d_skip, SSD_HD).reshape(1, -1)
    exp, expt = _ssd_expand_constants()
    per_b3 = lambda n, w: pl.BlockSpec((1, n, w), lambda bi, ti: (bi, 0, 0))
    state = pl.BlockSpec((1, SSD_GROUPS, gw, SSD_N), lambda bi, ti: (bi, 0, 0, 0))
    out, conv_new, h_new = pl.pallas_call(
        functools.partial(_ssd_kernel, has_halo=len(xs) == 2, valid=valid),
        grid=(b, tp // L),
        in_specs=xs + [per_b3(SSD_HALO, SSD_CONV_DIM), state, _resident((1, d)),
                       _resident((d, SSD_INNER)), _resident((d, SSD_CONV_DIM)), _resident((d, LANES)),
                       _resident((SSD_CONV_K, SSD_CONV_DIM)), _resident((1, SSD_CONV_DIM)),
                       _resident((1, LANES)), _resident((1, LANES)), _resident((1, SSD_INNER)),
                       _resident((1, SSD_INNER)), _resident((SSD_INNER, d)),
                       _resident((LANES, SSD_INNER)), _resident((SSD_GROUPS, gw, LANES))],
        out_specs=[pl.BlockSpec((1, L, d), lambda bi, ti: (bi, ti, 0)),
                   per_b3(SSD_CONV_K - 1, SSD_CONV_DIM), state],
        out_shape=[jax.ShapeDtypeStruct((b, tp, d), F32),
                   jax.ShapeDtypeStruct((b, SSD_CONV_K - 1, SSD_CONV_DIM), F32),
                   jax.ShapeDtypeStruct((b, SSD_GROUPS, gw, SSD_N), F32)],
        scratch_shapes=[pltpu.VMEM((SSD_HALO + L, SSD_CONV_DIM), F32), pltpu.VMEM((SSD_GROUPS, gw, SSD_N), F32)],
        compiler_params=_cparams("parallel", "arbitrary"),
        name="ssd_mixer",
    )(*([x] * len(xs)), hist_pad, h0.reshape(b, SSD_GROUPS, gw, SSD_N), g, wz, wxbc, wdt, w_conv,
      b_conv.reshape(1, -1), dtb, a, dskip, g_norm.reshape(1, -1), w_out.astype(BF16), exp, expt)
    return out[:, :t], conv_new, h_new.reshape(b, SSD_HEADS, SSD_HD, SSD_N)


def kernel(x_prompt, x_sample, cache_fox_k, cache_fox_v, cache_fox_logf, state_conf_conv, state_ssd_conv, state_ssd, state_pool, p_prompt, p_sample, g_mix, g_ffn, g_ple, g_final, fox_w_q, fox_w_k, fox_w_v, fox_w_f, fox_b_f, fox_w_o, conf_w_pw1, conf_b_pw1, conf_w_dw, conf_b_dw, conf_g_ln, conf_b_ln, conf_w_pw2, conf_b_pw2, ssd_w_in, ssd_w_conv, ssd_b_conv, ssd_dt_bias, ssd_a_log, ssd_d, ssd_g_norm, ssd_w_out, pool_w, pool_b, pool_scale, ffn_w_gate, ffn_w_up, ffn_w_down, moe_w_router, moe_w_gate, moe_w_up, moe_w_down, ple_w_up, ple_w_gate):
    b, t, d = x_prompt.shape
    bs, s, _ = x_sample.shape
    past = cache_fox_k.shape[1]
    row = lambda v: v.reshape(1, -1)
    bf = lambda w: w.astype(BF16)
    flat = lambda a: a.reshape(-1, a.shape[-1])
    hp, hs = x_prompt, x_sample

    def dense_ffn(h, p, i):
        j = i // 2
        out = ffn_ple(flat(h), flat(p[i]), row(g_ffn[i]), bf(ffn_w_gate[j]), bf(ffn_w_up[j]), bf(ffn_w_down[j]),
                      row(g_ple[i]), bf(ple_w_gate[i]), bf(ple_w_up[i]))
        return out.reshape(h.shape)

    def moe_ffn(h, p, i, final_norm):
        j = i // 2
        wr = jnp.zeros((d, LANES), F32).at[:, :N_EXPERTS].set(moe_w_router[j]).astype(BF16)
        out = moe_ple(flat(h), flat(p[i]), row(g_ffn[i]), wr, bf(moe_w_gate[j]), bf(moe_w_up[j]),
                      bf(moe_w_down[j]), row(g_ple[i]), bf(ple_w_gate[i]), bf(ple_w_up[i]), row(g_final),
                      final_norm)
        return out.reshape(h.shape)

    fox_k_p, fox_v_p, fox_lf_p, qt, vt, kaug, ft, qn2, kn2 = fox_proj(hp, row(g_mix[0]), fox_w_q, fox_w_k,
                                                                      fox_w_v, fox_w_f, fox_b_f)
    ub = fox_dead_block_bounds(qn2, kn2, ft, vt.shape[3])
    hp = out_proj_t(hp, fox_attention(qt, kaug, vt, ft, ub), fox_w_o)
    hs, fox_k_s, fox_v_s, fox_lf_s = fox_sample(hs, row(g_mix[0]), fox_w_q, fox_w_k, fox_w_v, fox_w_f, fox_b_f,
                                                fox_w_o, cache_fox_k, cache_fox_v, cache_fox_logf)
    hp = dense_ffn(hp, p_prompt, 0)
    hs = dense_ffn(hs, p_sample, 0)

    conf_args = (row(g_mix[1]), conf_w_pw1, conf_b_pw1, conf_w_dw, conf_b_dw, conf_g_ln, conf_b_ln,
                 conf_w_pw2, conf_b_pw2)
    hp, conf_p = conformer(hp, jnp.zeros((b, CONF_K - 1, d), F32), *conf_args)
    hs, conf_s = conformer(hs, state_conf_conv, *conf_args)
    hp = moe_ffn(hp, p_prompt, 1, False)
    hs = moe_ffn(hs, p_sample, 1, False)

    ssd_args = (row(g_mix[2]), ssd_w_in, ssd_w_conv, ssd_b_conv, ssd_dt_bias, ssd_a_log, ssd_d, ssd_g_norm,
                ssd_w_out)
    hp, ssdc_p, ssdh_p = ssd_mixer(hp, jnp.zeros((b, SSD_CONV_K - 1, SSD_CONV_DIM), F32),
                                   jnp.zeros((b, SSD_HEADS, SSD_HD, SSD_N), F32), *ssd_args)
    hs, ssdc_s, ssdh_s = ssd_mixer(hs, state_ssd_conv, state_ssd, *ssd_args)
    hp = dense_ffn(hp, p_prompt, 2)
    hs = dense_ffn(hs, p_sample, 2)

    pool_args = (row(g_mix[3]), pool_w, pool_b, pool_scale)
    hp, pool_p = pool_mixer(hp, jnp.zeros((b, POOL_HIST, d), F32), 0, *pool_args)
    hs, pool_s = pool_mixer(hs, state_pool, past, *pool_args)
    y_prompt = moe_ffn(hp, p_prompt, 3, True)
    y_sample = moe_ffn(hs, p_sample, 3, True)

    heads = lambda a: a.reshape(a.shape[0], a.shape[1], FOX_HEADS, FOX_HD)
    return (y_prompt, y_sample, heads(fox_k_p), heads(fox_v_p), fox_lf_p, heads(fox_k_s), heads(fox_v_s), fox_lf_s,
            conf_p, conf_s, ssdc_p, ssdc_s, ssdh_p, ssdh_s, pool_p, pool_s)
```

```python
import functools

import jax
import jax.numpy as jnp
import numpy as np
from jax import lax
from jax.experimental import pallas as pl
from jax.experimental.pallas import tpu as pltpu

F32 = jnp.float32
BF16 = jnp.bfloat16

D_MODEL = 1024
EPS = 1e-6
PLE_DIM = 256
FOX_HEADS = 16
FOX_HD = 64
FOX_PAIRS = FOX_HEADS // 2
FOX_VROWS = FOX_HD + 16
CONF_K = 31
CONF_HALO = 32
SSD_INNER = 2048
SSD_HD = 64
SSD_HEADS = 32
SSD_GROUPS = 4
SSD_HPG = SSD_HEADS // SSD_GROUPS
SSD_N = 128
SSD_CONV_K = 4
SSD_CONV_DIM = SSD_INNER + 2 * SSD_GROUPS * SSD_N
SSD_HALO = 8
POOL_WINDOWS = (2, 4, 8, 16)
POOL_GROUP = D_MODEL // len(POOL_WINDOWS)
POOL_HIST = 15
POOL_HALO = 16
N_EXPERTS = 8
D_FF = 2816
D_FF_EXPERT = 1408
LANES = 128

VMEM_LIMIT_BYTES = 56 * 1024 * 1024
TOKEN_TILE = 512
ATTN_BLOCK = 512
ATTN_STRIP = 256
ATTN_KEYS = 512
NORM_SLACK = 1.01
DEAD_LOGIT_GAP = 106.0
SSD_CHUNK = 256
FFN_CHUNK = 256
MOE_TILE = 1024
MOE_CHUNK = 320
MOE_SCATTER = 512


def _cparams(*sem):
    return pltpu.CompilerParams(dimension_semantics=sem, vmem_limit_bytes=VMEM_LIMIT_BYTES)


def _resident(shape):
    n = len(shape)
    return pl.BlockSpec(shape, lambda *_: (0,) * n, pipeline_mode=pl.Buffered(1))


def _dot(a, b):
    return jnp.dot(a, b, preferred_element_type=F32)


def _dot_nt(a, b):
    return lax.dot_general(a, b, (((1,), (1,)), ((), ())), preferred_element_type=F32)


def _rms(x, g):
    return x * lax.rsqrt(jnp.mean(x * x, axis=-1, keepdims=True) + EPS) * g


def _sigmoid(x):
    return 1.0 / (1.0 + jnp.exp(-x))


def _silu(x):
    return x * _sigmoid(x)


def _softplus(x):
    return jnp.maximum(x, 0.0) + jnp.log1p(jnp.exp(-jnp.abs(x)))


def _split3(x):
    p1 = x.astype(BF16)
    r1 = x - p1.astype(F32)
    p2 = r1.astype(BF16)
    r2 = r1 - p2.astype(F32)
    return p1, p2, r2.astype(BF16)


def _dot3(a_f32, b_bf16):
    p1, p2, p3 = _split3(a_f32)
    return (_dot(p1, b_bf16) + _dot(p2, b_bf16)) + _dot(p3, b_bf16)


def _dot3_left(a_bf16, b_f32):
    p1, p2, p3 = _split3(b_f32)
    return (_dot(a_bf16, p1) + _dot(a_bf16, p2)) + _dot(a_bf16, p3)


def _rep_rows(a, n):
    s, w = a.shape
    return jnp.broadcast_to(a[:, None, :], (s, n, w)).reshape(s * n, w)


def _tile_rows(a, n):
    h, w = a.shape
    return jnp.broadcast_to(a[None], (n, h, w)).reshape(n * h, w)


def _tril(n):
    r = lax.broadcasted_iota(jnp.int32, (n, n), 0)
    c = lax.broadcasted_iota(jnp.int32, (n, n), 1)
    return jnp.where(c <= r, 1.0, 0.0).astype(BF16)


def _ple_apply(h1, p, g_ple, w_gate, w_up):
    hn = _rms(h1, g_ple).astype(BF16)
    gate = _sigmoid(_dot(hn, w_gate))
    return h1 + gate * _dot(p.astype(BF16), w_up)


def _ffn_ple_kernel(h_ref, p_ref, gf_ref, wg_ref, wu_ref, wd_ref, gp_ref, wpg_ref, wpu_ref, o_ref):
    x = h_ref[...]
    xn = _rms(x, gf_ref[...]).astype(BF16)
    acc = x
    for c in range(D_FF // FFN_CHUNK):
        sl = slice(c * FFN_CHUNK, (c + 1) * FFN_CHUNK)
        hh = (_silu(_dot(xn, wg_ref[:, sl])) * _dot(xn, wu_ref[:, sl])).astype(BF16)
        acc = acc + _dot(hh, wd_ref[sl, :])
    o_ref[...] = _ple_apply(acc, p_ref[...], gp_ref[...], wpg_ref[...], wpu_ref[...])


def ffn_ple(h, p, g_ffn, wg, wu, wd, g_ple, wpg, wpu):
    m = h.shape[0]
    tm = min(TOKEN_TILE, m)
    row = lambda w: pl.BlockSpec((tm, w), lambda i: (i, 0))
    return pl.pallas_call(
        _ffn_ple_kernel,
        grid=(m // tm,),
        in_specs=[row(D_MODEL), row(PLE_DIM), _resident((1, D_MODEL)),
                  _resident((D_MODEL, D_FF)), _resident((D_MODEL, D_FF)), _resident((D_FF, D_MODEL)),
                  _resident((1, D_MODEL)), _resident((D_MODEL, D_MODEL)), _resident((PLE_DIM, D_MODEL))],
        out_specs=row(D_MODEL),
        out_shape=jax.ShapeDtypeStruct((m, D_MODEL), F32),
        compiler_params=_cparams("parallel"),
        name="ffn_ple",
    )(h, p, g_ffn, wg, wu, wd, g_ple, wpg, wpu)


def _moe_ple_kernel(h_ref, p_ref, gf_ref, wr_ref, wg_ref, wu_ref, wd_ref, gp_ref, wpg_ref, wpu_ref, gfin_ref,
                    upper_ref, lower_ref, o_ref, xn_s, comb_s, combt_s, acc_s, *, final_norm):
    e = pl.program_id(1)

    @pl.when(e == 0)
    def _():
        x = h_ref[...]
        xn = _rms(x, gf_ref[...]).astype(BF16)
        xn_s[...] = xn
        acc_s[...] = x
        logits = _dot(xn, wr_ref[...])
        lane = lax.broadcasted_iota(jnp.int32, logits.shape, 1).astype(F32)
        neg = jnp.float32(-jnp.inf)
        lg = jnp.where(lane < N_EXPERTS, logits, neg)
        m1 = jnp.max(lg, axis=-1, keepdims=True)
        i1 = jnp.min(jnp.where(lg == m1, lane, float(LANES)), axis=-1, keepdims=True)
        lg2 = jnp.where(lane == i1, neg, lg)
        m2 = jnp.max(lg2, axis=-1, keepdims=True)
        i2 = jnp.min(jnp.where(lg2 == m2, lane, float(LANES)), axis=-1, keepdims=True)
        e2 = jnp.exp(m2 - m1)
        den = 1.0 + e2
        comb = jnp.where(lane == i1, 1.0 / den, jnp.where(lane == i2, e2 / den, 0.0))
        comb_s[...] = comb
        combt_s[...] = comb.T[:N_EXPERTS, :]

    tm = xn_s.shape[0]
    lane = lax.broadcasted_iota(jnp.int32, comb_s.shape, 1)
    gate_col = jnp.sum(jnp.where(lane == e, comb_s[...], 0.0), axis=-1, keepdims=True)
    gate_row = combt_s[pl.ds(e, 1), :]
    sel_row = jnp.where(gate_row > 0.0, 1.0, 0.0)
    sel_col = jnp.where(gate_col > 0.0, 1.0, 0.0)
    rank_row = _dot(jnp.broadcast_to(sel_row, (8, tm)).astype(BF16), upper_ref[...])[0:1, :]
    rank_col = _dot(lower_ref[...], jnp.broadcast_to(sel_col, (tm, MOE_SCATTER)).astype(BF16))
    n_chunks = (jnp.max(rank_row).astype(jnp.int32) + (MOE_CHUNK - 1)) // MOE_CHUNK
    slot_row = lax.broadcasted_iota(jnp.int32, (MOE_CHUNK, tm), 0).astype(F32) + 1.0
    slot_lane = lax.broadcasted_iota(jnp.int32, (tm, MOE_SCATTER), 1)
    slot_col = slot_lane.astype(F32) + 1.0
    routed_col = (sel_col > 0.0) & (slot_lane < MOE_CHUNK)

    def chunk(ci, carry):
        base = (ci * MOE_CHUNK).astype(F32)
        gather = jnp.where((sel_row > 0.0) & (rank_row == slot_row + base), 1.0, 0.0).astype(BF16)
        xg = _dot(gather, xn_s[...]).astype(BF16)
        hh = (_silu(_dot(xg, wg_ref[0])) * _dot(xg, wu_ref[0])).astype(BF16)
        y = _dot(hh, wd_ref[0]).astype(BF16)
        y = jnp.concatenate([y, jnp.zeros((MOE_SCATTER - MOE_CHUNK, D_MODEL), BF16)], axis=0)
        scatter = jnp.where(routed_col & (rank_col == slot_col + base), 1.0, 0.0).astype(BF16)
        acc_s[...] += gate_col * _dot(scatter, y)
        return carry

    lax.fori_loop(0, n_chunks, chunk, 0)

    @pl.when(e == N_EXPERTS - 1)
    def _():
        h2 = _ple_apply(acc_s[...], p_ref[...], gp_ref[...], wpg_ref[...], wpu_ref[...])
        if final_norm:
            h2 = _rms(h2, gfin_ref[...])
        o_ref[...] = h2


def moe_ple(h, p, g_ffn, wr, wg, wu, wd, g_ple, wpg, wpu, g_final, final_norm):
    m = h.shape[0]
    tm = min(MOE_TILE, m)
    row = lambda w: pl.BlockSpec((tm, w), lambda i, e: (i, 0))
    expert = lambda a, b: pl.BlockSpec((1, a, b), lambda i, e: (e, 0, 0))
    upper = jnp.asarray(np.triu(np.ones((tm, tm), np.float32)), BF16)
    lower = jnp.asarray(np.tril(np.ones((tm, tm), np.float32)), BF16)
    return pl.pallas_call(
        functools.partial(_moe_ple_kernel, final_norm=final_norm),
        grid=(m // tm, N_EXPERTS),
        in_specs=[row(D_MODEL), row(PLE_DIM), _resident((1, D_MODEL)), _resident((D_MODEL, LANES)),
                  expert(D_MODEL, D_FF_EXPERT), expert(D_MODEL, D_FF_EXPERT), expert(D_FF_EXPERT, D_MODEL),
                  _resident((1, D_MODEL)), _resident((D_MODEL, D_MODEL)), _resident((PLE_DIM, D_MODEL)),
                  _resident((1, D_MODEL)), _resident((tm, tm)), _resident((tm, tm))],
        out_specs=row(D_MODEL),
        out_shape=jax.ShapeDtypeStruct((m, D_MODEL), F32),
        scratch_shapes=[pltpu.VMEM((tm, D_MODEL), BF16), pltpu.VMEM((tm, LANES), F32),
                        pltpu.VMEM((N_EXPERTS, tm), F32), pltpu.VMEM((tm, D_MODEL), F32)],
        compiler_params=_cparams("parallel", "arbitrary"),
        name="moe_ple",
    )(h, p, g_ffn, wr, wg, wu, wd, g_ple, wpg, wpu, g_final, upper, lower)


def _log_sigmoid(z):
    return jnp.minimum(z, 0.0) - jnp.log1p(jnp.exp(-jnp.abs(z)))


def _fox_proj_kernel(x_ref, g_ref, wqt_ref, wk_ref, wv_ref, wf_ref, bf_ref, sel_ref, ones_ref, hsel_ref,
                     k_out, v_out, lf_out, qt_out, vt_out, kaug_out, ft_out, qn_out, kn_out, carry):
    tm = x_ref.shape[1]

    @pl.when(pl.program_id(1) == 0)
    def _():
        carry[...] = jnp.zeros_like(carry)

    xn = _rms(x_ref[0], g_ref[...])
    xb = xn.astype(BF16)
    xnt = xn.T.astype(BF16)
    k = _dot(xb, wk_ref[...])
    v = _dot(xb, wv_ref[...])
    k_out[0] = k
    v_out[0] = v
    qtb = (_dot(wqt_ref[...], xnt) * (FOX_HD ** -0.5)).astype(BF16)
    qt_out[0] = qtb
    kb = k.astype(BF16)
    qsq = qtb.astype(F32)
    ksq = kb.astype(F32)
    qn2 = _dot(hsel_ref[...], (qsq * qsq).astype(BF16))
    kn2 = _dot_nt((ksq * ksq).astype(BF16), hsel_ref[...])
    qn_out[0, 0] = jnp.broadcast_to(jnp.max(qn2, axis=1, keepdims=True), (LANES, LANES))[:FOX_HEADS]
    kn_out[0, 0] = jnp.broadcast_to(jnp.max(kn2, axis=0, keepdims=True), (8, LANES))
    vt = v.T.astype(BF16)
    ones_rows = jnp.ones((FOX_VROWS - FOX_HD, tm), BF16)
    for h in range(FOX_HEADS):
        vt_out[0, 0, FOX_VROWS * h:FOX_VROWS * h + FOX_HD, :] = vt[FOX_HD * h:FOX_HD * (h + 1), :]
        vt_out[0, 0, FOX_VROWS * h + FOX_HD:FOX_VROWS * (h + 1), :] = ones_rows
    lf = _log_sigmoid(_dot(xb, wf_ref[...]) + bf_ref[...])
    lf_out[0] = lf[:, :FOX_HEADS]
    f = _dot3_left(_tril(tm), lf) + carry[...]
    carry[...] = f[tm - 1:tm, :]
    ft_out[0] = f.T[:FOX_HEADS, :]
    p1, p2, p3 = _split3(f)
    aug = ((_dot(p1, sel_ref[0]) + _dot(p2, sel_ref[1])) + _dot(p3, sel_ref[2]) + ones_ref[...]).astype(BF16)
    for p in range(FOX_PAIRS):
        kaug_out[0, :, 2 * LANES * p:2 * LANES * p + LANES] = kb[:, LANES * p:LANES * (p + 1)]
        kaug_out[0, :, 2 * LANES * p + LANES:2 * LANES * (p + 1)] = aug[:, LANES * p:LANES * (p + 1)]


def _fox_aug_constants():
    h = np.arange(FOX_HEADS)
    sel = np.zeros((3, LANES, FOX_PAIRS * LANES), np.float32)
    for r in range(3):
        sel[r, h, LANES * (h // 2) + 16 * (h % 2) + r] = 1.0
    p = np.arange(FOX_PAIRS)
    ones = np.zeros((1, FOX_PAIRS * LANES), np.float32)
    for r in range(3):
        ones[0, LANES * p + 32 + r] = 1.0
    ch = np.arange(D_MODEL)
    hsel = np.zeros((LANES, D_MODEL), np.float32)
    hsel[ch // FOX_HD, ch] = 1.0
    return jnp.asarray(sel, BF16), jnp.asarray(ones), jnp.asarray(hsel, BF16)


def fox_proj(x, g, wq, wk, wv, wf, bf):
    b, t, d = x.shape
    tm = min(ATTN_BLOCK, t)
    sel, ones, hsel = _fox_aug_constants()
    wf_pad = jnp.zeros((d, LANES), F32).at[:, :FOX_HEADS].set(wf).astype(BF16)
    bf_pad = jnp.zeros((1, LANES), F32).at[0, :FOX_HEADS].set(bf)
    tile = lambda w: pl.BlockSpec((1, tm, w), lambda bi, ti: (bi, ti, 0))
    return pl.pallas_call(
        _fox_proj_kernel,
        grid=(b, t // tm),
        in_specs=[tile(d), _resident((1, d)), _resident((d, d)), _resident((d, d)), _resident((d, d)),
                  _resident((d, LANES)), _resident((1, LANES)), _resident((3, LANES, FOX_PAIRS * LANES)),
                  _resident((1, FOX_PAIRS * LANES)), _resident((LANES, d))],
        out_specs=[tile(d), tile(d), tile(FOX_HEADS),
                   pl.BlockSpec((1, d, tm), lambda bi, ti: (bi, 0, ti)),
                   pl.BlockSpec((1, 1, FOX_HEADS * FOX_VROWS, tm), lambda bi, ti: (bi, ti, 0, 0)),
                   tile(2 * d),
                   pl.BlockSpec((1, FOX_HEADS, tm), lambda bi, ti: (bi, 0, ti)),
                   pl.BlockSpec((1, 1, FOX_HEADS, LANES), lambda bi, ti: (bi, ti, 0, 0)),
                   pl.BlockSpec((1, 1, 8, LANES), lambda bi, ti: (bi, ti, 0, 0))],
        out_shape=[jax.ShapeDtypeStruct((b, t, d), F32), jax.ShapeDtypeStruct((b, t, d), F32),
                   jax.ShapeDtypeStruct((b, t, FOX_HEADS), F32),
                   jax.ShapeDtypeStruct((b, d, t), BF16),
                   jax.ShapeDtypeStruct((b, t // tm, FOX_HEADS * FOX_VROWS, tm), BF16),
                   jax.ShapeDtypeStruct((b, t, 2 * d), BF16),
                   jax.ShapeDtypeStruct((b, FOX_HEADS, t), F32),
                   jax.ShapeDtypeStruct((b, t // tm, FOX_HEADS, LANES), F32),
                   jax.ShapeDtypeStruct((b, t // tm, 8, LANES), F32)],
        scratch_shapes=[pltpu.VMEM((1, LANES), F32)],
        compiler_params=_cparams("arbitrary", "arbitrary"),
        name="fox_proj",
    )(x, g, wq.T.astype(BF16), wk.astype(BF16), wv.astype(BF16), wf_pad, bf_pad, sel, ones, hsel)


def fox_dead_block_bounds(qn2, kn2, ft, blk):
    b, nb = qn2.shape[:2]
    qn = jnp.sqrt(qn2[..., 0] * NORM_SLACK).transpose(0, 2, 1)
    kn = jnp.sqrt(kn2[:, :, 0, :FOX_HEADS] * NORM_SLACK).transpose(0, 2, 1)
    kn_upto = lax.cummax(kn, axis=2)
    fq_first = ft[:, :, ::blk]
    fk_last = ft[:, :, blk - 1::blk]
    ub = qn[:, :, :, None] * kn_upto[:, :, None, :] + fq_first[:, :, :, None] - fk_last[:, :, None, :]
    ub = jnp.pad(ub, ((0, 0), (0, 0), (0, 0), (0, LANES - nb)), constant_values=-jnp.inf)
    return ub.reshape(b, FOX_PAIRS, 2, nb, LANES).transpose(0, 1, 3, 2, 4)


def _fox_attn_kernel(qt_ref, kaug_ref, vt_ref, ft_ref, ub_ref, ot_ref, qaug, m_s, acc, s_buf):
    tq = qt_ref.shape[2]
    tk = vt_ref.shape[3]
    i = pl.program_id(2)
    zeros = lambda r: jnp.zeros((r, tq), BF16)
    minus = jnp.full((16, tq), -1.0, BF16)
    row = lax.broadcasted_iota(jnp.int32, (16, tq), 0)
    for h in range(2):
        qaug[h, 64 * h:64 * (h + 1)] = qt_ref[0, 64 * h:64 * (h + 1), :]
        qaug[h, 64 * (1 - h):64 * (2 - h)] = zeros(64)
        qaug[h, LANES + 16 * h:LANES + 16 * (h + 1)] = minus
        qaug[h, LANES + 16 * (1 - h):LANES + 16 * (2 - h)] = zeros(16)
        p1, p2, p3 = _split3(ft_ref[0, 0, h:h + 1, :])
        blk = jnp.where(row == 0, p1.astype(F32),
                        jnp.where(row == 1, p2.astype(F32), jnp.where(row == 2, p3.astype(F32), 0.0)))
        qaug[h, LANES + 32:LANES + 48] = blk.astype(BF16)
        qaug[h, LANES + 48:2 * LANES] = zeros(LANES - 48)
    m_s[...] = jnp.full(m_s.shape, -jnp.inf, F32)
    acc[...] = jnp.zeros(acc.shape, F32)

    chains = [(h, c) for h in range(2) for c in range(tq // ATTN_STRIP)]

    def qk(n, j, slot):
        h, c = chains[n]
        qa = qaug[h, :, c * ATTN_STRIP:(c + 1) * ATTN_STRIP]
        for r in range(tk // ATTN_KEYS):
            kj = kaug_ref[0, pl.ds(pl.multiple_of(j * tk + r * ATTN_KEYS, ATTN_KEYS), ATTN_KEYS), :]
            s_buf[slot, n, r * ATTN_KEYS:(r + 1) * ATTN_KEYS, :] = _dot(kj, qa)

    def softmax(n, slot, masked):
        h, c = chains[n]
        qs = slice(c * ATTN_STRIP, (c + 1) * ATTN_STRIP)
        st = s_buf[slot, n]
        if masked:
            kpos = lax.broadcasted_iota(jnp.int32, st.shape, 0)
            qpos = lax.broadcasted_iota(jnp.int32, st.shape, 1) + c * ATTN_STRIP
            st = jnp.where(kpos <= qpos, st, -jnp.inf)
        m_prev = m_s[h, :, qs]
        m_new = jnp.maximum(m_prev, jnp.max(st, axis=0, keepdims=True))
        alpha = jnp.exp(m_prev - m_new)
        p = jnp.exp(st - m_new)
        m_s[h, :, qs] = m_new
        return p.astype(BF16), alpha

    def pv(n, j, p, alpha):
        h, c = chains[n]
        qs = slice(c * ATTN_STRIP, (c + 1) * ATTN_STRIP)
        acc[h, :, qs] = alpha * acc[h, :, qs] + _dot(vt_ref[0, j, FOX_VROWS * h:FOX_VROWS * (h + 1), :], p)

    def block(j, slot, masked, prefetch=True):
        jn = jnp.maximum(j - 1, 0)
        pa = {}
        if prefetch:
            for n in range(len(chains)):
                qk(n, jn, 1 - slot)
        for n in range(len(chains)):
            pa[n] = softmax(n, slot, masked)
            if n >= 1:
                pv(n - 1, j, *pa.pop(n - 1))
        pv(len(chains) - 1, j, *pa.pop(len(chains) - 1))

    for n in range(len(chains)):
        qk(n, i, 0)
    block(i, 0, True)

    lane = lax.broadcasted_iota(jnp.int32, (1, LANES), 1)
    live = lane < 0
    for h in range(2):
        m_low = jnp.min(m_s[h], axis=1, keepdims=True)
        live = live | (ub_ref[0, 0, 0, h:h + 1, :] >= m_low - DEAD_LOGIT_GAP)
    n_live = jnp.sum(jnp.where(live & (lane < i), 1.0, 0.0)).astype(jnp.int32)

    def pair(u, carry):
        j = i - 1 - 2 * u
        block(j, 1, False)
        block(j - 1, 0, False)
        return carry

    n_pairs = n_live // 2
    even = n_live % 2 == 0
    lax.fori_loop(0, jnp.where(even, jnp.maximum(n_pairs - 1, 0), n_pairs), pair, 0)

    @pl.when(even & (n_live >= 2))
    def _():
        block(i - n_live + 1, 1, False)
        block(i - n_live, 0, False, prefetch=False)

    @pl.when(n_live % 2 == 1)
    def _():
        block(i - n_live, 1, False, prefetch=False)

    for h in range(2):
        ot_ref[0, 64 * h:64 * (h + 1), :] = (acc[h, :FOX_HD] / acc[h, FOX_HD:FOX_HD + 1]).astype(BF16)


def fox_attention(qt, kaug, vt, ft, ub):
    b, d, t = qt.shape
    blk = vt.shape[3]
    assert t // blk <= LANES
    ft = ft.reshape(b, FOX_PAIRS, 2, t)
    return pl.pallas_call(
        _fox_attn_kernel,
        grid=(b, FOX_PAIRS, t // blk),
        in_specs=[pl.BlockSpec((1, LANES, blk), lambda bi, p, i: (bi, p, i)),
                  pl.BlockSpec((1, t, 2 * LANES), lambda bi, p, i: (bi, 0, p)),
                  pl.BlockSpec((1, t // blk, 2 * FOX_VROWS, blk), lambda bi, p, i: (bi, 0, p, 0)),
                  pl.BlockSpec((1, 1, 2, blk), lambda bi, p, i: (bi, p, 0, i)),
                  pl.BlockSpec((1, 1, 1, 2, LANES), lambda bi, p, i: (bi, p, i, 0, 0))],
        out_specs=pl.BlockSpec((1, LANES, blk), lambda bi, p, i: (bi, p, i)),
        out_shape=jax.ShapeDtypeStruct((b, d, t), BF16),
        scratch_shapes=[pltpu.VMEM((2, 2 * LANES, blk), BF16), pltpu.VMEM((2, 1, blk), F32),
                        pltpu.VMEM((2, FOX_VROWS, blk), F32),
                        pltpu.VMEM((2, 2 * blk // ATTN_STRIP, blk, ATTN_STRIP), F32)],
        compiler_params=_cparams("parallel", "parallel", "arbitrary"),
        name="fox_attention",
    )(qt, kaug, vt, ft, ub)


def _out_proj_t_kernel(h_ref, ot_ref, w_ref, o_ref):
    o_ref[0] = h_ref[0] + pl.dot(ot_ref[0], w_ref[...], trans_a=True)


def out_proj_t(h, ot, w):
    b, t, d = h.shape
    tm = min(TOKEN_TILE, t)
    tile = pl.BlockSpec((1, tm, d), lambda bi, ti: (bi, ti, 0))
    return pl.pallas_call(
        _out_proj_t_kernel,
        grid=(b, t // tm),
        in_specs=[tile, pl.BlockSpec((1, d, tm), lambda bi, ti: (bi, 0, ti)), _resident((d, d))],
        out_specs=tile,
        out_shape=jax.ShapeDtypeStruct((b, t, d), F32),
        compiler_params=_cparams("parallel", "parallel"),
        name="fox_out_proj",
    )(h, ot, w.astype(BF16))


def _fox_sample_kernel(x_ref, g_ref, wq_ref, wk_ref, wv_ref, wf_ref, bf_ref, wo_ref, ck_ref, cv_ref, clf_ref,
                       h_out, k_out, v_out, lf_out):
    s = x_ref.shape[1]
    pl_len = ck_ref.shape[1]
    x = x_ref[0]
    xb = _rms(x, g_ref[...]).astype(BF16)
    q = _dot(xb, wq_ref[...]) * (FOX_HD ** -0.5)
    k = _dot(xb, wk_ref[...])
    v = _dot(xb, wv_ref[...])
    lf = _log_sigmoid(_dot(xb, wf_ref[...]) + bf_ref[...])
    k_out[0] = k
    v_out[0] = v
    lf_out[0] = lf[:, :FOX_HEADS]

    rows = s * FOX_HEADS
    rh = lax.broadcasted_iota(jnp.int32, (rows, D_MODEL), 0) % FOX_HEADS
    lh = lax.broadcasted_iota(jnp.int32, (rows, D_MODEL), 1) // FOX_HD
    head_lanes = rh == lh
    qbd = jnp.where(head_lanes, _rep_rows(q, FOX_HEADS), 0.0).astype(BF16)

    cblk = 256
    carry = jnp.zeros((1, LANES), F32)
    tri = _tril(cblk)
    f_parts = []
    for c in range(pl_len // cblk):
        fc = _dot3_left(tri, clf_ref[0, c * cblk:(c + 1) * cblk, :]) + carry
        carry = fc[cblk - 1:cblk, :]
        f_parts.append(fc)
    f_cache = jnp.concatenate(f_parts, axis=0)
    pad_rows = lambda a: jnp.concatenate([a, jnp.zeros((LANES - s, a.shape[1]), a.dtype)], axis=0)
    f_new = _dot3_left(_tril(LANES), pad_rows(lf)) + carry

    lane = lax.broadcasted_iota(jnp.int32, (rows, LANES), 1)
    row_head = lax.broadcasted_iota(jnp.int32, (rows, LANES), 0) % FOX_HEADS
    fq = jnp.sum(jnp.where(lane == row_head, _rep_rows(f_new[:s], FOX_HEADS), 0.0), axis=-1, keepdims=True)
    fk_cache = _tile_rows(f_cache.T[:FOX_HEADS, :], s)
    fk_new = _tile_rows(f_new.T[:FOX_HEADS, :], s)

    kc = 512
    qk = [_dot_nt(qbd, ck_ref[0, c * kc:(c + 1) * kc, :].astype(BF16)) for c in range(pl_len // kc)]
    s_cache = jnp.concatenate(qk, axis=1) + fq - fk_cache
    s_new = _dot_nt(qbd, pad_rows(k).astype(BF16)) + fq - fk_new
    qi = lax.broadcasted_iota(jnp.int32, (rows, LANES), 0) // FOX_HEADS
    kj = lax.broadcasted_iota(jnp.int32, (rows, LANES), 1)
    s_new = jnp.where(kj <= qi, s_new, -jnp.inf)
    m = jnp.maximum(jnp.max(s_cache, axis=-1, keepdims=True), jnp.max(s_new, axis=-1, keepdims=True))
    p_cache = jnp.exp(s_cache - m)
    p_new = jnp.exp(s_new - m)
    den = jnp.sum(p_cache, axis=-1, keepdims=True) + jnp.sum(p_new, axis=-1, keepdims=True)
    pv = _dot(p_new.astype(BF16), pad_rows(v).astype(BF16))
    pb = p_cache.astype(BF16)
    for c in range(pl_len // kc):
        pv = pv + _dot(pb[:, c * kc:(c + 1) * kc], cv_ref[0, c * kc:(c + 1) * kc, :].astype(BF16))
    o_heads = jnp.where(head_lanes, pv / den, 0.0).astype(BF16)
    pick = (lax.broadcasted_iota(jnp.int32, (s, rows), 1) // FOX_HEADS
            == lax.broadcasted_iota(jnp.int32, (s, rows), 0))
    o = _dot(jnp.where(pick, 1.0, 0.0).astype(BF16), o_heads)
    h_out[0] = x + _dot(o.astype(BF16), wo_ref[...])


def fox_sample(x, g, wq, wk, wv, wf, bf, wo, ck, cv, clf):
    b, s, d = x.shape
    p = ck.shape[1]
    wf_pad = jnp.zeros((d, LANES), F32).at[:, :FOX_HEADS].set(wf).astype(BF16)
    bf_pad = jnp.zeros((1, LANES), F32).at[0, :FOX_HEADS].set(bf)
    clf_pad = jnp.zeros((b, p, LANES), F32).at[:, :, :FOX_HEADS].set(clf)
    seq = lambda n, w: pl.BlockSpec((1, n, w), lambda bi: (bi, 0, 0))
    return pl.pallas_call(
        _fox_sample_kernel,
        grid=(b,),
        in_specs=[seq(s, d), _resident((1, d)), _resident((d, d)), _resident((d, d)), _resident((d, d)),
                  _resident((d, LANES)), _resident((1, LANES)), _resident((d, d)),
                  seq(p, d), seq(p, d), seq(p, LANES)],
        out_specs=[seq(s, d), seq(s, d), seq(s, d), seq(s, FOX_HEADS)],
        out_shape=[jax.ShapeDtypeStruct((b, s, d), F32)] * 3 + [jax.ShapeDtypeStruct((b, s, FOX_HEADS), F32)],
        compiler_params=_cparams("parallel"),
        name="fox_sample",
    )(x, g, wq.astype(BF16), wk.astype(BF16), wv.astype(BF16), wf_pad, bf_pad, wo.astype(BF16),
      ck.reshape(b, p, d), cv.reshape(b, p, d), clf_pad)


def _halo_specs(b, t, tm, d, halo):
    main = pl.BlockSpec((1, tm, d), lambda bi, ti: (bi, ti, 0))
    if t == tm:
        return [main]
    per = tm // halo
    return [main, pl.BlockSpec((1, halo, d), lambda bi, ti: (bi, jnp.maximum(ti * per - 1, 0), 0))]


def _conf_kernel(*refs, has_halo):
    if has_halo:
        x_ref, xh_ref, *refs = refs
    else:
        x_ref, *refs = refs
    (hist_ref, g_ref, w1_ref, b1_ref, wdw_ref, bdw_ref, gln_ref, bln_ref, w2_ref, b2_ref,
     o_ref, hs_ref, uext, ushift) = refs
    tm = x_ref.shape[1]

    def glu(rows):
        ag = _dot(_rms(rows, g_ref[...]).astype(BF16), w1_ref[...]) + b1_ref[...]
        return ag[:, :D_MODEL] * _sigmoid(ag[:, D_MODEL:])

    x = x_ref[0]
    if has_halo:
        prev = jnp.where(pl.program_id(1) == 0, hist_ref[0], glu(xh_ref[0]))
    else:
        prev = hist_ref[0]
    uext[0:CONF_HALO] = prev
    uext[CONF_HALO:CONF_HALO + tm] = glu(x)
    uext[CONF_HALO + tm:] = jnp.zeros((8, D_MODEL), F32)
    off = CONF_HALO - (CONF_K - 1)
    y = jnp.broadcast_to(bdw_ref[...], (tm, D_MODEL))
    for s in range(8):
        ushift[...] = uext[s:s + tm + CONF_HALO, :]
        for k in range(CONF_K):
            if (off + k) % 8 == s:
                a = off + k - s
                y = y + wdw_ref[k:k + 1, :] * ushift[a:a + tm, :]
    hs_ref[0] = uext[tm + off:tm + CONF_HALO, :]
    mu = jnp.mean(y, axis=-1, keepdims=True)
    yc = y - mu
    var = jnp.mean(yc * yc, axis=-1, keepdims=True)
    v = _silu(yc * lax.rsqrt(var + EPS) * gln_ref[...] + bln_ref[...])
    o_ref[0] = x + _dot(v.astype(BF16), w2_ref[...]) + b2_ref[...]


def conformer(x, hist, g, w1, b1, wdw, bdw, gln, bln, w2, b2):
    b, t, d = x.shape
    tm = min(TOKEN_TILE, t)
    xs = _halo_specs(b, t, tm, d, CONF_HALO)
    hist_pad = jnp.pad(hist, ((0, 0), (CONF_HALO - (CONF_K - 1), 0), (0, 0)))
    per_b = lambda n: pl.BlockSpec((1, n, d), lambda bi, ti: (bi, 0, 0))
    return pl.pallas_call(
        functools.partial(_conf_kernel, has_halo=len(xs) == 2),
        grid=(b, t // tm),
        in_specs=xs + [per_b(CONF_HALO), _resident((1, d)), _resident((d, 2 * d)), _resident((1, 2 * d)),
                       _resident((CONF_K, d)), _resident((1, d)), _resident((1, d)), _resident((1, d)),
                       _resident((d, d)), _resident((1, d))],
        out_specs=[pl.BlockSpec((1, tm, d), lambda bi, ti: (bi, ti, 0)), per_b(CONF_K - 1)],
        out_shape=[jax.ShapeDtypeStruct((b, t, d), F32), jax.ShapeDtypeStruct((b, CONF_K - 1, d), F32)],
        scratch_shapes=[pltpu.VMEM((CONF_HALO + tm + 8, d), F32),
                        pltpu.VMEM((CONF_HALO + tm, d), F32)],
        compiler_params=_cparams("parallel", "arbitrary"),
        name="conformer",
    )(*([x] * len(xs)), hist_pad, g, w1.astype(BF16), b1.reshape(1, -1), wdw, bdw.reshape(1, -1),
      gln.reshape(1, -1), bln.reshape(1, -1), w2.astype(BF16), b2.reshape(1, -1))


def _pool_kernel(*refs, has_halo, pos0):
    if has_halo:
        x_ref, xh_ref, *refs = refs
    else:
        x_ref, *refs = refs
    hist_ref, g_ref, w_ref, b_ref, scale_ref, o_ref, hs_ref, ext = refs
    tm = x_ref.shape[1]
    i = pl.program_id(1)
    x = x_ref[0]
    xn = _rms(x, g_ref[...])
    if has_halo:
        prev = jnp.where(i == 0, hist_ref[0], _rms(xh_ref[0], g_ref[...]))
    else:
        prev = hist_ref[0]
    ext[0:POOL_HALO] = prev
    ext[POOL_HALO:] = xn
    hs_ref[0] = ext[tm + POOL_HALO - POOL_HIST:tm + POOL_HALO, :]
    pos = pos0 + i * tm + lax.broadcasted_iota(jnp.int32, (tm, 1), 0)
    ys = []
    for gi, wlen in enumerate(POOL_WINDOWS):
        lanes = slice(gi * POOL_GROUP, (gi + 1) * POOL_GROUP)
        s = ext[POOL_HALO:POOL_HALO + tm, lanes]
        for k in range(1, wlen):
            s = s + ext[POOL_HALO - k:POOL_HALO - k + tm, lanes]
        cnt = jnp.minimum(pos + 1, wlen).astype(F32)
        dg = s / cnt - xn[:, lanes]
        ys.append(_dot(dg.astype(BF16), w_ref[gi]))
    y = (jnp.concatenate(ys, axis=1) + b_ref[...]) * scale_ref[...]
    o_ref[0] = x + y


def pool_mixer(x, hist, pos0, g, w, bias, scale):
    b, t, d = x.shape
    tm = min(TOKEN_TILE, t)
    xs = _halo_specs(b, t, tm, d, POOL_HALO)
    hist_pad = jnp.pad(hist, ((0, 0), (POOL_HALO - POOL_HIST, 0), (0, 0)))
    per_b = lambda n: pl.BlockSpec((1, n, d), lambda bi, ti: (bi, 0, 0))
    ng = len(POOL_WINDOWS)
    return pl.pallas_call(
        functools.partial(_pool_kernel, has_halo=len(xs) == 2, pos0=pos0),
        grid=(b, t // tm),
        in_specs=xs + [per_b(POOL_HALO), _resident((1, d)), _resident((ng, POOL_GROUP, POOL_GROUP)),
                       _resident((1, d)), _resident((1, d))],
        out_specs=[pl.BlockSpec((1, tm, d), lambda bi, ti: (bi, ti, 0)), per_b(POOL_HIST)],
        out_shape=[jax.ShapeDtypeStruct((b, t, d), F32), jax.ShapeDtypeStruct((b, POOL_HIST, d), F32)],
        scratch_shapes=[pltpu.VMEM((POOL_HALO + tm, d), F32)],
        compiler_params=_cparams("parallel", "arbitrary"),
        name="pool_mixer",
    )(*([x] * len(xs)), hist_pad, g, w.astype(BF16), bias.reshape(1, -1), scale.reshape(1, -1))


def _ssd_kernel(*refs, has_halo, valid):
    if has_halo:
        x_ref, xh_ref, *refs = refs
    else:
        x_ref, *refs = refs
    (chist_ref, h0_ref, g_ref, wz_ref, wxbc_ref, wdt_ref, wconv_ref, bconv_ref, dtb_ref, a_ref, dskip_ref,
     gn_ref, wout_ref, exp_ref, expt_ref, o_ref, cs_out_ref, h_out_ref, ext, hstate) = refs
    L = x_ref.shape[1]
    i = pl.program_id(1)
    gw = SSD_HPG * SSD_HD
    gn = SSD_GROUPS * SSD_N

    @pl.when(i == 0)
    def _():
        hstate[...] = h0_ref[0]

    x = x_ref[0]
    xb = _rms(x, g_ref[...]).astype(BF16)
    z = _dot(xb, wz_ref[...])
    dtr = _dot(xb, wdt_ref[...])
    if has_halo:
        prev = jnp.where(i == 0, chist_ref[0], _dot(_rms(xh_ref[0], g_ref[...]).astype(BF16), wxbc_ref[...]))
    else:
        prev = chist_ref[0]
    ext[0:SSD_HALO] = prev
    ext[SSD_HALO:] = _dot(xb, wxbc_ref[...])
    off = SSD_HALO - (SSD_CONV_K - 1)
    y = jnp.broadcast_to(bconv_ref[...], (L, SSD_CONV_DIM))
    for k in range(SSD_CONV_K):
        y = y + wconv_ref[k:k + 1, :] * ext[off + k:off + k + L, :]
    cs_out_ref[0] = ext[valid + off:valid + SSD_HALO, :]
    xbc = _silu(y)
    xs = xbc[:, :SSD_INNER]
    bm = xbc[:, SSD_INNER:SSD_INNER + gn].astype(BF16)
    cm = xbc[:, SSD_INNER + gn:].astype(BF16)

    head_lane = lax.broadcasted_iota(jnp.int32, (L, LANES), 1) < SSD_HEADS
    live = head_lane
    if valid < L:
        live = live & (lax.broadcasted_iota(jnp.int32, (L, LANES), 0) < valid)
    dt = jnp.where(live, _softplus(dtr + dtb_ref[...]), 0.0)
    cs = _dot3_left(_tril(L), dt * a_ref[...])
    cst = cs.T
    dtt = dt.T
    to_end = jnp.exp(cs[L - 1:L, :] - cs) * dt
    td_x = _dot(to_end.astype(BF16), exp_ref[...])
    ecs_x = _dot3(jnp.exp(cs), exp_ref[...])
    chunk_decay = jnp.broadcast_to(jnp.exp(cst[:, L - 1:L]), (LANES, LANES))
    xs_b = xs.astype(BF16)
    xtd = xs * td_x
    r = lax.broadcasted_iota(jnp.int32, (L, L), 0)
    c = lax.broadcasted_iota(jnp.int32, (L, L), 1)
    causal = c <= r
    low_half = lax.broadcasted_iota(jnp.int32, (L, 2 * SSD_HD), 1) < SSD_HD

    y_groups = []
    for g in range(SSD_GROUPS):
        cg = cm[:, g * SSD_N:(g + 1) * SSD_N]
        bg = bm[:, g * SSD_N:(g + 1) * SSD_N]
        cb = _dot_nt(cg, bg)
        hg = hstate[g]
        y_off = _dot_nt(cg, hg.astype(BF16)) * ecs_x[:, g * gw:(g + 1) * gw]
        pairs = []
        for pr in range(SSD_HPG // 2):
            lo = g * gw + pr * 2 * SSD_HD
            xpair = xs_b[:, lo:lo + 2 * SSD_HD]
            acc = None
            for hh in range(2):
                e = g * SSD_HPG + 2 * pr + hh
                seg = cs[:, e:e + 1] - cst[e:e + 1, :]
                w = cb * jnp.exp(jnp.where(causal, seg, -jnp.inf)) * dtt[e:e + 1, :]
                xm = jnp.where(low_half == (hh == 0), xpair, jnp.zeros_like(xpair))
                term = _dot(w.astype(BF16), xm)
                acc = term if acc is None else acc + term
            pairs.append(acc)
        y_groups.append(jnp.concatenate(pairs, axis=1) + y_off)
        s_g = _dot(xtd[:, g * gw:(g + 1) * gw].T.astype(BF16), bg)
        hstate[g] = hg * _dot3_left(expt_ref[g], chunk_decay) + s_g

    ys = (jnp.concatenate(y_groups, axis=1) + dskip_ref[...] * xs) * _silu(z)
    normed = []
    for g in range(SSD_GROUPS):
        yg = ys[:, g * gw:(g + 1) * gw]
        normed.append(yg * lax.rsqrt(jnp.mean(yg * yg, axis=-1, keepdims=True) + EPS))
    yn = jnp.concatenate(normed, axis=1) * gn_ref[...]
    o_ref[0] = x + _dot(yn.astype(BF16), wout_ref[...])

    @pl.when(i == pl.num_programs(1) - 1)
    def _():
        h_out_ref[0] = hstate[...]


def _ssd_expand_constants():
    ch = np.arange(SSD_INNER)
    exp = np.zeros((LANES, SSD_INNER), np.float32)
    exp[ch // SSD_HD, ch] = 1.0
    gw = SSD_HPG * SSD_HD
    rows = np.arange(gw)
    expt = np.zeros((SSD_GROUPS, gw, LANES), np.float32)
    for g in range(SSD_GROUPS):
        expt[g, rows, g * SSD_HPG + rows // SSD_HD] = 1.0
    return jnp.asarray(exp, BF16), jnp.asarray(expt, BF16)


def ssd_mixer(x, conv_hist, h0, g, w_in, w_conv, b_conv, dt_bias, a_log, d_skip, g_norm, w_out):
    b, t, d = x.shape
    valid = t
    if t < LANES:
        x = jnp.pad(x, ((0, 0), (0, LANES - t), (0, 0)))
    tp = x.shape[1]
    L = min(SSD_CHUNK, tp)
    valid = L if tp > L else valid
    xs = _halo_specs(b, tp, L, d, SSD_HALO)
    gw = SSD_HPG * SSD_HD
    hist_pad = jnp.pad(conv_hist, ((0, 0), (SSD_HALO - (SSD_CONV_K - 1), 0), (0, 0)))
    wz = w_in[:, :SSD_INNER].astype(BF16)
    wxbc = w_in[:, SSD_INNER:SSD_INNER + SSD_CONV_DIM].astype(BF16)
    pad_heads = lambda v: jnp.zeros((v.shape[0], LANES), F32).at[:, :SSD_HEADS].set(v)
    wdt = pad_heads(w_in[:, SSD_INNER + SSD_CONV_DIM:]).astype(BF16)
    dtb = pad_heads(dt_bias.reshape(1, -1))
    a = pad_heads(-jnp.exp(a_log.astype(F32)).reshape(1, -1))
    dskip = jnp.repeat(d_skip, SSD_HD).reshape(1, -1)
    exp, expt = _ssd_expand_constants()
    per_b3 = lambda n, w: pl.BlockSpec((1, n, w), lambda bi, ti: (bi, 0, 0))
    state = pl.BlockSpec((1, SSD_GROUPS, gw, SSD_N), lambda bi, ti: (bi, 0, 0, 0))
    out, conv_new, h_new = pl.pallas_call(
        functools.partial(_ssd_kernel, has_halo=len(xs) == 2, valid=valid),
        grid=(b, tp // L),
        in_specs=xs + [per_b3(SSD_HALO, SSD_CONV_DIM), state, _resident((1, d)),
                       _resident((d, SSD_INNER)), _resident((d, SSD_CONV_DIM)), _resident((d, LANES)),
                       _resident((SSD_CONV_K, SSD_CONV_DIM)), _resident((1, SSD_CONV_DIM)),
                       _resident((1, LANES)), _resident((1, LANES)), _resident((1, SSD_INNER)),
                       _resident((1, SSD_INNER)), _resident((SSD_INNER, d)),
                       _resident((LANES, SSD_INNER)), _resident((SSD_GROUPS, gw, LANES))],
        out_specs=[pl.BlockSpec((1, L, d), lambda bi, ti: (bi, ti, 0)),
                   per_b3(SSD_CONV_K - 1, SSD_CONV_DIM), state],
        out_shape=[jax.ShapeDtypeStruct((b, tp, d), F32),
                   jax.ShapeDtypeStruct((b, SSD_CONV_K - 1, SSD_CONV_DIM), F32),
                   jax.ShapeDtypeStruct((b, SSD_GROUPS, gw, SSD_N), F32)],
        scratch_shapes=[pltpu.VMEM((SSD_HALO + L, SSD_CONV_DIM), F32), pltpu.VMEM((SSD_GROUPS, gw, SSD_N), F32)],
        compiler_params=_cparams("parallel", "arbitrary"),
        name="ssd_mixer",
    )(*([x] * len(xs)), hist_pad, h0.reshape(b, SSD_GROUPS, gw, SSD_N), g, wz, wxbc, wdt, w_conv,
      b_conv.reshape(1, -1), dtb, a, dskip, g_norm.reshape(1, -1), w_out.astype(BF16), exp, expt)
    return out[:, :t], conv_new, h_new.reshape(b, SSD_HEADS, SSD_HD, SSD_N)


def kernel(x_prompt, x_sample, cache_fox_k, cache_fox_v, cache_fox_logf, state_conf_conv, state_ssd_conv, state_ssd, state_pool, p_prompt, p_sample, g_mix, g_ffn, g_ple, g_final, fox_w_q, fox_w_k, fox_w_v, fox_w_f, fox_b_f, fox_w_o, conf_w_pw1, conf_b_pw1, conf_w_dw, conf_b_dw, conf_g_ln, conf_b_ln, conf_w_pw2, conf_b_pw2, ssd_w_in, ssd_w_conv, ssd_b_conv, ssd_dt_bias, ssd_a_log, ssd_d, ssd_g_norm, ssd_w_out, pool_w, pool_b, pool_scale, ffn_w_gate, ffn_w_up, ffn_w_down, moe_w_router, moe_w_gate, moe_w_up, moe_w_down, ple_w_up, ple_w_gate):
    b, t, d = x_prompt.shape
    bs, s, _ = x_sample.shape
    past = cache_fox_k.shape[1]
    row = lambda v: v.reshape(1, -1)
    bf = lambda w: w.astype(BF16)
    flat = lambda a: a.reshape(-1, a.shape[-1])
    hp, hs = x_prompt, x_sample

    def dense_ffn(h, p, i):
        j = i // 2
        out = ffn_ple(flat(h), flat(p[i]), row(g_ffn[i]), bf(ffn_w_gate[j]), bf(ffn_w_up[j]), bf(ffn_w_down[j]),
                      row(g_ple[i]), bf(ple_w_gate[i]), bf(ple_w_up[i]))
        return out.reshape(h.shape)

    def moe_ffn(h, p, i, final_norm):
        j = i // 2
        wr = jnp.zeros((d, LANES), F32).at[:, :N_EXPERTS].set(moe_w_router[j]).astype(BF16)
        out = moe_ple(flat(h), flat(p[i]), row(g_ffn[i]), wr, bf(moe_w_gate[j]), bf(moe_w_up[j]),
                      bf(moe_w_down[j]), row(g_ple[i]), bf(ple_w_gate[i]), bf(ple_w_up[i]), row(g_final),
                      final_norm)
        return out.reshape(h.shape)

    fox_k_p, fox_v_p, fox_lf_p, qt, vt, kaug, ft, qn2, kn2 = fox_proj(hp, row(g_mix[0]), fox_w_q, fox_w_k,
                                                                      fox_w_v, fox_w_f, fox_b_f)
    ub = fox_dead_block_bounds(qn2, kn2, ft, vt.shape[3])
    hp = out_proj_t(hp, fox_attention(qt, kaug, vt, ft, ub), fox_w_o)
    hs, fox_k_s, fox_v_s, fox_lf_s = fox_sample(hs, row(g_mix[0]), fox_w_q, fox_w_k, fox_w_v, fox_w_f, fox_b_f,
                                                fox_w_o, cache_fox_k, cache_fox_v, cache_fox_logf)
    hp = dense_ffn(hp, p_prompt, 0)
    hs = dense_ffn(hs, p_sample, 0)

    conf_args = (row(g_mix[1]), conf_w_pw1, conf_b_pw1, conf_w_dw, conf_b_dw, conf_g_ln, conf_b_ln,
                 conf_w_pw2, conf_b_pw2)
    hp, conf_p = conformer(hp, jnp.zeros((b, CONF_K - 1, d), F32), *conf_args)
    hs, conf_s = conformer(hs, state_conf_conv, *conf_args)
    hp = moe_ffn(hp, p_prompt, 1, False)
    hs = moe_ffn(hs, p_sample, 1, False)

    ssd_args = (row(g_mix[2]), ssd_w_in, ssd_w_conv, ssd_b_conv, ssd_dt_bias, ssd_a_log, ssd_d, ssd_g_norm,
                ssd_w_out)
    hp, ssdc_p, ssdh_p = ssd_mixer(hp, jnp.zeros((b, SSD_CONV_K - 1, SSD_CONV_DIM), F32),
                                   jnp.zeros((b, SSD_HEADS, SSD_HD, SSD_N), F32), *ssd_args)
    hs, ssdc_s, ssdh_s = ssd_mixer(hs, state_ssd_conv, state_ssd, *ssd_args)
    hp = dense_ffn(hp, p_prompt, 2)
    hs = dense_ffn(hs, p_sample, 2)

    pool_args = (row(g_mix[3]), pool_w, pool_b, pool_scale)
    hp, pool_p = pool_mixer(hp, jnp.zeros((b, POOL_HIST, d), F32), 0, *pool_args)
    hs, pool_s = pool_mixer(hs, state_pool, past, *pool_args)
    y_prompt = moe_ffn(hp, p_prompt, 3, True)
    y_sample = moe_ffn(hs, p_sample, 3, True)

    heads = lambda a: a.reshape(a.shape[0], a.shape[1], FOX_HEADS, FOX_HD)
    return (y_prompt, y_sample, heads(fox_k_p), heads(fox_v_p), fox_lf_p, heads(fox_k_s), heads(fox_v_s), fox_lf_s,
            conf_p, conf_s, ssdc_p, ssdc_s, ssdh_p, ssdh_s, pool_p, pool_s)
```

```python
import functools

import jax
import jax.numpy as jnp
import numpy as np
from jax import lax
from jax.experimental import pallas as pl
from jax.experimental.pallas import tpu as pltpu

F32 = jnp.float32
BF16 = jnp.bfloat16

D_MODEL = 1024
EPS = 1e-6
PLE_DIM = 256
FOX_HEADS = 16
FOX_HD = 64
FOX_PAIRS = FOX_HEADS // 2
FOX_VROWS = FOX_HD + 16
CONF_K = 31
CONF_HALO = 32
SSD_INNER = 2048
SSD_HD = 64
SSD_HEADS = 32
SSD_GROUPS = 4
SSD_HPG = SSD_HEADS // SSD_GROUPS
SSD_N = 128
SSD_CONV_K = 4
SSD_CONV_DIM = SSD_INNER + 2 * SSD_GROUPS * SSD_N
SSD_HALO = 8
POOL_WINDOWS = (2, 4, 8, 16)
POOL_GROUP = D_MODEL // len(POOL_WINDOWS)
POOL_HIST = 15
POOL_HALO = 16
N_EXPERTS = 8
D_FF = 2816
D_FF_EXPERT = 1408
LANES = 128

VMEM_LIMIT_BYTES = 56 * 1024 * 1024
TOKEN_TILE = 512
ATTN_BLOCK = 512
ATTN_STRIP = 256
ATTN_KEYS = 512
NORM_SLACK = 1.01
DEAD_LOGIT_GAP = 106.0
SSD_CHUNK = 256
FFN_CHUNK = 256
MOE_TILE = 1024
MOE_SUBTILE = 512
MOE_CHUNK = 160
MOE_SCATTER = 256


def _cparams(*sem):
    return pltpu.CompilerParams(dimension_semantics=sem, vmem_limit_bytes=VMEM_LIMIT_BYTES)


def _resident(shape):
    n = len(shape)
    return pl.BlockSpec(shape, lambda *_: (0,) * n, pipeline_mode=pl.Buffered(1))


def _dot(a, b):
    return jnp.dot(a, b, preferred_element_type=F32)


def _dot_nt(a, b):
    return lax.dot_general(a, b, (((1,), (1,)), ((), ())), preferred_element_type=F32)


def _rms(x, g):
    return x * lax.rsqrt(jnp.mean(x * x, axis=-1, keepdims=True) + EPS) * g


def _sigmoid(x):
    return 1.0 / (1.0 + jnp.exp(-x))


def _silu(x):
    return x * _sigmoid(x)


def _softplus(x):
    return jnp.maximum(x, 0.0) + jnp.log1p(jnp.exp(-jnp.abs(x)))


def _split3(x):
    p1 = x.astype(BF16)
    r1 = x - p1.astype(F32)
    p2 = r1.astype(BF16)
    r2 = r1 - p2.astype(F32)
    return p1, p2, r2.astype(BF16)


def _dot3(a_f32, b_bf16):
    p1, p2, p3 = _split3(a_f32)
    return (_dot(p1, b_bf16) + _dot(p2, b_bf16)) + _dot(p3, b_bf16)


def _dot3_left(a_bf16, b_f32):
    p1, p2, p3 = _split3(b_f32)
    return (_dot(a_bf16, p1) + _dot(a_bf16, p2)) + _dot(a_bf16, p3)


def _rep_rows(a, n):
    s, w = a.shape
    return jnp.broadcast_to(a[:, None, :], (s, n, w)).reshape(s * n, w)


def _tile_rows(a, n):
    h, w = a.shape
    return jnp.broadcast_to(a[None], (n, h, w)).reshape(n * h, w)


def _tril(n):
    r = lax.broadcasted_iota(jnp.int32, (n, n), 0)
    c = lax.broadcasted_iota(jnp.int32, (n, n), 1)
    return jnp.where(c <= r, 1.0, 0.0).astype(BF16)


def _ple_apply(h1, p, g_ple, w_gate, w_up):
    hn = _rms(h1, g_ple).astype(BF16)
    gate = _sigmoid(_dot(hn, w_gate))
    return h1 + gate * _dot(p.astype(BF16), w_up)


def _ffn_ple_kernel(h_ref, p_ref, gf_ref, wg_ref, wu_ref, wd_ref, gp_ref, wpg_ref, wpu_ref, o_ref):
    x = h_ref[...]
    xn = _rms(x, gf_ref[...]).astype(BF16)
    acc = x
    for c in range(D_FF // FFN_CHUNK):
        sl = slice(c * FFN_CHUNK, (c + 1) * FFN_CHUNK)
        hh = (_silu(_dot(xn, wg_ref[:, sl])) * _dot(xn, wu_ref[:, sl])).astype(BF16)
        acc = acc + _dot(hh, wd_ref[sl, :])
    o_ref[...] = _ple_apply(acc, p_ref[...], gp_ref[...], wpg_ref[...], wpu_ref[...])


def ffn_ple(h, p, g_ffn, wg, wu, wd, g_ple, wpg, wpu):
    m = h.shape[0]
    tm = min(TOKEN_TILE, m)
    row = lambda w: pl.BlockSpec((tm, w), lambda i: (i, 0))
    return pl.pallas_call(
        _ffn_ple_kernel,
        grid=(m // tm,),
        in_specs=[row(D_MODEL), row(PLE_DIM), _resident((1, D_MODEL)),
                  _resident((D_MODEL, D_FF)), _resident((D_MODEL, D_FF)), _resident((D_FF, D_MODEL)),
                  _resident((1, D_MODEL)), _resident((D_MODEL, D_MODEL)), _resident((PLE_DIM, D_MODEL))],
        out_specs=row(D_MODEL),
        out_shape=jax.ShapeDtypeStruct((m, D_MODEL), F32),
        compiler_params=_cparams("parallel"),
        name="ffn_ple",
    )(h, p, g_ffn, wg, wu, wd, g_ple, wpg, wpu)


def _moe_ple_kernel(h_ref, p_ref, gf_ref, wr_ref, wg_ref, wu_ref, wd_ref, gp_ref, wpg_ref, wpu_ref, gfin_ref,
                    upper_ref, lower_ref, o_ref, xn_s, comb_s, combt_s, acc_s, *, final_norm):
    e = pl.program_id(1)

    @pl.when(e == 0)
    def _():
        x = h_ref[...]
        xn = _rms(x, gf_ref[...]).astype(BF16)
        xn_s[...] = xn
        acc_s[...] = x
        logits = _dot(xn, wr_ref[...])
        lane = lax.broadcasted_iota(jnp.int32, logits.shape, 1).astype(F32)
        neg = jnp.float32(-jnp.inf)
        lg = jnp.where(lane < N_EXPERTS, logits, neg)
        m1 = jnp.max(lg, axis=-1, keepdims=True)
        i1 = jnp.min(jnp.where(lg == m1, lane, float(LANES)), axis=-1, keepdims=True)
        lg2 = jnp.where(lane == i1, neg, lg)
        m2 = jnp.max(lg2, axis=-1, keepdims=True)
        i2 = jnp.min(jnp.where(lg2 == m2, lane, float(LANES)), axis=-1, keepdims=True)
        e2 = jnp.exp(m2 - m1)
        den = 1.0 + e2
        comb = jnp.where(lane == i1, 1.0 / den, jnp.where(lane == i2, e2 / den, 0.0))
        comb_s[...] = comb
        combt_s[...] = comb.T[:N_EXPERTS, :]

    ts = upper_ref.shape[0]
    for sb in range(xn_s.shape[0] // ts):
        rows = slice(sb * ts, (sb + 1) * ts)
        lane = lax.broadcasted_iota(jnp.int32, (ts, LANES), 1)
        gate_col = jnp.sum(jnp.where(lane == e, comb_s[rows, :], 0.0), axis=-1, keepdims=True)
        gate_row = combt_s[pl.ds(e, 1), rows]
        sel_row = jnp.where(gate_row > 0.0, 1.0, 0.0)
        sel_col = jnp.where(gate_col > 0.0, 1.0, 0.0)
        rank_row = _dot(jnp.broadcast_to(sel_row, (8, ts)).astype(BF16), upper_ref[...])[0:1, :]
        rank_col = _dot(lower_ref[...], jnp.broadcast_to(sel_col, (ts, MOE_SCATTER)).astype(BF16))
        n_chunks = (jnp.max(rank_row).astype(jnp.int32) + (MOE_CHUNK - 1)) // MOE_CHUNK
        slot_row = lax.broadcasted_iota(jnp.int32, (MOE_CHUNK, ts), 0).astype(F32) + 1.0
        slot_lane = lax.broadcasted_iota(jnp.int32, (ts, MOE_SCATTER), 1)
        slot_col = slot_lane.astype(F32) + 1.0
        routed_col = (sel_col > 0.0) & (slot_lane < MOE_CHUNK)

        def chunk(ci, carry, rows=rows, gate_col=gate_col, sel_row=sel_row, rank_row=rank_row, rank_col=rank_col,
                  slot_row=slot_row, slot_col=slot_col, routed_col=routed_col):
            base = (ci * MOE_CHUNK).astype(F32)
            gather = jnp.where((sel_row > 0.0) & (rank_row == slot_row + base), 1.0, 0.0).astype(BF16)
            xg = _dot(gather, xn_s[rows, :]).astype(BF16)
            hh = (_silu(_dot(xg, wg_ref[0])) * _dot(xg, wu_ref[0])).astype(BF16)
            y = _dot(hh, wd_ref[0]).astype(BF16)
            y = jnp.concatenate([y, jnp.zeros((MOE_SCATTER - MOE_CHUNK, D_MODEL), BF16)], axis=0)
            scatter = jnp.where(routed_col & (rank_col == slot_col + base), 1.0, 0.0).astype(BF16)
            acc_s[rows, :] += gate_col * _dot(scatter, y)
            return carry

        lax.fori_loop(0, n_chunks, chunk, 0)

    @pl.when(e == N_EXPERTS - 1)
    def _():
        h2 = _ple_apply(acc_s[...], p_ref[...], gp_ref[...], wpg_ref[...], wpu_ref[...])
        if final_norm:
            h2 = _rms(h2, gfin_ref[...])
        o_ref[...] = h2


def moe_ple(h, p, g_ffn, wr, wg, wu, wd, g_ple, wpg, wpu, g_final, final_norm):
    m = h.shape[0]
    tm = min(MOE_TILE, m)
    row = lambda w: pl.BlockSpec((tm, w), lambda i, e: (i, 0))
    expert = lambda a, b: pl.BlockSpec((1, a, b), lambda i, e: (e, 0, 0))
    ts = min(MOE_SUBTILE, tm)
    upper = jnp.asarray(np.triu(np.ones((ts, ts), np.float32)), BF16)
    lower = jnp.asarray(np.tril(np.ones((ts, ts), np.float32)), BF16)
    return pl.pallas_call(
        functools.partial(_moe_ple_kernel, final_norm=final_norm),
        grid=(m // tm, N_EXPERTS),
        in_specs=[row(D_MODEL), row(PLE_DIM), _resident((1, D_MODEL)), _resident((D_MODEL, LANES)),
                  expert(D_MODEL, D_FF_EXPERT), expert(D_MODEL, D_FF_EXPERT), expert(D_FF_EXPERT, D_MODEL),
                  _resident((1, D_MODEL)), _resident((D_MODEL, D_MODEL)), _resident((PLE_DIM, D_MODEL)),
                  _resident((1, D_MODEL)), _resident((ts, ts)), _resident((ts, ts))],
        out_specs=row(D_MODEL),
        out_shape=jax.ShapeDtypeStruct((m, D_MODEL), F32),
        scratch_shapes=[pltpu.VMEM((tm, D_MODEL), BF16), pltpu.VMEM((tm, LANES), F32),
                        pltpu.VMEM((N_EXPERTS, tm), F32), pltpu.VMEM((tm, D_MODEL), F32)],
        compiler_params=_cparams("parallel", "arbitrary"),
        name="moe_ple",
    )(h, p, g_ffn, wr, wg, wu, wd, g_ple, wpg, wpu, g_final, upper, lower)


def _log_sigmoid(z):
    return jnp.minimum(z, 0.0) - jnp.log1p(jnp.exp(-jnp.abs(z)))


def _fox_proj_kernel(x_ref, g_ref, wqt_ref, wk_ref, wv_ref, wf_ref, bf_ref, sel_ref, ones_ref, hsel_ref,
                     k_out, v_out, lf_out, qt_out, vt_out, kaug_out, ft_out, qn_out, kn_out, carry):
    tm = x_ref.shape[1]

    @pl.when(pl.program_id(1) == 0)
    def _():
        carry[...] = jnp.zeros_like(carry)

    xn = _rms(x_ref[0], g_ref[...])
    xb = xn.astype(BF16)
    xnt = xn.T.astype(BF16)
    k = _dot(xb, wk_ref[...])
    v = _dot(xb, wv_ref[...])
    k_out[0] = k
    v_out[0] = v
    qtb = (_dot(wqt_ref[...], xnt) * (FOX_HD ** -0.5)).astype(BF16)
    qt_out[0] = qtb
    kb = k.astype(BF16)
    qsq = qtb.astype(F32)
    ksq = kb.astype(F32)
    qn2 = _dot(hsel_ref[...], (qsq * qsq).astype(BF16))
    kn2 = _dot_nt((ksq * ksq).astype(BF16), hsel_ref[...])
    qn_out[0, 0] = jnp.broadcast_to(jnp.max(qn2, axis=1, keepdims=True), (LANES, LANES))[:FOX_HEADS]
    kn_out[0, 0] = jnp.broadcast_to(jnp.max(kn2, axis=0, keepdims=True), (8, LANES))
    vt = v.T.astype(BF16)
    ones_rows = jnp.ones((FOX_VROWS - FOX_HD, tm), BF16)
    for h in range(FOX_HEADS):
        vt_out[0, 0, FOX_VROWS * h:FOX_VROWS * h + FOX_HD, :] = vt[FOX_HD * h:FOX_HD * (h + 1), :]
        vt_out[0, 0, FOX_VROWS * h + FOX_HD:FOX_VROWS * (h + 1), :] = ones_rows
    lf = _log_sigmoid(_dot(xb, wf_ref[...]) + bf_ref[...])
    lf_out[0] = lf[:, :FOX_HEADS]
    f = _dot3_left(_tril(tm), lf) + carry[...]
    carry[...] = f[tm - 1:tm, :]
    ft_out[0] = f.T[:FOX_HEADS, :]
    p1, p2, p3 = _split3(f)
    aug = ((_dot(p1, sel_ref[0]) + _dot(p2, sel_ref[1])) + _dot(p3, sel_ref[2]) + ones_ref[...]).astype(BF16)
    for p in range(FOX_PAIRS):
        kaug_out[0, :, 2 * LANES * p:2 * LANES * p + LANES] = kb[:, LANES * p:LANES * (p + 1)]
        kaug_out[0, :, 2 * LANES * p + LANES:2 * LANES * (p + 1)] = aug[:, LANES * p:LANES * (p + 1)]


def _fox_aug_constants():
    h = np.arange(FOX_HEADS)
    sel = np.zeros((3, LANES, FOX_PAIRS * LANES), np.float32)
    for r in range(3):
        sel[r, h, LANES * (h // 2) + 16 * (h % 2) + r] = 1.0
    p = np.arange(FOX_PAIRS)
    ones = np.zeros((1, FOX_PAIRS * LANES), np.float32)
    for r in range(3):
        ones[0, LANES * p + 32 + r] = 1.0
    ch = np.arange(D_MODEL)
    hsel = np.zeros((LANES, D_MODEL), np.float32)
    hsel[ch // FOX_HD, ch] = 1.0
    return jnp.asarray(sel, BF16), jnp.asarray(ones), jnp.asarray(hsel, BF16)


def fox_proj(x, g, wq, wk, wv, wf, bf):
    b, t, d = x.shape
    tm = min(ATTN_BLOCK, t)
    sel, ones, hsel = _fox_aug_constants()
    wf_pad = jnp.zeros((d, LANES), F32).at[:, :FOX_HEADS].set(wf).astype(BF16)
    bf_pad = jnp.zeros((1, LANES), F32).at[0, :FOX_HEADS].set(bf)
    tile = lambda w: pl.BlockSpec((1, tm, w), lambda bi, ti: (bi, ti, 0))
    return pl.pallas_call(
        _fox_proj_kernel,
        grid=(b, t // tm),
        in_specs=[tile(d), _resident((1, d)), _resident((d, d)), _resident((d, d)), _resident((d, d)),
                  _resident((d, LANES)), _resident((1, LANES)), _resident((3, LANES, FOX_PAIRS * LANES)),
                  _resident((1, FOX_PAIRS * LANES)), _resident((LANES, d))],
        out_specs=[tile(d), tile(d), tile(FOX_HEADS),
                   pl.BlockSpec((1, d, tm), lambda bi, ti: (bi, 0, ti)),
                   pl.BlockSpec((1, 1, FOX_HEADS * FOX_VROWS, tm), lambda bi, ti: (bi, ti, 0, 0)),
                   tile(2 * d),
                   pl.BlockSpec((1, FOX_HEADS, tm), lambda bi, ti: (bi, 0, ti)),
                   pl.BlockSpec((1, 1, FOX_HEADS, LANES), lambda bi, ti: (bi, ti, 0, 0)),
                   pl.BlockSpec((1, 1, 8, LANES), lambda bi, ti: (bi, ti, 0, 0))],
        out_shape=[jax.ShapeDtypeStruct((b, t, d), F32), jax.ShapeDtypeStruct((b, t, d), F32),
                   jax.ShapeDtypeStruct((b, t, FOX_HEADS), F32),
                   jax.ShapeDtypeStruct((b, d, t), BF16),
                   jax.ShapeDtypeStruct((b, t // tm, FOX_HEADS * FOX_VROWS, tm), BF16),
                   jax.ShapeDtypeStruct((b, t, 2 * d), BF16),
                   jax.ShapeDtypeStruct((b, FOX_HEADS, t), F32),
                   jax.ShapeDtypeStruct((b, t // tm, FOX_HEADS, LANES), F32),
                   jax.ShapeDtypeStruct((b, t // tm, 8, LANES), F32)],
        scratch_shapes=[pltpu.VMEM((1, LANES), F32)],
        compiler_params=_cparams("arbitrary", "arbitrary"),
        name="fox_proj",
    )(x, g, wq.T.astype(BF16), wk.astype(BF16), wv.astype(BF16), wf_pad, bf_pad, sel, ones, hsel)


def fox_dead_block_bounds(qn2, kn2, ft, blk):
    b, nb = qn2.shape[:2]
    qn = jnp.sqrt(qn2[..., 0] * NORM_SLACK).transpose(0, 2, 1)
    kn = jnp.sqrt(kn2[:, :, 0, :FOX_HEADS] * NORM_SLACK).transpose(0, 2, 1)
    kn_upto = lax.cummax(kn, axis=2)
    fq_first = ft[:, :, ::blk]
    fk_last = ft[:, :, blk - 1::blk]
    ub = qn[:, :, :, None] * kn_upto[:, :, None, :] + fq_first[:, :, :, None] - fk_last[:, :, None, :]
    ub = jnp.pad(ub, ((0, 0), (0, 0), (0, 0), (0, LANES - nb)), constant_values=-jnp.inf)
    return ub.reshape(b, FOX_PAIRS, 2, nb, LANES).transpose(0, 1, 3, 2, 4)


def _fox_attn_kernel(qt_ref, kaug_ref, vt_ref, ft_ref, ub_ref, ot_ref, qaug, m_s, acc, s_buf):
    tq = qt_ref.shape[2]
    tk = vt_ref.shape[3]
    i = pl.program_id(2)
    zeros = lambda r: jnp.zeros((r, tq), BF16)
    minus = jnp.full((16, tq), -1.0, BF16)
    row = lax.broadcasted_iota(jnp.int32, (16, tq), 0)
    for h in range(2):
        qaug[h, 64 * h:64 * (h + 1)] = qt_ref[0, 64 * h:64 * (h + 1), :]
        qaug[h, 64 * (1 - h):64 * (2 - h)] = zeros(64)
        qaug[h, LANES + 16 * h:LANES + 16 * (h + 1)] = minus
        qaug[h, LANES + 16 * (1 - h):LANES + 16 * (2 - h)] = zeros(16)
        p1, p2, p3 = _split3(ft_ref[0, 0, h:h + 1, :])
        blk = jnp.where(row == 0, p1.astype(F32),
                        jnp.where(row == 1, p2.astype(F32), jnp.where(row == 2, p3.astype(F32), 0.0)))
        qaug[h, LANES + 32:LANES + 48] = blk.astype(BF16)
        qaug[h, LANES + 48:2 * LANES] = zeros(LANES - 48)
    m_s[...] = jnp.full(m_s.shape, -jnp.inf, F32)
    acc[...] = jnp.zeros(acc.shape, F32)

    chains = [(h, c) for h in range(2) for c in range(tq // ATTN_STRIP)]

    def qk(n, j, slot):
        h, c = chains[n]
        qa = qaug[h, :, c * ATTN_STRIP:(c + 1) * ATTN_STRIP]
        for r in range(tk // ATTN_KEYS):
            kj = kaug_ref[0, pl.ds(pl.multiple_of(j * tk + r * ATTN_KEYS, ATTN_KEYS), ATTN_KEYS), :]
            s_buf[slot, n, r * ATTN_KEYS:(r + 1) * ATTN_KEYS, :] = _dot(kj, qa)

    def softmax(n, slot, masked):
        h, c = chains[n]
        qs = slice(c * ATTN_STRIP, (c + 1) * ATTN_STRIP)
        st = s_buf[slot, n]
        if masked:
            kpos = lax.broadcasted_iota(jnp.int32, st.shape, 0)
            qpos = lax.broadcasted_iota(jnp.int32, st.shape, 1) + c * ATTN_STRIP
            st = jnp.where(kpos <= qpos, st, -jnp.inf)
        m_prev = m_s[h, :, qs]
        m_new = jnp.maximum(m_prev, jnp.max(st, axis=0, keepdims=True))
        alpha = jnp.exp(m_prev - m_new)
        p = jnp.exp(st - m_new)
        m_s[h, :, qs] = m_new
        return p.astype(BF16), alpha

    def pv(n, j, p, alpha):
        h, c = chains[n]
        qs = slice(c * ATTN_STRIP, (c + 1) * ATTN_STRIP)
        acc[h, :, qs] = alpha * acc[h, :, qs] + _dot(vt_ref[0, j, FOX_VROWS * h:FOX_VROWS * (h + 1), :], p)

    def block(j, slot, masked, prefetch=True):
        jn = jnp.maximum(j - 1, 0)
        pa = {}
        if prefetch:
            for n in range(len(chains)):
                qk(n, jn, 1 - slot)
        for n in range(len(chains)):
            pa[n] = softmax(n, slot, masked)
            if n >= 1:
                pv(n - 1, j, *pa.pop(n - 1))
        pv(len(chains) - 1, j, *pa.pop(len(chains) - 1))

    for n in range(len(chains)):
        qk(n, i, 0)
    block(i, 0, True)

    lane = lax.broadcasted_iota(jnp.int32, (1, LANES), 1)
    live = lane < 0
    for h in range(2):
        m_low = jnp.min(m_s[h], axis=1, keepdims=True)
        live = live | (ub_ref[0, 0, 0, h:h + 1, :] >= m_low - DEAD_LOGIT_GAP)
    n_live = jnp.sum(jnp.where(live & (lane < i), 1.0, 0.0)).astype(jnp.int32)

    def pair(u, carry):
        j = i - 1 - 2 * u
        block(j, 1, False)
        block(j - 1, 0, False)
        return carry

    n_pairs = n_live // 2
    even = n_live % 2 == 0
    lax.fori_loop(0, jnp.where(even, jnp.maximum(n_pairs - 1, 0), n_pairs), pair, 0)

    @pl.when(even & (n_live >= 2))
    def _():
        block(i - n_live + 1, 1, False)
        block(i - n_live, 0, False, prefetch=False)

    @pl.when(n_live % 2 == 1)
    def _():
        block(i - n_live, 1, False, prefetch=False)

    for h in range(2):
        ot_ref[0, 64 * h:64 * (h + 1), :] = (acc[h, :FOX_HD] / acc[h, FOX_HD:FOX_HD + 1]).astype(BF16)


def fox_attention(qt, kaug, vt, ft, ub):
    b, d, t = qt.shape
    blk = vt.shape[3]
    assert t // blk <= LANES
    ft = ft.reshape(b, FOX_PAIRS, 2, t)
    return pl.pallas_call(
        _fox_attn_kernel,
        grid=(b, FOX_PAIRS, t // blk),
        in_specs=[pl.BlockSpec((1, LANES, blk), lambda bi, p, i: (bi, p, i)),
                  pl.BlockSpec((1, t, 2 * LANES), lambda bi, p, i: (bi, 0, p)),
                  pl.BlockSpec((1, t // blk, 2 * FOX_VROWS, blk), lambda bi, p, i: (bi, 0, p, 0)),
                  pl.BlockSpec((1, 1, 2, blk), lambda bi, p, i: (bi, p, 0, i)),
                  pl.BlockSpec((1, 1, 1, 2, LANES), lambda bi, p, i: (bi, p, i, 0, 0))],
        out_specs=pl.BlockSpec((1, LANES, blk), lambda bi, p, i: (bi, p, i)),
        out_shape=jax.ShapeDtypeStruct((b, d, t), BF16),
        scratch_shapes=[pltpu.VMEM((2, 2 * LANES, blk), BF16), pltpu.VMEM((2, 1, blk), F32),
                        pltpu.VMEM((2, FOX_VROWS, blk), F32),
                        pltpu.VMEM((2, 2 * blk // ATTN_STRIP, blk, ATTN_STRIP), F32)],
        compiler_params=_cparams("parallel", "parallel", "arbitrary"),
        name="fox_attention",
    )(qt, kaug, vt, ft, ub)


def _out_proj_t_kernel(h_ref, ot_ref, w_ref, o_ref):
    o_ref[0] = h_ref[0] + pl.dot(ot_ref[0], w_ref[...], trans_a=True)


def out_proj_t(h, ot, w):
    b, t, d = h.shape
    tm = min(TOKEN_TILE, t)
    tile = pl.BlockSpec((1, tm, d), lambda bi, ti: (bi, ti, 0))
    return pl.pallas_call(
        _out_proj_t_kernel,
        grid=(b, t // tm),
        in_specs=[tile, pl.BlockSpec((1, d, tm), lambda bi, ti: (bi, 0, ti)), _resident((d, d))],
        out_specs=tile,
        out_shape=jax.ShapeDtypeStruct((b, t, d), F32),
        compiler_params=_cparams("parallel", "parallel"),
        name="fox_out_proj",
    )(h, ot, w.astype(BF16))


def _fox_sample_kernel(x_ref, g_ref, wq_ref, wk_ref, wv_ref, wf_ref, bf_ref, wo_ref, ck_ref, cv_ref, clf_ref,
                       h_out, k_out, v_out, lf_out):
    s = x_ref.shape[1]
    pl_len = ck_ref.shape[1]
    x = x_ref[0]
    xb = _rms(x, g_ref[...]).astype(BF16)
    q = _dot(xb, wq_ref[...]) * (FOX_HD ** -0.5)
    k = _dot(xb, wk_ref[...])
    v = _dot(xb, wv_ref[...])
    lf = _log_sigmoid(_dot(xb, wf_ref[...]) + bf_ref[...])
    k_out[0] = k
    v_out[0] = v
    lf_out[0] = lf[:, :FOX_HEADS]

    rows = s * FOX_HEADS
    rh = lax.broadcasted_iota(jnp.int32, (rows, D_MODEL), 0) % FOX_HEADS
    lh = lax.broadcasted_iota(jnp.int32, (rows, D_MODEL), 1) // FOX_HD
    head_lanes = rh == lh
    qbd = jnp.where(head_lanes, _rep_rows(q, FOX_HEADS), 0.0).astype(BF16)

    cblk = 256
    carry = jnp.zeros((1, LANES), F32)
    tri = _tril(cblk)
    f_parts = []
    for c in range(pl_len // cblk):
        fc = _dot3_left(tri, clf_ref[0, c * cblk:(c + 1) * cblk, :]) + carry
        carry = fc[cblk - 1:cblk, :]
        f_parts.append(fc)
    f_cache = jnp.concatenate(f_parts, axis=0)
    pad_rows = lambda a: jnp.concatenate([a, jnp.zeros((LANES - s, a.shape[1]), a.dtype)], axis=0)
    f_new = _dot3_left(_tril(LANES), pad_rows(lf)) + carry

    lane = lax.broadcasted_iota(jnp.int32, (rows, LANES), 1)
    row_head = lax.broadcasted_iota(jnp.int32, (rows, LANES), 0) % FOX_HEADS
    fq = jnp.sum(jnp.where(lane == row_head, _rep_rows(f_new[:s], FOX_HEADS), 0.0), axis=-1, keepdims=True)
    fk_cache = _tile_rows(f_cache.T[:FOX_HEADS, :], s)
    fk_new = _tile_rows(f_new.T[:FOX_HEADS, :], s)

    kc = 512
    qk = [_dot_nt(qbd, ck_ref[0, c * kc:(c + 1) * kc, :].astype(BF16)) for c in range(pl_len // kc)]
    s_cache = jnp.concatenate(qk, axis=1) + fq - fk_cache
    s_new = _dot_nt(qbd, pad_rows(k).astype(BF16)) + fq - fk_new
    qi = lax.broadcasted_iota(jnp.int32, (rows, LANES), 0) // FOX_HEADS
    kj = lax.broadcasted_iota(jnp.int32, (rows, LANES), 1)
    s_new = jnp.where(kj <= qi, s_new, -jnp.inf)
    m = jnp.maximum(jnp.max(s_cache, axis=-1, keepdims=True), jnp.max(s_new, axis=-1, keepdims=True))
    p_cache = jnp.exp(s_cache - m)
    p_new = jnp.exp(s_new - m)
    den = jnp.sum(p_cache, axis=-1, keepdims=True) + jnp.sum(p_new, axis=-1, keepdims=True)
    pv = _dot(p_new.astype(BF16), pad_rows(v).astype(BF16))
    pb = p_cache.astype(BF16)
    for c in range(pl_len // kc):
        pv = pv + _dot(pb[:, c * kc:(c + 1) * kc], cv_ref[0, c * kc:(c + 1) * kc, :].astype(BF16))
    o_heads = jnp.where(head_lanes, pv / den, 0.0).astype(BF16)
    pick = (lax.broadcasted_iota(jnp.int32, (s, rows), 1) // FOX_HEADS
            == lax.broadcasted_iota(jnp.int32, (s, rows), 0))
    o = _dot(jnp.where(pick, 1.0, 0.0).astype(BF16), o_heads)
    h_out[0] = x + _dot(o.astype(BF16), wo_ref[...])


def fox_sample(x, g, wq, wk, wv, wf, bf, wo, ck, cv, clf):
    b, s, d = x.shape
    p = ck.shape[1]
    wf_pad = jnp.zeros((d, LANES), F32).at[:, :FOX_HEADS].set(wf).astype(BF16)
    bf_pad = jnp.zeros((1, LANES), F32).at[0, :FOX_HEADS].set(bf)
    clf_pad = jnp.zeros((b, p, LANES), F32).at[:, :, :FOX_HEADS].set(clf)
    seq = lambda n, w: pl.BlockSpec((1, n, w), lambda bi: (bi, 0, 0))
    return pl.pallas_call(
        _fox_sample_kernel,
        grid=(b,),
        in_specs=[seq(s, d), _resident((1, d)), _resident((d, d)), _resident((d, d)), _resident((d, d)),
                  _resident((d, LANES)), _resident((1, LANES)), _resident((d, d)),
                  seq(p, d), seq(p, d), seq(p, LANES)],
        out_specs=[seq(s, d), seq(s, d), seq(s, d), seq(s, FOX_HEADS)],
        out_shape=[jax.ShapeDtypeStruct((b, s, d), F32)] * 3 + [jax.ShapeDtypeStruct((b, s, FOX_HEADS), F32)],
        compiler_params=_cparams("parallel"),
        name="fox_sample",
    )(x, g, wq.astype(BF16), wk.astype(BF16), wv.astype(BF16), wf_pad, bf_pad, wo.astype(BF16),
      ck.reshape(b, p, d), cv.reshape(b, p, d), clf_pad)


def _halo_specs(b, t, tm, d, halo):
    main = pl.BlockSpec((1, tm, d), lambda bi, ti: (bi, ti, 0))
    if t == tm:
        return [main]
    per = tm // halo
    return [main, pl.BlockSpec((1, halo, d), lambda bi, ti: (bi, jnp.maximum(ti * per - 1, 0), 0))]


def _conf_kernel(*refs, has_halo):
    if has_halo:
        x_ref, xh_ref, *refs = refs
    else:
        x_ref, *refs = refs
    (hist_ref, g_ref, w1_ref, b1_ref, wdw_ref, bdw_ref, gln_ref, bln_ref, w2_ref, b2_ref,
     o_ref, hs_ref, uext, ushift) = refs
    tm = x_ref.shape[1]

    def glu(rows):
        ag = _dot(_rms(rows, g_ref[...]).astype(BF16), w1_ref[...]) + b1_ref[...]
        return ag[:, :D_MODEL] * _sigmoid(ag[:, D_MODEL:])

    x = x_ref[0]
    if has_halo:
        prev = jnp.where(pl.program_id(1) == 0, hist_ref[0], glu(xh_ref[0]))
    else:
        prev = hist_ref[0]
    uext[0:CONF_HALO] = prev
    uext[CONF_HALO:CONF_HALO + tm] = glu(x)
    uext[CONF_HALO + tm:] = jnp.zeros((8, D_MODEL), F32)
    off = CONF_HALO - (CONF_K - 1)
    y = jnp.broadcast_to(bdw_ref[...], (tm, D_MODEL))
    for s in range(8):
        ushift[...] = uext[s:s + tm + CONF_HALO, :]
        for k in range(CONF_K):
            if (off + k) % 8 == s:
                a = off + k - s
                y = y + wdw_ref[k:k + 1, :] * ushift[a:a + tm, :]
    hs_ref[0] = uext[tm + off:tm + CONF_HALO, :]
    mu = jnp.mean(y, axis=-1, keepdims=True)
    yc = y - mu
    var = jnp.mean(yc * yc, axis=-1, keepdims=True)
    v = _silu(yc * lax.rsqrt(var + EPS) * gln_ref[...] + bln_ref[...])
    o_ref[0] = x + _dot(v.astype(BF16), w2_ref[...]) + b2_ref[...]


def conformer(x, hist, g, w1, b1, wdw, bdw, gln, bln, w2, b2):
    b, t, d = x.shape
    tm = min(TOKEN_TILE, t)
    xs = _halo_specs(b, t, tm, d, CONF_HALO)
    hist_pad = jnp.pad(hist, ((0, 0), (CONF_HALO - (CONF_K - 1), 0), (0, 0)))
    per_b = lambda n: pl.BlockSpec((1, n, d), lambda bi, ti: (bi, 0, 0))
    return pl.pallas_call(
        functools.partial(_conf_kernel, has_halo=len(xs) == 2),
        grid=(b, t // tm),
        in_specs=xs + [per_b(CONF_HALO), _resident((1, d)), _resident((d, 2 * d)), _resident((1, 2 * d)),
                       _resident((CONF_K, d)), _resident((1, d)), _resident((1, d)), _resident((1, d)),
                       _resident((d, d)), _resident((1, d))],
        out_specs=[pl.BlockSpec((1, tm, d), lambda bi, ti: (bi, ti, 0)), per_b(CONF_K - 1)],
        out_shape=[jax.ShapeDtypeStruct((b, t, d), F32), jax.ShapeDtypeStruct((b, CONF_K - 1, d), F32)],
        scratch_shapes=[pltpu.VMEM((CONF_HALO + tm + 8, d), F32),
                        pltpu.VMEM((CONF_HALO + tm, d), F32)],
        compiler_params=_cparams("parallel", "arbitrary"),
        name="conformer",
    )(*([x] * len(xs)), hist_pad, g, w1.astype(BF16), b1.reshape(1, -1), wdw, bdw.reshape(1, -1),
      gln.reshape(1, -1), bln.reshape(1, -1), w2.astype(BF16), b2.reshape(1, -1))


def _pool_kernel(*refs, has_halo, pos0):
    if has_halo:
        x_ref, xh_ref, *refs = refs
    else:
        x_ref, *refs = refs
    hist_ref, g_ref, w_ref, b_ref, scale_ref, o_ref, hs_ref, ext = refs
    tm = x_ref.shape[1]
    i = pl.program_id(1)
    x = x_ref[0]
    xn = _rms(x, g_ref[...])
    if has_halo:
        prev = jnp.where(i == 0, hist_ref[0], _rms(xh_ref[0], g_ref[...]))
    else:
        prev = hist_ref[0]
    ext[0:POOL_HALO] = prev
    ext[POOL_HALO:] = xn
    hs_ref[0] = ext[tm + POOL_HALO - POOL_HIST:tm + POOL_HALO, :]
    pos = pos0 + i * tm + lax.broadcasted_iota(jnp.int32, (tm, 1), 0)
    ys = []
    for gi, wlen in enumerate(POOL_WINDOWS):
        lanes = slice(gi * POOL_GROUP, (gi + 1) * POOL_GROUP)
        s = ext[POOL_HALO:POOL_HALO + tm, lanes]
        for k in range(1, wlen):
            s = s + ext[POOL_HALO - k:POOL_HALO - k + tm, lanes]
        cnt = jnp.minimum(pos + 1, wlen).astype(F32)
        dg = s / cnt - xn[:, lanes]
        ys.append(_dot(dg.astype(BF16), w_ref[gi]))
    y = (jnp.concatenate(ys, axis=1) + b_ref[...]) * scale_ref[...]
    o_ref[0] = x + y


def pool_mixer(x, hist, pos0, g, w, bias, scale):
    b, t, d = x.shape
    tm = min(TOKEN_TILE, t)
    xs = _halo_specs(b, t, tm, d, POOL_HALO)
    hist_pad = jnp.pad(hist, ((0, 0), (POOL_HALO - POOL_HIST, 0), (0, 0)))
    per_b = lambda n: pl.BlockSpec((1, n, d), lambda bi, ti: (bi, 0, 0))
    ng = len(POOL_WINDOWS)
    return pl.pallas_call(
        functools.partial(_pool_kernel, has_halo=len(xs) == 2, pos0=pos0),
        grid=(b, t // tm),
        in_specs=xs + [per_b(POOL_HALO), _resident((1, d)), _resident((ng, POOL_GROUP, POOL_GROUP)),
                       _resident((1, d)), _resident((1, d))],
        out_specs=[pl.BlockSpec((1, tm, d), lambda bi, ti: (bi, ti, 0)), per_b(POOL_HIST)],
        out_shape=[jax.ShapeDtypeStruct((b, t, d), F32), jax.ShapeDtypeStruct((b, POOL_HIST, d), F32)],
        scratch_shapes=[pltpu.VMEM((POOL_HALO + tm, d), F32)],
        compiler_params=_cparams("parallel", "arbitrary"),
        name="pool_mixer",
    )(*([x] * len(xs)), hist_pad, g, w.astype(BF16), bias.reshape(1, -1), scale.reshape(1, -1))


def _ssd_kernel(*refs, has_halo, valid):
    if has_halo:
        x_ref, xh_ref, *refs = refs
    else:
        x_ref, *refs = refs
    (chist_ref, h0_ref, g_ref, wz_ref, wxbc_ref, wdt_ref, wconv_ref, bconv_ref, dtb_ref, a_ref, dskip_ref,
     gn_ref, wout_ref, exp_ref, expt_ref, o_ref, cs_out_ref, h_out_ref, ext, hstate) = refs
    L = x_ref.shape[1]
    i = pl.program_id(1)
    gw = SSD_HPG * SSD_HD
    gn = SSD_GROUPS * SSD_N

    @pl.when(i == 0)
    def _():
        hstate[...] = h0_ref[0]

    x = x_ref[0]
    xb = _rms(x, g_ref[...]).astype(BF16)
    z = _dot(xb, wz_ref[...])
    dtr = _dot(xb, wdt_ref[...])
    if has_halo:
        prev = jnp.where(i == 0, chist_ref[0], _dot(_rms(xh_ref[0], g_ref[...]).astype(BF16), wxbc_ref[...]))
    else:
        prev = chist_ref[0]
    ext[0:SSD_HALO] = prev
    ext[SSD_HALO:] = _dot(xb, wxbc_ref[...])
    off = SSD_HALO - (SSD_CONV_K - 1)
    y = jnp.broadcast_to(bconv_ref[...], (L, SSD_CONV_DIM))
    for k in range(SSD_CONV_K):
        y = y + wconv_ref[k:k + 1, :] * ext[off + k:off + k + L, :]
    cs_out_ref[0] = ext[valid + off:valid + SSD_HALO, :]
    xbc = _silu(y)
    xs = xbc[:, :SSD_INNER]
    bm = xbc[:, SSD_INNER:SSD_INNER + gn].astype(BF16)
    cm = xbc[:, SSD_INNER + gn:].astype(BF16)

    head_lane = lax.broadcasted_iota(jnp.int32, (L, LANES), 1) < SSD_HEADS
    live = head_lane
    if valid < L:
        live = live & (lax.broadcasted_iota(jnp.int32, (L, LANES), 0) < valid)
    dt = jnp.where(live, _softplus(dtr + dtb_ref[...]), 0.0)
    cs = _dot3_left(_tril(L), dt * a_ref[...])
    cst = cs.T
    dtt = dt.T
    to_end = jnp.exp(cs[L - 1:L, :] - cs) * dt
    td_x = _dot(to_end.astype(BF16), exp_ref[...])
    ecs_x = _dot3(jnp.exp(cs), exp_ref[...])
    chunk_decay = jnp.broadcast_to(jnp.exp(cst[:, L - 1:L]), (LANES, LANES))
    xs_b = xs.astype(BF16)
    xtd = xs * td_x
    r = lax.broadcasted_iota(jnp.int32, (L, L), 0)
    c = lax.broadcasted_iota(jnp.int32, (L, L), 1)
    causal = c <= r
    low_half = lax.broadcasted_iota(jnp.int32, (L, 2 * SSD_HD), 1) < SSD_HD

    y_groups = []
    for g in range(SSD_GROUPS):
        cg = cm[:, g * SSD_N:(g + 1) * SSD_N]
        bg = bm[:, g * SSD_N:(g + 1) * SSD_N]
        cb = _dot_nt(cg, bg)
        hg = hstate[g]
        y_off = _dot_nt(cg, hg.astype(BF16)) * ecs_x[:, g * gw:(g + 1) * gw]
        pairs = []
        for pr in range(SSD_HPG // 2):
            lo = g * gw + pr * 2 * SSD_HD
            xpair = xs_b[:, lo:lo + 2 * SSD_HD]
            acc = None
            for hh in range(2):
                e = g * SSD_HPG + 2 * pr + hh
                seg = cs[:, e:e + 1] - cst[e:e + 1, :]
                w = cb * jnp.exp(jnp.where(causal, seg, -jnp.inf)) * dtt[e:e + 1, :]
                xm = jnp.where(low_half == (hh == 0), xpair, jnp.zeros_like(xpair))
                term = _dot(w.astype(BF16), xm)
                acc = term if acc is None else acc + term
            pairs.append(acc)
        y_groups.append(jnp.concatenate(pairs, axis=1) + y_off)
        s_g = _dot(xtd[:, g * gw:(g + 1) * gw].T.astype(BF16), bg)
        hstate[g] = hg * _dot3_left(expt_ref[g], chunk_decay) + s_g

    ys = (jnp.concatenate(y_groups, axis=1) + dskip_ref[...] * xs) * _silu(z)
    normed = []
    for g in range(SSD_GROUPS):
        yg = ys[:, g * gw:(g + 1) * gw]
        normed.append(yg * lax.rsqrt(jnp.mean(yg * yg, axis=-1, keepdims=True) + EPS))
    yn = jnp.concatenate(normed, axis=1) * gn_ref[...]
    o_ref[0] = x + _dot(yn.astype(BF16), wout_ref[...])

    @pl.when(i == pl.num_programs(1) - 1)
    def _():
        h_out_ref[0] = hstate[...]


def _ssd_expand_constants():
    ch = np.arange(SSD_INNER)
    exp = np.zeros((LANES, SSD_INNER), np.float32)
    exp[ch // SSD_HD, ch] = 1.0
    gw = SSD_HPG * SSD_HD
    rows = np.arange(gw)
    expt = np.zeros((SSD_GROUPS, gw, LANES), np.float32)
    for g in range(SSD_GROUPS):
        expt[g, rows, g * SSD_HPG + rows // SSD_HD] = 1.0
    return jnp.asarray(exp, BF16), jnp.asarray(expt, BF16)


def ssd_mixer(x, conv_hist, h0, g, w_in, w_conv, b_conv, dt_bias, a_log, d_skip, g_norm, w_out):
    b, t, d = x.shape
    valid = t
    if t < LANES:
        x = jnp.pad(x, ((0, 0), (0, LANES - t), (0, 0)))
    tp = x.shape[1]
    L = min(SSD_CHUNK, tp)
    valid = L if tp > L else valid
    xs = _halo_specs(b, tp, L, d, SSD_HALO)
    gw = SSD_HPG * SSD_HD
    hist_pad = jnp.pad(conv_hist, ((0, 0), (SSD_HALO - (SSD_CONV_K - 1), 0), (0, 0)))
    wz = w_in[:, :SSD_INNER].astype(BF16)
    wxbc = w_in[:, SSD_INNER:SSD_INNER + SSD_CONV_DIM].astype(BF16)
    pad_heads = lambda v: jnp.zeros((v.shape[0], LANES), F32).at[:, :SSD_HEADS].set(v)
    wdt = pad_heads(w_in[:, SSD_INNER + SSD_CONV_DIM:]).astype(BF16)
    dtb = pad_heads(dt_bias.reshape(1, -1))
    a = pad_heads(-jnp.exp(a_log.astype(F32)).reshape(1, -1))
    dskip = jnp.repeat(d_skip, SSD_HD).reshape(1, -1)
    exp, expt = _ssd_expand_constants()
    per_b3 = lambda n, w: pl.BlockSpec((1, n, w), lambda bi, ti: (bi, 0, 0))
    state = pl.BlockSpec((1, SSD_GROUPS, gw, SSD_N), lambda bi, ti: (bi, 0, 0, 0))
    out, conv_new, h_new = pl.pallas_call(
        functools.partial(_ssd_kernel, has_halo=len(xs) == 2, valid=valid),
        grid=(b, tp // L),
        in_specs=xs + [per_b3(SSD_HALO, SSD_CONV_DIM), state, _resident((1, d)),
                       _resident((d, SSD_INNER)), _resident((d, SSD_CONV_DIM)), _resident((d, LANES)),
                       _resident((SSD_CONV_K, SSD_CONV_DIM)), _resident((1, SSD_CONV_DIM)),
                       _resident((1, LANES)), _resident((1, LANES)), _resident((1, SSD_INNER)),
                       _resident((1, SSD_INNER)), _resident((SSD_INNER, d)),
                       _resident((LANES, SSD_INNER)), _resident((SSD_GROUPS, gw, LANES))],
        out_specs=[pl.BlockSpec((1, L, d), lambda bi, ti: (bi, ti, 0)),
                   per_b3(SSD_CONV_K - 1, SSD_CONV_DIM), state],
        out_shape=[jax.ShapeDtypeStruct((b, tp, d), F32),
                   jax.ShapeDtypeStruct((b, SSD_CONV_K - 1, SSD_CONV_DIM), F32),
                   jax.ShapeDtypeStruct((b, SSD_GROUPS, gw, SSD_N), F32)],
        scratch_shapes=[pltpu.VMEM((SSD_HALO + L, SSD_CONV_DIM), F32), pltpu.VMEM((SSD_GROUPS, gw, SSD_N), F32)],
        compiler_params=_cparams("parallel", "arbitrary"),
        name="ssd_mixer",
    )(*([x] * len(xs)), hist_pad, h0.reshape(b, SSD_GROUPS, gw, SSD_N), g, wz, wxbc, wdt, w_conv,
      b_conv.reshape(1, -1), dtb, a, dskip, g_norm.reshape(1, -1), w_out.astype(BF16), exp, expt)
    return out[:, :t], conv_new, h_new.reshape(b, SSD_HEADS, SSD_HD, SSD_N)


def kernel(x_prompt, x_sample, cache_fox_k, cache_fox_v, cache_fox_logf, state_conf_conv, state_ssd_conv, state_ssd, state_pool, p_prompt, p_sample, g_mix, g_ffn, g_ple, g_final, fox_w_q, fox_w_k, fox_w_v, fox_w_f, fox_b_f, fox_w_o, conf_w_pw1, conf_b_pw1, conf_w_dw, conf_b_dw, conf_g_ln, conf_b_ln, conf_w_pw2, conf_b_pw2, ssd_w_in, ssd_w_conv, ssd_b_conv, ssd_dt_bias, ssd_a_log, ssd_d, ssd_g_norm, ssd_w_out, pool_w, pool_b, pool_scale, ffn_w_gate, ffn_w_up, ffn_w_down, moe_w_router, moe_w_gate, moe_w_up, moe_w_down, ple_w_up, ple_w_gate):
    b, t, d = x_prompt.shape
    bs, s, _ = x_sample.shape
    past = cache_fox_k.shape[1]
    row = lambda v: v.reshape(1, -1)
    bf = lambda w: w.astype(BF16)
    flat = lambda a: a.reshape(-1, a.shape[-1])
    hp, hs = x_prompt, x_sample

    def dense_ffn(h, p, i):
        j = i // 2
        out = ffn_ple(flat(h), flat(p[i]), row(g_ffn[i]), bf(ffn_w_gate[j]), bf(ffn_w_up[j]), bf(ffn_w_down[j]),
                      row(g_ple[i]), bf(ple_w_gate[i]), bf(ple_w_up[i]))
        return out.reshape(h.shape)

    def moe_ffn(h, p, i, final_norm):
        j = i // 2
        wr = jnp.zeros((d, LANES), F32).at[:, :N_EXPERTS].set(moe_w_router[j]).astype(BF16)
        out = moe_ple(flat(h), flat(p[i]), row(g_ffn[i]), wr, bf(moe_w_gate[j]), bf(moe_w_up[j]),
                      bf(moe_w_down[j]), row(g_ple[i]), bf(ple_w_gate[i]), bf(ple_w_up[i]), row(g_final),
                      final_norm)
        return out.reshape(h.shape)

    fox_k_p, fox_v_p, fox_lf_p, qt, vt, kaug, ft, qn2, kn2 = fox_proj(hp, row(g_mix[0]), fox_w_q, fox_w_k,
                                                                      fox_w_v, fox_w_f, fox_b_f)
    ub = fox_dead_block_bounds(qn2, kn2, ft, vt.shape[3])
    hp = out_proj_t(hp, fox_attention(qt, kaug, vt, ft, ub), fox_w_o)
    hs, fox_k_s, fox_v_s, fox_lf_s = fox_sample(hs, row(g_mix[0]), fox_w_q, fox_w_k, fox_w_v, fox_w_f, fox_b_f,
                                                fox_w_o, cache_fox_k, cache_fox_v, cache_fox_logf)
    hp = dense_ffn(hp, p_prompt, 0)
    hs = dense_ffn(hs, p_sample, 0)

    conf_args = (row(g_mix[1]), conf_w_pw1, conf_b_pw1, conf_w_dw, conf_b_dw, conf_g_ln, conf_b_ln,
                 conf_w_pw2, conf_b_pw2)
    hp, conf_p = conformer(hp, jnp.zeros((b, CONF_K - 1, d), F32), *conf_args)
    hs, conf_s = conformer(hs, state_conf_conv, *conf_args)
    hp = moe_ffn(hp, p_prompt, 1, False)
    hs = moe_ffn(hs, p_sample, 1, False)

    ssd_args = (row(g_mix[2]), ssd_w_in, ssd_w_conv, ssd_b_conv, ssd_dt_bias, ssd_a_log, ssd_d, ssd_g_norm,
                ssd_w_out)
    hp, ssdc_p, ssdh_p = ssd_mixer(hp, jnp.zeros((b, SSD_CONV_K - 1, SSD_CONV_DIM), F32),
                                   jnp.zeros((b, SSD_HEADS, SSD_HD, SSD_N), F32), *ssd_args)
    hs, ssdc_s, ssdh_s = ssd_mixer(hs, state_ssd_conv, state_ssd, *ssd_args)
    hp = dense_ffn(hp, p_prompt, 2)
    hs = dense_ffn(hs, p_sample, 2)

    pool_args = (row(g_mix[3]), pool_w, pool_b, pool_scale)
    hp, pool_p = pool_mixer(hp, jnp.zeros((b, POOL_HIST, d), F32), 0, *pool_args)
    hs, pool_s = pool_mixer(hs, state_pool, past, *pool_args)
    y_prompt = moe_ffn(hp, p_prompt, 3, True)
    y_sample = moe_ffn(hs, p_sample, 3, True)

    heads = lambda a: a.reshape(a.shape[0], a.shape[1], FOX_HEADS, FOX_HD)
    return (y_prompt, y_sample, heads(fox_k_p), heads(fox_v_p), fox_lf_p, heads(fox_k_s), heads(fox_v_s), fox_lf_s,
            conf_p, conf_s, ssdc_p, ssdc_s, ssdh_p, ssdh_s, pool_p, pool_s)
```

```python
import functools

import jax
import jax.numpy as jnp
import numpy as np
from jax import lax
from jax.experimental import pallas as pl
from jax.experimental.pallas import tpu as pltpu

F32 = jnp.float32
BF16 = jnp.bfloat16

D_MODEL = 1024
EPS = 1e-6
PLE_DIM = 256
FOX_HEADS = 16
FOX_HD = 64
FOX_PAIRS = FOX_HEADS // 2
FOX_VROWS = FOX_HD + 16
CONF_K = 31
CONF_HALO = 32
SSD_INNER = 2048
SSD_HD = 64
SSD_HEADS = 32
SSD_GROUPS = 4
SSD_HPG = SSD_HEADS // SSD_GROUPS
SSD_N = 128
SSD_CONV_K = 4
SSD_CONV_DIM = SSD_INNER + 2 * SSD_GROUPS * SSD_N
SSD_HALO = 8
POOL_WINDOWS = (2, 4, 8, 16)
POOL_GROUP = D_MODEL // len(POOL_WINDOWS)
POOL_HIST = 15
POOL_HALO = 16
N_EXPERTS = 8
D_FF = 2816
D_FF_EXPERT = 1408
LANES = 128

VMEM_LIMIT_BYTES = 56 * 1024 * 1024
TOKEN_TILE = 512
ATTN_BLOCK = 512
ATTN_STRIP = 256
ATTN_KEYS = 512
NORM_SLACK = 1.01
DEAD_LOGIT_GAP = 106.0
SSD_CHUNK = 256
FFN_CHUNK = 256
MOE_TILE = 1024
MOE_SUBTILE = 512
MOE_CHUNK = 160
MOE_SCATTER = 256


def _cparams(*sem):
    return pltpu.CompilerParams(dimension_semantics=sem, vmem_limit_bytes=VMEM_LIMIT_BYTES)


def _resident(shape):
    n = len(shape)
    return pl.BlockSpec(shape, lambda *_: (0,) * n, pipeline_mode=pl.Buffered(1))


def _dot(a, b):
    return jnp.dot(a, b, preferred_element_type=F32)


def _dot_nt(a, b):
    return lax.dot_general(a, b, (((1,), (1,)), ((), ())), preferred_element_type=F32)


def _rms(x, g):
    return x * lax.rsqrt(jnp.mean(x * x, axis=-1, keepdims=True) + EPS) * g


def _sigmoid(x):
    return 1.0 / (1.0 + jnp.exp(-x))


def _silu(x):
    return x * _sigmoid(x)


def _softplus(x):
    return jnp.maximum(x, 0.0) + jnp.log1p(jnp.exp(-jnp.abs(x)))


def _split3(x):
    p1 = x.astype(BF16)
    r1 = x - p1.astype(F32)
    p2 = r1.astype(BF16)
    r2 = r1 - p2.astype(F32)
    return p1, p2, r2.astype(BF16)


def _dot3(a_f32, b_bf16):
    p1, p2, p3 = _split3(a_f32)
    return (_dot(p1, b_bf16) + _dot(p2, b_bf16)) + _dot(p3, b_bf16)


def _dot3_left(a_bf16, b_f32):
    p1, p2, p3 = _split3(b_f32)
    return (_dot(a_bf16, p1) + _dot(a_bf16, p2)) + _dot(a_bf16, p3)


def _rep_rows(a, n):
    s, w = a.shape
    return jnp.broadcast_to(a[:, None, :], (s, n, w)).reshape(s * n, w)


def _tile_rows(a, n):
    h, w = a.shape
    return jnp.broadcast_to(a[None], (n, h, w)).reshape(n * h, w)


def _tril(n):
    r = lax.broadcasted_iota(jnp.int32, (n, n), 0)
    c = lax.broadcasted_iota(jnp.int32, (n, n), 1)
    return jnp.where(c <= r, 1.0, 0.0).astype(BF16)


def _ple_apply(h1, p, g_ple, w_gate, w_up):
    hn = _rms(h1, g_ple).astype(BF16)
    gate = _sigmoid(_dot(hn, w_gate))
    return h1 + gate * _dot(p.astype(BF16), w_up)


def _ffn_ple_apply(x, p, gf_ref, wg_ref, wu_ref, wd_ref, gp_ref, wpg_ref, wpu_ref):
    xn = _rms(x, gf_ref[...]).astype(BF16)
    acc = x
    for c in range(D_FF // FFN_CHUNK):
        sl = slice(c * FFN_CHUNK, (c + 1) * FFN_CHUNK)
        hh = (_silu(_dot(xn, wg_ref[:, sl])) * _dot(xn, wu_ref[:, sl])).astype(BF16)
        acc = acc + _dot(hh, wd_ref[sl, :])
    return _ple_apply(acc, p, gp_ref[...], wpg_ref[...], wpu_ref[...])


def _ffn_ple_kernel(h_ref, p_ref, *refs):
    *w_refs, o_ref = refs
    o_ref[...] = _ffn_ple_apply(h_ref[...], p_ref[...], *w_refs)


def _proj_ffn_ple_kernel(h_ref, ot_ref, wo_ref, p_ref, *refs):
    *w_refs, o_ref = refs
    x = h_ref[0] + pl.dot(ot_ref[0], wo_ref[...], trans_a=True)
    o_ref[0] = _ffn_ple_apply(x, p_ref[0], *w_refs)


def proj_ffn_ple(h, ot, wo, p, g_ffn, wg, wu, wd, g_ple, wpg, wpu):
    b, t, d = h.shape
    tm = min(TOKEN_TILE, t)
    tile = lambda w: pl.BlockSpec((1, tm, w), lambda bi, ti: (bi, ti, 0))
    return pl.pallas_call(
        _proj_ffn_ple_kernel,
        grid=(b, t // tm),
        in_specs=[tile(d), pl.BlockSpec((1, d, tm), lambda bi, ti: (bi, 0, ti)), _resident((d, d)), tile(PLE_DIM),
                  _resident((1, d)), _resident((d, D_FF)), _resident((d, D_FF)), _resident((D_FF, d)),
                  _resident((1, d)), _resident((d, d)), _resident((PLE_DIM, d))],
        out_specs=tile(d),
        out_shape=jax.ShapeDtypeStruct((b, t, d), F32),
        compiler_params=_cparams("parallel", "parallel"),
        name="proj_ffn_ple",
    )(h, ot, wo.astype(BF16), p, g_ffn, wg, wu, wd, g_ple, wpg, wpu)


def ffn_ple(h, p, g_ffn, wg, wu, wd, g_ple, wpg, wpu):
    m = h.shape[0]
    tm = min(TOKEN_TILE, m)
    row = lambda w: pl.BlockSpec((tm, w), lambda i: (i, 0))
    return pl.pallas_call(
        _ffn_ple_kernel,
        grid=(m // tm,),
        in_specs=[row(D_MODEL), row(PLE_DIM), _resident((1, D_MODEL)),
                  _resident((D_MODEL, D_FF)), _resident((D_MODEL, D_FF)), _resident((D_FF, D_MODEL)),
                  _resident((1, D_MODEL)), _resident((D_MODEL, D_MODEL)), _resident((PLE_DIM, D_MODEL))],
        out_specs=row(D_MODEL),
        out_shape=jax.ShapeDtypeStruct((m, D_MODEL), F32),
        compiler_params=_cparams("parallel"),
        name="ffn_ple",
    )(h, p, g_ffn, wg, wu, wd, g_ple, wpg, wpu)


def _moe_ple_kernel(h_ref, p_ref, gf_ref, wr_ref, wg_ref, wu_ref, wd_ref, gp_ref, wpg_ref, wpu_ref, gfin_ref,
                    upper_ref, lower_ref, o_ref, xn_s, comb_s, combt_s, acc_s, *, final_norm):
    e = pl.program_id(1)

    @pl.when(e == 0)
    def _():
        x = h_ref[...]
        xn = _rms(x, gf_ref[...]).astype(BF16)
        xn_s[...] = xn
        acc_s[...] = x
        logits = _dot(xn, wr_ref[...])
        lane = lax.broadcasted_iota(jnp.int32, logits.shape, 1).astype(F32)
        neg = jnp.float32(-jnp.inf)
        lg = jnp.where(lane < N_EXPERTS, logits, neg)
        m1 = jnp.max(lg, axis=-1, keepdims=True)
        i1 = jnp.min(jnp.where(lg == m1, lane, float(LANES)), axis=-1, keepdims=True)
        lg2 = jnp.where(lane == i1, neg, lg)
        m2 = jnp.max(lg2, axis=-1, keepdims=True)
        i2 = jnp.min(jnp.where(lg2 == m2, lane, float(LANES)), axis=-1, keepdims=True)
        e2 = jnp.exp(m2 - m1)
        den = 1.0 + e2
        comb = jnp.where(lane == i1, 1.0 / den, jnp.where(lane == i2, e2 / den, 0.0))
        comb_s[...] = comb
        combt_s[...] = comb.T[:N_EXPERTS, :]

    ts = upper_ref.shape[0]
    for sb in range(xn_s.shape[0] // ts):
        rows = slice(sb * ts, (sb + 1) * ts)
        lane = lax.broadcasted_iota(jnp.int32, (ts, LANES), 1)
        gate_col = jnp.sum(jnp.where(lane == e, comb_s[rows, :], 0.0), axis=-1, keepdims=True)
        gate_row = combt_s[pl.ds(e, 1), rows]
        sel_row = jnp.where(gate_row > 0.0, 1.0, 0.0)
        sel_col = jnp.where(gate_col > 0.0, 1.0, 0.0)
        rank_row = _dot(jnp.broadcast_to(sel_row, (8, ts)).astype(BF16), upper_ref[...])[0:1, :]
        rank_col = _dot(lower_ref[...], jnp.broadcast_to(sel_col, (ts, MOE_SCATTER)).astype(BF16))
        n_chunks = (jnp.max(rank_row).astype(jnp.int32) + (MOE_CHUNK - 1)) // MOE_CHUNK
        slot_row = lax.broadcasted_iota(jnp.int32, (MOE_CHUNK, ts), 0).astype(F32) + 1.0
        slot_lane = lax.broadcasted_iota(jnp.int32, (ts, MOE_SCATTER), 1)
        slot_col = slot_lane.astype(F32) + 1.0
        routed_col = (sel_col > 0.0) & (slot_lane < MOE_CHUNK)

        def chunk(ci, carry, rows=rows, gate_col=gate_col, sel_row=sel_row, rank_row=rank_row, rank_col=rank_col,
                  slot_row=slot_row, slot_col=slot_col, routed_col=routed_col):
            base = (ci * MOE_CHUNK).astype(F32)
            gather = jnp.where((sel_row > 0.0) & (rank_row == slot_row + base), 1.0, 0.0).astype(BF16)
            xg = _dot(gather, xn_s[rows, :]).astype(BF16)
            hh = (_silu(_dot(xg, wg_ref[0])) * _dot(xg, wu_ref[0])).astype(BF16)
            y = _dot(hh, wd_ref[0]).astype(BF16)
            y = jnp.concatenate([y, jnp.zeros((MOE_SCATTER - MOE_CHUNK, D_MODEL), BF16)], axis=0)
            scatter = jnp.where(routed_col & (rank_col == slot_col + base), 1.0, 0.0).astype(BF16)
            acc_s[rows, :] += gate_col * _dot(scatter, y)
            return carry

        lax.fori_loop(0, n_chunks, chunk, 0)

    @pl.when(e == N_EXPERTS - 1)
    def _():
        h2 = _ple_apply(acc_s[...], p_ref[...], gp_ref[...], wpg_ref[...], wpu_ref[...])
        if final_norm:
            h2 = _rms(h2, gfin_ref[...])
        o_ref[...] = h2


def moe_ple(h, p, g_ffn, wr, wg, wu, wd, g_ple, wpg, wpu, g_final, final_norm):
    m = h.shape[0]
    tm = min(MOE_TILE, m)
    row = lambda w: pl.BlockSpec((tm, w), lambda i, e: (i, 0))
    expert = lambda a, b: pl.BlockSpec((1, a, b), lambda i, e: (e, 0, 0))
    ts = min(MOE_SUBTILE, tm)
    upper = jnp.asarray(np.triu(np.ones((ts, ts), np.float32)), BF16)
    lower = jnp.asarray(np.tril(np.ones((ts, ts), np.float32)), BF16)
    return pl.pallas_call(
        functools.partial(_moe_ple_kernel, final_norm=final_norm),
        grid=(m // tm, N_EXPERTS),
        in_specs=[row(D_MODEL), row(PLE_DIM), _resident((1, D_MODEL)), _resident((D_MODEL, LANES)),
                  expert(D_MODEL, D_FF_EXPERT), expert(D_MODEL, D_FF_EXPERT), expert(D_FF_EXPERT, D_MODEL),
                  _resident((1, D_MODEL)), _resident((D_MODEL, D_MODEL)), _resident((PLE_DIM, D_MODEL)),
                  _resident((1, D_MODEL)), _resident((ts, ts)), _resident((ts, ts))],
        out_specs=row(D_MODEL),
        out_shape=jax.ShapeDtypeStruct((m, D_MODEL), F32),
        scratch_shapes=[pltpu.VMEM((tm, D_MODEL), BF16), pltpu.VMEM((tm, LANES), F32),
                        pltpu.VMEM((N_EXPERTS, tm), F32), pltpu.VMEM((tm, D_MODEL), F32)],
        compiler_params=_cparams("parallel", "arbitrary"),
        name="moe_ple",
    )(h, p, g_ffn, wr, wg, wu, wd, g_ple, wpg, wpu, g_final, upper, lower)


def _log_sigmoid(z):
    return jnp.minimum(z, 0.0) - jnp.log1p(jnp.exp(-jnp.abs(z)))


def _fox_proj_kernel(x_ref, g_ref, wqt_ref, wk_ref, wv_ref, wf_ref, bf_ref, sel_ref, ones_ref, hsel_ref,
                     k_out, v_out, lf_out, qt_out, vt_out, kaug_out, ft_out, qn_out, kn_out, carry):
    tm = x_ref.shape[1]

    @pl.when(pl.program_id(1) == 0)
    def _():
        carry[...] = jnp.zeros_like(carry)

    xn = _rms(x_ref[0], g_ref[...])
    xb = xn.astype(BF16)
    xnt = xn.T.astype(BF16)
    k = _dot(xb, wk_ref[...])
    v = _dot(xb, wv_ref[...])
    k_out[0] = k
    v_out[0] = v
    qtb = (_dot(wqt_ref[...], xnt) * (FOX_HD ** -0.5)).astype(BF16)
    qt_out[0] = qtb
    kb = k.astype(BF16)
    qsq = qtb.astype(F32)
    ksq = kb.astype(F32)
    qn2 = _dot(hsel_ref[...], (qsq * qsq).astype(BF16))
    kn2 = _dot_nt((ksq * ksq).astype(BF16), hsel_ref[...])
    qn_out[0, 0] = jnp.broadcast_to(jnp.max(qn2, axis=1, keepdims=True), (LANES, LANES))[:FOX_HEADS]
    kn_out[0, 0] = jnp.broadcast_to(jnp.max(kn2, axis=0, keepdims=True), (8, LANES))
    vt = v.T.astype(BF16)
    ones_rows = jnp.ones((FOX_VROWS - FOX_HD, tm), BF16)
    for h in range(FOX_HEADS):
        vt_out[0, 0, FOX_VROWS * h:FOX_VROWS * h + FOX_HD, :] = vt[FOX_HD * h:FOX_HD * (h + 1), :]
        vt_out[0, 0, FOX_VROWS * h + FOX_HD:FOX_VROWS * (h + 1), :] = ones_rows
    lf = _log_sigmoid(_dot(xb, wf_ref[...]) + bf_ref[...])
    lf_out[0] = lf[:, :FOX_HEADS]
    f = _dot3_left(_tril(tm), lf) + carry[...]
    carry[...] = f[tm - 1:tm, :]
    ft_out[0] = f.T[:FOX_HEADS, :]
    p1, p2, p3 = _split3(f)
    aug = ((_dot(p1, sel_ref[0]) + _dot(p2, sel_ref[1])) + _dot(p3, sel_ref[2]) + ones_ref[...]).astype(BF16)
    for p in range(FOX_PAIRS):
        kaug_out[0, :, 2 * LANES * p:2 * LANES * p + LANES] = kb[:, LANES * p:LANES * (p + 1)]
        kaug_out[0, :, 2 * LANES * p + LANES:2 * LANES * (p + 1)] = aug[:, LANES * p:LANES * (p + 1)]


def _fox_aug_constants():
    h = np.arange(FOX_HEADS)
    sel = np.zeros((3, LANES, FOX_PAIRS * LANES), np.float32)
    for r in range(3):
        sel[r, h, LANES * (h // 2) + 16 * (h % 2) + r] = 1.0
    p = np.arange(FOX_PAIRS)
    ones = np.zeros((1, FOX_PAIRS * LANES), np.float32)
    for r in range(3):
        ones[0, LANES * p + 32 + r] = 1.0
    ch = np.arange(D_MODEL)
    hsel = np.zeros((LANES, D_MODEL), np.float32)
    hsel[ch // FOX_HD, ch] = 1.0
    return jnp.asarray(sel, BF16), jnp.asarray(ones), jnp.asarray(hsel, BF16)


def fox_proj(x, g, wq, wk, wv, wf, bf):
    b, t, d = x.shape
    tm = min(ATTN_BLOCK, t)
    sel, ones, hsel = _fox_aug_constants()
    wf_pad = jnp.zeros((d, LANES), F32).at[:, :FOX_HEADS].set(wf).astype(BF16)
    bf_pad = jnp.zeros((1, LANES), F32).at[0, :FOX_HEADS].set(bf)
    tile = lambda w: pl.BlockSpec((1, tm, w), lambda bi, ti: (bi, ti, 0))
    return pl.pallas_call(
        _fox_proj_kernel,
        grid=(b, t // tm),
        in_specs=[tile(d), _resident((1, d)), _resident((d, d)), _resident((d, d)), _resident((d, d)),
                  _resident((d, LANES)), _resident((1, LANES)), _resident((3, LANES, FOX_PAIRS * LANES)),
                  _resident((1, FOX_PAIRS * LANES)), _resident((LANES, d))],
        out_specs=[tile(d), tile(d), tile(FOX_HEADS),
                   pl.BlockSpec((1, d, tm), lambda bi, ti: (bi, 0, ti)),
                   pl.BlockSpec((1, 1, FOX_HEADS * FOX_VROWS, tm), lambda bi, ti: (bi, ti, 0, 0)),
                   tile(2 * d),
                   pl.BlockSpec((1, FOX_HEADS, tm), lambda bi, ti: (bi, 0, ti)),
                   pl.BlockSpec((1, 1, FOX_HEADS, LANES), lambda bi, ti: (bi, ti, 0, 0)),
                   pl.BlockSpec((1, 1, 8, LANES), lambda bi, ti: (bi, ti, 0, 0))],
        out_shape=[jax.ShapeDtypeStruct((b, t, d), F32), jax.ShapeDtypeStruct((b, t, d), F32),
                   jax.ShapeDtypeStruct((b, t, FOX_HEADS), F32),
                   jax.ShapeDtypeStruct((b, d, t), BF16),
                   jax.ShapeDtypeStruct((b, t // tm, FOX_HEADS * FOX_VROWS, tm), BF16),
                   jax.ShapeDtypeStruct((b, t, 2 * d), BF16),
                   jax.ShapeDtypeStruct((b, FOX_HEADS, t), F32),
                   jax.ShapeDtypeStruct((b, t // tm, FOX_HEADS, LANES), F32),
                   jax.ShapeDtypeStruct((b, t // tm, 8, LANES), F32)],
        scratch_shapes=[pltpu.VMEM((1, LANES), F32)],
        compiler_params=_cparams("arbitrary", "arbitrary"),
        name="fox_proj",
    )(x, g, wq.T.astype(BF16), wk.astype(BF16), wv.astype(BF16), wf_pad, bf_pad, sel, ones, hsel)


def fox_dead_block_bounds(qn2, kn2, ft, blk):
    b, nb = qn2.shape[:2]
    qn = jnp.sqrt(qn2[..., 0] * NORM_SLACK).transpose(0, 2, 1)
    kn = jnp.sqrt(kn2[:, :, 0, :FOX_HEADS] * NORM_SLACK).transpose(0, 2, 1)
    kn_upto = lax.cummax(kn, axis=2)
    fq_first = ft[:, :, ::blk]
    fk_last = ft[:, :, blk - 1::blk]
    ub = qn[:, :, :, None] * kn_upto[:, :, None, :] + fq_first[:, :, :, None] - fk_last[:, :, None, :]
    ub = jnp.pad(ub, ((0, 0), (0, 0), (0, 0), (0, LANES - nb)), constant_values=-jnp.inf)
    return ub.reshape(b, FOX_PAIRS, 2, nb, LANES).transpose(0, 1, 3, 2, 4)


def _fox_attn_kernel(qt_ref, kaug_ref, vt_ref, ft_ref, ub_ref, ot_ref, qaug, m_s, acc, s_buf):
    tq = qt_ref.shape[2]
    tk = vt_ref.shape[3]
    i = pl.program_id(2)
    zeros = lambda r: jnp.zeros((r, tq), BF16)
    minus = jnp.full((16, tq), -1.0, BF16)
    row = lax.broadcasted_iota(jnp.int32, (16, tq), 0)
    for h in range(2):
        qaug[h, 64 * h:64 * (h + 1)] = qt_ref[0, 64 * h:64 * (h + 1), :]
        qaug[h, 64 * (1 - h):64 * (2 - h)] = zeros(64)
        qaug[h, LANES + 16 * h:LANES + 16 * (h + 1)] = minus
        qaug[h, LANES + 16 * (1 - h):LANES + 16 * (2 - h)] = zeros(16)
        p1, p2, p3 = _split3(ft_ref[0, 0, h:h + 1, :])
        blk = jnp.where(row == 0, p1.astype(F32),
                        jnp.where(row == 1, p2.astype(F32), jnp.where(row == 2, p3.astype(F32), 0.0)))
        qaug[h, LANES + 32:LANES + 48] = blk.astype(BF16)
        qaug[h, LANES + 48:2 * LANES] = zeros(LANES - 48)
    m_s[...] = jnp.full(m_s.shape, -jnp.inf, F32)
    acc[...] = jnp.zeros(acc.shape, F32)

    chains = [(h, c) for h in range(2) for c in range(tq // ATTN_STRIP)]

    def qk(n, j, slot):
        h, c = chains[n]
        qa = qaug[h, :, c * ATTN_STRIP:(c + 1) * ATTN_STRIP]
        for r in range(tk // ATTN_KEYS):
            kj = kaug_ref[0, pl.ds(pl.multiple_of(j * tk + r * ATTN_KEYS, ATTN_KEYS), ATTN_KEYS), :]
            s_buf[slot, n, r * ATTN_KEYS:(r + 1) * ATTN_KEYS, :] = _dot(kj, qa)

    def softmax(n, slot, masked):
        h, c = chains[n]
        qs = slice(c * ATTN_STRIP, (c + 1) * ATTN_STRIP)
        st = s_buf[slot, n]
        if masked:
            kpos = lax.broadcasted_iota(jnp.int32, st.shape, 0)
            qpos = lax.broadcasted_iota(jnp.int32, st.shape, 1) + c * ATTN_STRIP
            st = jnp.where(kpos <= qpos, st, -jnp.inf)
        m_prev = m_s[h, :, qs]
        m_new = jnp.maximum(m_prev, jnp.max(st, axis=0, keepdims=True))
        alpha = jnp.exp(m_prev - m_new)
        p = jnp.exp(st - m_new)
        m_s[h, :, qs] = m_new
        return p.astype(BF16), alpha

    def pv(n, j, p, alpha):
        h, c = chains[n]
        qs = slice(c * ATTN_STRIP, (c + 1) * ATTN_STRIP)
        acc[h, :, qs] = alpha * acc[h, :, qs] + _dot(vt_ref[0, j, FOX_VROWS * h:FOX_VROWS * (h + 1), :], p)

    def block(j, slot, masked, prefetch=True):
        jn = jnp.maximum(j - 1, 0)
        pa = {}
        if prefetch:
            for n in range(len(chains)):
                qk(n, jn, 1 - slot)
        for n in range(len(chains)):
            pa[n] = softmax(n, slot, masked)
            if n >= 1:
                pv(n - 1, j, *pa.pop(n - 1))
        pv(len(chains) - 1, j, *pa.pop(len(chains) - 1))

    for n in range(len(chains)):
        qk(n, i, 0)
    block(i, 0, True)

    lane = lax.broadcasted_iota(jnp.int32, (1, LANES), 1)
    live = lane < 0
    for h in range(2):
        m_low = jnp.min(m_s[h], axis=1, keepdims=True)
        live = live | (ub_ref[0, 0, 0, h:h + 1, :] >= m_low - DEAD_LOGIT_GAP)
    n_live = jnp.sum(jnp.where(live & (lane < i), 1.0, 0.0)).astype(jnp.int32)

    def pair(u, carry):
        j = i - 1 - 2 * u
        block(j, 1, False)
        block(j - 1, 0, False)
        return carry

    n_pairs = n_live // 2
    even = n_live % 2 == 0
    lax.fori_loop(0, jnp.where(even, jnp.maximum(n_pairs - 1, 0), n_pairs), pair, 0)

    @pl.when(even & (n_live >= 2))
    def _():
        block(i - n_live + 1, 1, False)
        block(i - n_live, 0, False, prefetch=False)

    @pl.when(n_live % 2 == 1)
    def _():
        block(i - n_live, 1, False, prefetch=False)

    for h in range(2):
        ot_ref[0, 64 * h:64 * (h + 1), :] = (acc[h, :FOX_HD] / acc[h, FOX_HD:FOX_HD + 1]).astype(BF16)


def fox_attention(qt, kaug, vt, ft, ub):
    b, d, t = qt.shape
    blk = vt.shape[3]
    assert t // blk <= LANES
    ft = ft.reshape(b, FOX_PAIRS, 2, t)
    return pl.pallas_call(
        _fox_attn_kernel,
        grid=(b, FOX_PAIRS, t // blk),
        in_specs=[pl.BlockSpec((1, LANES, blk), lambda bi, p, i: (bi, p, i)),
                  pl.BlockSpec((1, t, 2 * LANES), lambda bi, p, i: (bi, 0, p)),
                  pl.BlockSpec((1, t // blk, 2 * FOX_VROWS, blk), lambda bi, p, i: (bi, 0, p, 0)),
                  pl.BlockSpec((1, 1, 2, blk), lambda bi, p, i: (bi, p, 0, i)),
                  pl.BlockSpec((1, 1, 1, 2, LANES), lambda bi, p, i: (bi, p, i, 0, 0))],
        out_specs=pl.BlockSpec((1, LANES, blk), lambda bi, p, i: (bi, p, i)),
        out_shape=jax.ShapeDtypeStruct((b, d, t), BF16),
        scratch_shapes=[pltpu.VMEM((2, 2 * LANES, blk), BF16), pltpu.VMEM((2, 1, blk), F32),
                        pltpu.VMEM((2, FOX_VROWS, blk), F32),
                        pltpu.VMEM((2, 2 * blk // ATTN_STRIP, blk, ATTN_STRIP), F32)],
        compiler_params=_cparams("parallel", "parallel", "arbitrary"),
        name="fox_attention",
    )(qt, kaug, vt, ft, ub)


def _out_proj_t_kernel(h_ref, ot_ref, w_ref, o_ref):
    o_ref[0] = h_ref[0] + pl.dot(ot_ref[0], w_ref[...], trans_a=True)


def out_proj_t(h, ot, w):
    b, t, d = h.shape
    tm = min(TOKEN_TILE, t)
    tile = pl.BlockSpec((1, tm, d), lambda bi, ti: (bi, ti, 0))
    return pl.pallas_call(
        _out_proj_t_kernel,
        grid=(b, t // tm),
        in_specs=[tile, pl.BlockSpec((1, d, tm), lambda bi, ti: (bi, 0, ti)), _resident((d, d))],
        out_specs=tile,
        out_shape=jax.ShapeDtypeStruct((b, t, d), F32),
        compiler_params=_cparams("parallel", "parallel"),
        name="fox_out_proj",
    )(h, ot, w.astype(BF16))


def _fox_sample_kernel(x_ref, g_ref, wq_ref, wk_ref, wv_ref, wf_ref, bf_ref, wo_ref, ck_ref, cv_ref, clf_ref,
                       h_out, k_out, v_out, lf_out):
    s = x_ref.shape[1]
    pl_len = ck_ref.shape[1]
    x = x_ref[0]
    xb = _rms(x, g_ref[...]).astype(BF16)
    q = _dot(xb, wq_ref[...]) * (FOX_HD ** -0.5)
    k = _dot(xb, wk_ref[...])
    v = _dot(xb, wv_ref[...])
    lf = _log_sigmoid(_dot(xb, wf_ref[...]) + bf_ref[...])
    k_out[0] = k
    v_out[0] = v
    lf_out[0] = lf[:, :FOX_HEADS]

    rows = s * FOX_HEADS
    rh = lax.broadcasted_iota(jnp.int32, (rows, D_MODEL), 0) % FOX_HEADS
    lh = lax.broadcasted_iota(jnp.int32, (rows, D_MODEL), 1) // FOX_HD
    head_lanes = rh == lh
    qbd = jnp.where(head_lanes, _rep_rows(q, FOX_HEADS), 0.0).astype(BF16)

    cblk = 256
    carry = jnp.zeros((1, LANES), F32)
    tri = _tril(cblk)
    f_parts = []
    for c in range(pl_len // cblk):
        fc = _dot3_left(tri, clf_ref[0, c * cblk:(c + 1) * cblk, :]) + carry
        carry = fc[cblk - 1:cblk, :]
        f_parts.append(fc)
    f_cache = jnp.concatenate(f_parts, axis=0)
    pad_rows = lambda a: jnp.concatenate([a, jnp.zeros((LANES - s, a.shape[1]), a.dtype)], axis=0)
    f_new = _dot3_left(_tril(LANES), pad_rows(lf)) + carry

    lane = lax.broadcasted_iota(jnp.int32, (rows, LANES), 1)
    row_head = lax.broadcasted_iota(jnp.int32, (rows, LANES), 0) % FOX_HEADS
    fq = jnp.sum(jnp.where(lane == row_head, _rep_rows(f_new[:s], FOX_HEADS), 0.0), axis=-1, keepdims=True)
    fk_cache = _tile_rows(f_cache.T[:FOX_HEADS, :], s)
    fk_new = _tile_rows(f_new.T[:FOX_HEADS, :], s)

    kc = 512
    qk = [_dot_nt(qbd, ck_ref[0, c * kc:(c + 1) * kc, :].astype(BF16)) for c in range(pl_len // kc)]
    s_cache = jnp.concatenate(qk, axis=1) + fq - fk_cache
    s_new = _dot_nt(qbd, pad_rows(k).astype(BF16)) + fq - fk_new
    qi = lax.broadcasted_iota(jnp.int32, (rows, LANES), 0) // FOX_HEADS
    kj = lax.broadcasted_iota(jnp.int32, (rows, LANES), 1)
    s_new = jnp.where(kj <= qi, s_new, -jnp.inf)
    m = jnp.maximum(jnp.max(s_cache, axis=-1, keepdims=True), jnp.max(s_new, axis=-1, keepdims=True))
    p_cache = jnp.exp(s_cache - m)
    p_new = jnp.exp(s_new - m)
    den = jnp.sum(p_cache, axis=-1, keepdims=True) + jnp.sum(p_new, axis=-1, keepdims=True)
    pv = _dot(p_new.astype(BF16), pad_rows(v).astype(BF16))
    pb = p_cache.astype(BF16)
    for c in range(pl_len // kc):
        pv = pv + _dot(pb[:, c * kc:(c + 1) * kc], cv_ref[0, c * kc:(c + 1) * kc, :].astype(BF16))
    o_heads = jnp.where(head_lanes, pv / den, 0.0).astype(BF16)
    pick = (lax.broadcasted_iota(jnp.int32, (s, rows), 1) // FOX_HEADS
            == lax.broadcasted_iota(jnp.int32, (s, rows), 0))
    o = _dot(jnp.where(pick, 1.0, 0.0).astype(BF16), o_heads)
    h_out[0] = x + _dot(o.astype(BF16), wo_ref[...])


def fox_sample(x, g, wq, wk, wv, wf, bf, wo, ck, cv, clf):
    b, s, d = x.shape
    p = ck.shape[1]
    wf_pad = jnp.zeros((d, LANES), F32).at[:, :FOX_HEADS].set(wf).astype(BF16)
    bf_pad = jnp.zeros((1, LANES), F32).at[0, :FOX_HEADS].set(bf)
    clf_pad = jnp.zeros((b, p, LANES), F32).at[:, :, :FOX_HEADS].set(clf)
    seq = lambda n, w: pl.BlockSpec((1, n, w), lambda bi: (bi, 0, 0))
    return pl.pallas_call(
        _fox_sample_kernel,
        grid=(b,),
        in_specs=[seq(s, d), _resident((1, d)), _resident((d, d)), _resident((d, d)), _resident((d, d)),
                  _resident((d, LANES)), _resident((1, LANES)), _resident((d, d)),
                  seq(p, d), seq(p, d), seq(p, LANES)],
        out_specs=[seq(s, d), seq(s, d), seq(s, d), seq(s, FOX_HEADS)],
        out_shape=[jax.ShapeDtypeStruct((b, s, d), F32)] * 3 + [jax.ShapeDtypeStruct((b, s, FOX_HEADS), F32)],
        compiler_params=_cparams("parallel"),
        name="fox_sample",
    )(x, g, wq.astype(BF16), wk.astype(BF16), wv.astype(BF16), wf_pad, bf_pad, wo.astype(BF16),
      ck.reshape(b, p, d), cv.reshape(b, p, d), clf_pad)


def _halo_specs(b, t, tm, d, halo):
    main = pl.BlockSpec((1, tm, d), lambda bi, ti: (bi, ti, 0))
    if t == tm:
        return [main]
    per = tm // halo
    return [main, pl.BlockSpec((1, halo, d), lambda bi, ti: (bi, jnp.maximum(ti * per - 1, 0), 0))]


def _conf_kernel(*refs, has_halo):
    if has_halo:
        x_ref, xh_ref, *refs = refs
    else:
        x_ref, *refs = refs
    (hist_ref, g_ref, w1_ref, b1_ref, wdw_ref, bdw_ref, gln_ref, bln_ref, w2_ref, b2_ref,
     o_ref, hs_ref, uext, ushift) = refs
    tm = x_ref.shape[1]

    def glu(rows):
        ag = _dot(_rms(rows, g_ref[...]).astype(BF16), w1_ref[...]) + b1_ref[...]
        return ag[:, :D_MODEL] * _sigmoid(ag[:, D_MODEL:])

    x = x_ref[0]
    if has_halo:
        prev = jnp.where(pl.program_id(1) == 0, hist_ref[0], glu(xh_ref[0]))
    else:
        prev = hist_ref[0]
    uext[0:CONF_HALO] = prev
    uext[CONF_HALO:CONF_HALO + tm] = glu(x)
    uext[CONF_HALO + tm:] = jnp.zeros((8, D_MODEL), F32)
    off = CONF_HALO - (CONF_K - 1)
    y = jnp.broadcast_to(bdw_ref[...], (tm, D_MODEL))
    for s in range(8):
        ushift[...] = uext[s:s + tm + CONF_HALO, :]
        for k in range(CONF_K):
            if (off + k) % 8 == s:
                a = off + k - s
                y = y + wdw_ref[k:k + 1, :] * ushift[a:a + tm, :]
    hs_ref[0] = uext[tm + off:tm + CONF_HALO, :]
    mu = jnp.mean(y, axis=-1, keepdims=True)
    yc = y - mu
    var = jnp.mean(yc * yc, axis=-1, keepdims=True)
    v = _silu(yc * lax.rsqrt(var + EPS) * gln_ref[...] + bln_ref[...])
    o_ref[0] = x + _dot(v.astype(BF16), w2_ref[...]) + b2_ref[...]


def conformer(x, hist, g, w1, b1, wdw, bdw, gln, bln, w2, b2):
    b, t, d = x.shape
    tm = min(TOKEN_TILE, t)
    xs = _halo_specs(b, t, tm, d, CONF_HALO)
    hist_pad = jnp.pad(hist, ((0, 0), (CONF_HALO - (CONF_K - 1), 0), (0, 0)))
    per_b = lambda n: pl.BlockSpec((1, n, d), lambda bi, ti: (bi, 0, 0))
    return pl.pallas_call(
        functools.partial(_conf_kernel, has_halo=len(xs) == 2),
        grid=(b, t // tm),
        in_specs=xs + [per_b(CONF_HALO), _resident((1, d)), _resident((d, 2 * d)), _resident((1, 2 * d)),
                       _resident((CONF_K, d)), _resident((1, d)), _resident((1, d)), _resident((1, d)),
                       _resident((d, d)), _resident((1, d))],
        out_specs=[pl.BlockSpec((1, tm, d), lambda bi, ti: (bi, ti, 0)), per_b(CONF_K - 1)],
        out_shape=[jax.ShapeDtypeStruct((b, t, d), F32), jax.ShapeDtypeStruct((b, CONF_K - 1, d), F32)],
        scratch_shapes=[pltpu.VMEM((CONF_HALO + tm + 8, d), F32),
                        pltpu.VMEM((CONF_HALO + tm, d), F32)],
        compiler_params=_cparams("parallel", "arbitrary"),
        name="conformer",
    )(*([x] * len(xs)), hist_pad, g, w1.astype(BF16), b1.reshape(1, -1), wdw, bdw.reshape(1, -1),
      gln.reshape(1, -1), bln.reshape(1, -1), w2.astype(BF16), b2.reshape(1, -1))


def _pool_kernel(*refs, has_halo, pos0):
    if has_halo:
        x_ref, xh_ref, *refs = refs
    else:
        x_ref, *refs = refs
    hist_ref, g_ref, w_ref, b_ref, scale_ref, o_ref, hs_ref, ext = refs
    tm = x_ref.shape[1]
    i = pl.program_id(1)
    x = x_ref[0]
    xn = _rms(x, g_ref[...])
    if has_halo:
        prev = jnp.where(i == 0, hist_ref[0], _rms(xh_ref[0], g_ref[...]))
    else:
        prev = hist_ref[0]
    ext[0:POOL_HALO] = prev
    ext[POOL_HALO:] = xn
    hs_ref[0] = ext[tm + POOL_HALO - POOL_HIST:tm + POOL_HALO, :]
    pos = pos0 + i * tm + lax.broadcasted_iota(jnp.int32, (tm, 1), 0)
    ys = []
    for gi, wlen in enumerate(POOL_WINDOWS):
        lanes = slice(gi * POOL_GROUP, (gi + 1) * POOL_GROUP)
        s = ext[POOL_HALO:POOL_HALO + tm, lanes]
        for k in range(1, wlen):
            s = s + ext[POOL_HALO - k:POOL_HALO - k + tm, lanes]
        cnt = jnp.minimum(pos + 1, wlen).astype(F32)
        dg = s / cnt - xn[:, lanes]
        ys.append(_dot(dg.astype(BF16), w_ref[gi]))
    y = (jnp.concatenate(ys, axis=1) + b_ref[...]) * scale_ref[...]
    o_ref[0] = x + y


def pool_mixer(x, hist, pos0, g, w, bias, scale):
    b, t, d = x.shape
    tm = min(TOKEN_TILE, t)
    xs = _halo_specs(b, t, tm, d, POOL_HALO)
    hist_pad = jnp.pad(hist, ((0, 0), (POOL_HALO - POOL_HIST, 0), (0, 0)))
    per_b = lambda n: pl.BlockSpec((1, n, d), lambda bi, ti: (bi, 0, 0))
    ng = len(POOL_WINDOWS)
    return pl.pallas_call(
        functools.partial(_pool_kernel, has_halo=len(xs) == 2, pos0=pos0),
        grid=(b, t // tm),
        in_specs=xs + [per_b(POOL_HALO), _resident((1, d)), _resident((ng, POOL_GROUP, POOL_GROUP)),
                       _resident((1, d)), _resident((1, d))],
        out_specs=[pl.BlockSpec((1, tm, d), lambda bi, ti: (bi, ti, 0)), per_b(POOL_HIST)],
        out_shape=[jax.ShapeDtypeStruct((b, t, d), F32), jax.ShapeDtypeStruct((b, POOL_HIST, d), F32)],
        scratch_shapes=[pltpu.VMEM((POOL_HALO + tm, d), F32)],
        compiler_params=_cparams("parallel", "arbitrary"),
        name="pool_mixer",
    )(*([x] * len(xs)), hist_pad, g, w.astype(BF16), bias.reshape(1, -1), scale.reshape(1, -1))


def _ssd_kernel(*refs, has_halo, valid):
    if has_halo:
        x_ref, xh_ref, *refs = refs
    else:
        x_ref, *refs = refs
    (chist_ref, h0_ref, g_ref, wz_ref, wxbc_ref, wdt_ref, wconv_ref, bconv_ref, dtb_ref, a_ref, dskip_ref,
     gn_ref, wout_ref, exp_ref, expt_ref, o_ref, cs_out_ref, h_out_ref, ext, hstate) = refs
    L = x_ref.shape[1]
    i = pl.program_id(1)
    gw = SSD_HPG * SSD_HD
    gn = SSD_GROUPS * SSD_N

    @pl.when(i == 0)
    def _():
        hstate[...] = h0_ref[0]

    x = x_ref[0]
    xb = _rms(x, g_ref[...]).astype(BF16)
    z = _dot(xb, wz_ref[...])
    dtr = _dot(xb, wdt_ref[...])
    if has_halo:
        prev = jnp.where(i == 0, chist_ref[0], _dot(_rms(xh_ref[0], g_ref[...]).astype(BF16), wxbc_ref[...]))
    else:
        prev = chist_ref[0]
    ext[0:SSD_HALO] = prev
    ext[SSD_HALO:] = _dot(xb, wxbc_ref[...])
    off = SSD_HALO - (SSD_CONV_K - 1)
    y = jnp.broadcast_to(bconv_ref[...], (L, SSD_CONV_DIM))
    for k in range(SSD_CONV_K):
        y = y + wconv_ref[k:k + 1, :] * ext[off + k:off + k + L, :]
    cs_out_ref[0] = ext[valid + off:valid + SSD_HALO, :]
    xbc = _silu(y)
    xs = xbc[:, :SSD_INNER]
    bm = xbc[:, SSD_INNER:SSD_INNER + gn].astype(BF16)
    cm = xbc[:, SSD_INNER + gn:].astype(BF16)

    head_lane = lax.broadcasted_iota(jnp.int32, (L, LANES), 1) < SSD_HEADS
    live = head_lane
    if valid < L:
        live = live & (lax.broadcasted_iota(jnp.int32, (L, LANES), 0) < valid)
    dt = jnp.where(live, _softplus(dtr + dtb_ref[...]), 0.0)
    cs = _dot3_left(_tril(L), dt * a_ref[...])
    cst = cs.T
    dtt = dt.T
    to_end = jnp.exp(cs[L - 1:L, :] - cs) * dt
    td_x = _dot(to_end.astype(BF16), exp_ref[...])
    ecs_x = _dot3(jnp.exp(cs), exp_ref[...])
    chunk_decay = jnp.broadcast_to(jnp.exp(cst[:, L - 1:L]), (LANES, LANES))
    xs_b = xs.astype(BF16)
    xtd = xs * td_x
    r = lax.broadcasted_iota(jnp.int32, (L, L), 0)
    c = lax.broadcasted_iota(jnp.int32, (L, L), 1)
    causal = c <= r
    low_half = lax.broadcasted_iota(jnp.int32, (L, 2 * SSD_HD), 1) < SSD_HD

    y_groups = []
    for g in range(SSD_GROUPS):
        cg = cm[:, g * SSD_N:(g + 1) * SSD_N]
        bg = bm[:, g * SSD_N:(g + 1) * SSD_N]
        cb = _dot_nt(cg, bg)
        hg = hstate[g]
        y_off = _dot_nt(cg, hg.astype(BF16)) * ecs_x[:, g * gw:(g + 1) * gw]
        pairs = []
        for pr in range(SSD_HPG // 2):
            lo = g * gw + pr * 2 * SSD_HD
            xpair = xs_b[:, lo:lo + 2 * SSD_HD]
            acc = None
            for hh in range(2):
                e = g * SSD_HPG + 2 * pr + hh
                seg = cs[:, e:e + 1] - cst[e:e + 1, :]
                w = cb * jnp.exp(jnp.where(causal, seg, -jnp.inf)) * dtt[e:e + 1, :]
                xm = jnp.where(low_half == (hh == 0), xpair, jnp.zeros_like(xpair))
                term = _dot(w.astype(BF16), xm)
                acc = term if acc is None else acc + term
            pairs.append(acc)
        y_groups.append(jnp.concatenate(pairs, axis=1) + y_off)
        s_g = _dot(xtd[:, g * gw:(g + 1) * gw].T.astype(BF16), bg)
        hstate[g] = hg * _dot3_left(expt_ref[g], chunk_decay) + s_g

    ys = (jnp.concatenate(y_groups, axis=1) + dskip_ref[...] * xs) * _silu(z)
    normed = []
    for g in range(SSD_GROUPS):
        yg = ys[:, g * gw:(g + 1) * gw]
        normed.append(yg * lax.rsqrt(jnp.mean(yg * yg, axis=-1, keepdims=True) + EPS))
    yn = jnp.concatenate(normed, axis=1) * gn_ref[...]
    o_ref[0] = x + _dot(yn.astype(BF16), wout_ref[...])

    @pl.when(i == pl.num_programs(1) - 1)
    def _():
        h_out_ref[0] = hstate[...]


def _ssd_expand_constants():
    ch = np.arange(SSD_INNER)
    exp = np.zeros((LANES, SSD_INNER), np.float32)
    exp[ch // SSD_HD, ch] = 1.0
    gw = SSD_HPG * SSD_HD
    rows = np.arange(gw)
    expt = np.zeros((SSD_GROUPS, gw, LANES), np.float32)
    for g in range(SSD_GROUPS):
        expt[g, rows, g * SSD_HPG + rows // SSD_HD] = 1.0
    return jnp.asarray(exp, BF16), jnp.asarray(expt, BF16)


def ssd_mixer(x, conv_hist, h0, g, w_in, w_conv, b_conv, dt_bias, a_log, d_skip, g_norm, w_out):
    b, t, d = x.shape
    valid = t
    if t < LANES:
        x = jnp.pad(x, ((0, 0), (0, LANES - t), (0, 0)))
    tp = x.shape[1]
    L = min(SSD_CHUNK, tp)
    valid = L if tp > L else valid
    xs = _halo_specs(b, tp, L, d, SSD_HALO)
    gw = SSD_HPG * SSD_HD
    hist_pad = jnp.pad(conv_hist, ((0, 0), (SSD_HALO - (SSD_CONV_K - 1), 0), (0, 0)))
    wz = w_in[:, :SSD_INNER].astype(BF16)
    wxbc = w_in[:, SSD_INNER:SSD_INNER + SSD_CONV_DIM].astype(BF16)
    pad_heads = lambda v: jnp.zeros((v.shape[0], LANES), F32).at[:, :SSD_HEADS].set(v)
    wdt = pad_heads(w_in[:, SSD_INNER + SSD_CONV_DIM:]).astype(BF16)
    dtb = pad_heads(dt_bias.reshape(1, -1))
    a = pad_heads(-jnp.exp(a_log.astype(F32)).reshape(1, -1))
    dskip = jnp.repeat(d_skip, SSD_HD).reshape(1, -1)
    exp, expt = _ssd_expand_constants()
    per_b3 = lambda n, w: pl.BlockSpec((1, n, w), lambda bi, ti: (bi, 0, 0))
    state = pl.BlockSpec((1, SSD_GROUPS, gw, SSD_N), lambda bi, ti: (bi, 0, 0, 0))
    out, conv_new, h_new = pl.pallas_call(
        functools.partial(_ssd_kernel, has_halo=len(xs) == 2, valid=valid),
        grid=(b, tp // L),
        in_specs=xs + [per_b3(SSD_HALO, SSD_CONV_DIM), state, _resident((1, d)),
                       _resident((d, SSD_INNER)), _resident((d, SSD_CONV_DIM)), _resident((d, LANES)),
                       _resident((SSD_CONV_K, SSD_CONV_DIM)), _resident((1, SSD_CONV_DIM)),
                       _resident((1, LANES)), _resident((1, LANES)), _resident((1, SSD_INNER)),
                       _resident((1, SSD_INNER)), _resident((SSD_INNER, d)),
                       _resident((LANES, SSD_INNER)), _resident((SSD_GROUPS, gw, LANES))],
        out_specs=[pl.BlockSpec((1, L, d), lambda bi, ti: (bi, ti, 0)),
                   per_b3(SSD_CONV_K - 1, SSD_CONV_DIM), state],
        out_shape=[jax.ShapeDtypeStruct((b, tp, d), F32),
                   jax.ShapeDtypeStruct((b, SSD_CONV_K - 1, SSD_CONV_DIM), F32),
                   jax.ShapeDtypeStruct((b, SSD_GROUPS, gw, SSD_N), F32)],
        scratch_shapes=[pltpu.VMEM((SSD_HALO + L, SSD_CONV_DIM), F32), pltpu.VMEM((SSD_GROUPS, gw, SSD_N), F32)],
        compiler_params=_cparams("parallel", "arbitrary"),
        name="ssd_mixer",
    )(*([x] * len(xs)), hist_pad, h0.reshape(b, SSD_GROUPS, gw, SSD_N), g, wz, wxbc, wdt, w_conv,
      b_conv.reshape(1, -1), dtb, a, dskip, g_norm.reshape(1, -1), w_out.astype(BF16), exp, expt)
    return out[:, :t], conv_new, h_new.reshape(b, SSD_HEADS, SSD_HD, SSD_N)


def kernel(x_prompt, x_sample, cache_fox_k, cache_fox_v, cache_fox_logf, state_conf_conv, state_ssd_conv, state_ssd, state_pool, p_prompt, p_sample, g_mix, g_ffn, g_ple, g_final, fox_w_q, fox_w_k, fox_w_v, fox_w_f, fox_b_f, fox_w_o, conf_w_pw1, conf_b_pw1, conf_w_dw, conf_b_dw, conf_g_ln, conf_b_ln, conf_w_pw2, conf_b_pw2, ssd_w_in, ssd_w_conv, ssd_b_conv, ssd_dt_bias, ssd_a_log, ssd_d, ssd_g_norm, ssd_w_out, pool_w, pool_b, pool_scale, ffn_w_gate, ffn_w_up, ffn_w_down, moe_w_router, moe_w_gate, moe_w_up, moe_w_down, ple_w_up, ple_w_gate):
    b, t, d = x_prompt.shape
    bs, s, _ = x_sample.shape
    past = cache_fox_k.shape[1]
    row = lambda v: v.reshape(1, -1)
    bf = lambda w: w.astype(BF16)
    flat = lambda a: a.reshape(-1, a.shape[-1])
    hp, hs = x_prompt, x_sample

    def dense_ffn(h, p, i):
        j = i // 2
        out = ffn_ple(flat(h), flat(p[i]), row(g_ffn[i]), bf(ffn_w_gate[j]), bf(ffn_w_up[j]), bf(ffn_w_down[j]),
                      row(g_ple[i]), bf(ple_w_gate[i]), bf(ple_w_up[i]))
        return out.reshape(h.shape)

    def moe_ffn(h, p, i, final_norm):
        j = i // 2
        wr = jnp.zeros((d, LANES), F32).at[:, :N_EXPERTS].set(moe_w_router[j]).astype(BF16)
        out = moe_ple(flat(h), flat(p[i]), row(g_ffn[i]), wr, bf(moe_w_gate[j]), bf(moe_w_up[j]),
                      bf(moe_w_down[j]), row(g_ple[i]), bf(ple_w_gate[i]), bf(ple_w_up[i]), row(g_final),
                      final_norm)
        return out.reshape(h.shape)

    fox_k_p, fox_v_p, fox_lf_p, qt, vt, kaug, ft, qn2, kn2 = fox_proj(hp, row(g_mix[0]), fox_w_q, fox_w_k,
                                                                      fox_w_v, fox_w_f, fox_b_f)
    ub = fox_dead_block_bounds(qn2, kn2, ft, vt.shape[3])
    hp = proj_ffn_ple(hp, fox_attention(qt, kaug, vt, ft, ub), fox_w_o, p_prompt[0], row(g_ffn[0]),
                      bf(ffn_w_gate[0]), bf(ffn_w_up[0]), bf(ffn_w_down[0]), row(g_ple[0]), bf(ple_w_gate[0]),
                      bf(ple_w_up[0]))
    hs, fox_k_s, fox_v_s, fox_lf_s = fox_sample(hs, row(g_mix[0]), fox_w_q, fox_w_k, fox_w_v, fox_w_f, fox_b_f,
                                                fox_w_o, cache_fox_k, cache_fox_v, cache_fox_logf)
    hs = dense_ffn(hs, p_sample, 0)

    conf_args = (row(g_mix[1]), conf_w_pw1, conf_b_pw1, conf_w_dw, conf_b_dw, conf_g_ln, conf_b_ln,
                 conf_w_pw2, conf_b_pw2)
    hp, conf_p = conformer(hp, jnp.zeros((b, CONF_K - 1, d), F32), *conf_args)
    hs, conf_s = conformer(hs, state_conf_conv, *conf_args)
    hp = moe_ffn(hp, p_prompt, 1, False)
    hs = moe_ffn(hs, p_sample, 1, False)

    ssd_args = (row(g_mix[2]), ssd_w_in, ssd_w_conv, ssd_b_conv, ssd_dt_bias, ssd_a_log, ssd_d, ssd_g_norm,
                ssd_w_out)
    hp, ssdc_p, ssdh_p = ssd_mixer(hp, jnp.zeros((b, SSD_CONV_K - 1, SSD_CONV_DIM), F32),
                                   jnp.zeros((b, SSD_HEADS, SSD_HD, SSD_N), F32), *ssd_args)
    hs, ssdc_s, ssdh_s = ssd_mixer(hs, state_ssd_conv, state_ssd, *ssd_args)
    hp = dense_ffn(hp, p_prompt, 2)
    hs = dense_ffn(hs, p_sample, 2)

    pool_args = (row(g_mix[3]), pool_w, pool_b, pool_scale)
    hp, pool_p = pool_mixer(hp, jnp.zeros((b, POOL_HIST, d), F32), 0, *pool_args)
    hs, pool_s = pool_mixer(hs, state_pool, past, *pool_args)
    y_prompt = moe_ffn(hp, p_prompt, 3, True)
    y_sample = moe_ffn(hs, p_sample, 3, True)

    heads = lambda a: a.reshape(a.shape[0], a.shape[1], FOX_HEADS, FOX_HD)
    return (y_prompt, y_sample, heads(fox_k_p), heads(fox_v_p), fox_lf_p, heads(fox_k_s), heads(fox_v_s), fox_lf_s,
            conf_p, conf_s, ssdc_p, ssdc_s, ssdh_p, ssdh_s, pool_p, pool_s)
```

```python
import functools

import jax
import jax.numpy as jnp
import numpy as np
from jax import lax
from jax.experimental import pallas as pl
from jax.experimental.pallas import tpu as pltpu

F32 = jnp.float32
BF16 = jnp.bfloat16

D_MODEL = 1024
EPS = 1e-6
PLE_DIM = 256
FOX_HEADS = 16
FOX_HD = 64
FOX_PAIRS = FOX_HEADS // 2
FOX_VROWS = FOX_HD + 16
CONF_K = 31
CONF_HALO = 32
SSD_INNER = 2048
SSD_HD = 64
SSD_HEADS = 32
SSD_GROUPS = 4
SSD_HPG = SSD_HEADS // SSD_GROUPS
SSD_N = 128
SSD_CONV_K = 4
SSD_CONV_DIM = SSD_INNER + 2 * SSD_GROUPS * SSD_N
SSD_HALO = 8
POOL_WINDOWS = (2, 4, 8, 16)
POOL_GROUP = D_MODEL // len(POOL_WINDOWS)
POOL_HIST = 15
POOL_HALO = 16
N_EXPERTS = 8
D_FF = 2816
D_FF_EXPERT = 1408
LANES = 128

VMEM_LIMIT_BYTES = 56 * 1024 * 1024
TOKEN_TILE = 512
FFN_TILE = 1024
ATTN_BLOCK = 512
ATTN_STRIP = 256
ATTN_KEYS = 512
NORM_SLACK = 1.01
DEAD_LOGIT_GAP = 106.0
SSD_CHUNK = 256
FFN_CHUNK = 256
MOE_TILE = 1024
MOE_SUBTILE = 512
MOE_CHUNK = 160
MOE_SCATTER = 256


def _cparams(*sem):
    return pltpu.CompilerParams(dimension_semantics=sem, vmem_limit_bytes=VMEM_LIMIT_BYTES)


def _resident(shape):
    n = len(shape)
    return pl.BlockSpec(shape, lambda *_: (0,) * n, pipeline_mode=pl.Buffered(1))


def _dot(a, b):
    return jnp.dot(a, b, preferred_element_type=F32)


def _dot_nt(a, b):
    return lax.dot_general(a, b, (((1,), (1,)), ((), ())), preferred_element_type=F32)


def _rms(x, g):
    return x * lax.rsqrt(jnp.mean(x * x, axis=-1, keepdims=True) + EPS) * g


def _sigmoid(x):
    return 1.0 / (1.0 + jnp.exp(-x))


def _silu(x):
    return x * _sigmoid(x)


def _softplus(x):
    return jnp.maximum(x, 0.0) + jnp.log1p(jnp.exp(-jnp.abs(x)))


def _split3(x):
    p1 = x.astype(BF16)
    r1 = x - p1.astype(F32)
    p2 = r1.astype(BF16)
    r2 = r1 - p2.astype(F32)
    return p1, p2, r2.astype(BF16)


def _dot3(a_f32, b_bf16):
    p1, p2, p3 = _split3(a_f32)
    return (_dot(p1, b_bf16) + _dot(p2, b_bf16)) + _dot(p3, b_bf16)


def _dot3_left(a_bf16, b_f32):
    p1, p2, p3 = _split3(b_f32)
    return (_dot(a_bf16, p1) + _dot(a_bf16, p2)) + _dot(a_bf16, p3)


def _rep_rows(a, n):
    s, w = a.shape
    return jnp.broadcast_to(a[:, None, :], (s, n, w)).reshape(s * n, w)


def _tile_rows(a, n):
    h, w = a.shape
    return jnp.broadcast_to(a[None], (n, h, w)).reshape(n * h, w)


def _tril(n):
    r = lax.broadcasted_iota(jnp.int32, (n, n), 0)
    c = lax.broadcasted_iota(jnp.int32, (n, n), 1)
    return jnp.where(c <= r, 1.0, 0.0).astype(BF16)


def _ple_apply(h1, p, g_ple, w_gate, w_up):
    hn = _rms(h1, g_ple).astype(BF16)
    gate = _sigmoid(_dot(hn, w_gate))
    return h1 + gate * _dot(p.astype(BF16), w_up)


def _ffn_ple_apply(x, p, gf_ref, wg_ref, wu_ref, wd_ref, gp_ref, wpg_ref, wpu_ref):
    xn = _rms(x, gf_ref[...]).astype(BF16)
    acc = x
    for c in range(D_FF // FFN_CHUNK):
        sl = slice(c * FFN_CHUNK, (c + 1) * FFN_CHUNK)
        hh = (_silu(_dot(xn, wg_ref[:, sl])) * _dot(xn, wu_ref[:, sl])).astype(BF16)
        acc = acc + _dot(hh, wd_ref[sl, :])
    return _ple_apply(acc, p, gp_ref[...], wpg_ref[...], wpu_ref[...])


def _ffn_ple_kernel(h_ref, p_ref, *refs):
    *w_refs, o_ref = refs
    o_ref[...] = _ffn_ple_apply(h_ref[...], p_ref[...], *w_refs)


def _proj_ffn_ple_kernel(h_ref, ot_ref, wo_ref, p_ref, *refs):
    *w_refs, o_ref = refs
    x = h_ref[0] + pl.dot(ot_ref[0], wo_ref[...], trans_a=True)
    o_ref[0] = _ffn_ple_apply(x, p_ref[0], *w_refs)


def proj_ffn_ple(h, ot, wo, p, g_ffn, wg, wu, wd, g_ple, wpg, wpu):
    b, t, d = h.shape
    tm = min(FFN_TILE, t)
    tile = lambda w: pl.BlockSpec((1, tm, w), lambda bi, ti: (bi, ti, 0))
    return pl.pallas_call(
        _proj_ffn_ple_kernel,
        grid=(b, t // tm),
        in_specs=[tile(d), pl.BlockSpec((1, d, tm), lambda bi, ti: (bi, 0, ti)), _resident((d, d)), tile(PLE_DIM),
                  _resident((1, d)), _resident((d, D_FF)), _resident((d, D_FF)), _resident((D_FF, d)),
                  _resident((1, d)), _resident((d, d)), _resident((PLE_DIM, d))],
        out_specs=tile(d),
        out_shape=jax.ShapeDtypeStruct((b, t, d), F32),
        compiler_params=_cparams("parallel", "parallel"),
        name="proj_ffn_ple",
    )(h, ot, wo.astype(BF16), p, g_ffn, wg, wu, wd, g_ple, wpg, wpu)


def ffn_ple(h, p, g_ffn, wg, wu, wd, g_ple, wpg, wpu):
    m = h.shape[0]
    tm = min(FFN_TILE, m)
    row = lambda w: pl.BlockSpec((tm, w), lambda i: (i, 0))
    return pl.pallas_call(
        _ffn_ple_kernel,
        grid=(m // tm,),
        in_specs=[row(D_MODEL), row(PLE_DIM), _resident((1, D_MODEL)),
                  _resident((D_MODEL, D_FF)), _resident((D_MODEL, D_FF)), _resident((D_FF, D_MODEL)),
                  _resident((1, D_MODEL)), _resident((D_MODEL, D_MODEL)), _resident((PLE_DIM, D_MODEL))],
        out_specs=row(D_MODEL),
        out_shape=jax.ShapeDtypeStruct((m, D_MODEL), F32),
        compiler_params=_cparams("parallel"),
        name="ffn_ple",
    )(h, p, g_ffn, wg, wu, wd, g_ple, wpg, wpu)


def _moe_ple_kernel(h_ref, p_ref, gf_ref, wr_ref, wg_ref, wu_ref, wd_ref, gp_ref, wpg_ref, wpu_ref, gfin_ref,
                    upper_ref, lower_ref, o_ref, xn_s, comb_s, combt_s, acc_s, *, final_norm):
    e = pl.program_id(1)

    @pl.when(e == 0)
    def _():
        x = h_ref[...]
        xn = _rms(x, gf_ref[...]).astype(BF16)
        xn_s[...] = xn
        acc_s[...] = x
        logits = _dot(xn, wr_ref[...])
        lane = lax.broadcasted_iota(jnp.int32, logits.shape, 1).astype(F32)
        neg = jnp.float32(-jnp.inf)
        lg = jnp.where(lane < N_EXPERTS, logits, neg)
        m1 = jnp.max(lg, axis=-1, keepdims=True)
        i1 = jnp.min(jnp.where(lg == m1, lane, float(LANES)), axis=-1, keepdims=True)
        lg2 = jnp.where(lane == i1, neg, lg)
        m2 = jnp.max(lg2, axis=-1, keepdims=True)
        i2 = jnp.min(jnp.where(lg2 == m2, lane, float(LANES)), axis=-1, keepdims=True)
        e2 = jnp.exp(m2 - m1)
        den = 1.0 + e2
        comb = jnp.where(lane == i1, 1.0 / den, jnp.where(lane == i2, e2 / den, 0.0))
        comb_s[...] = comb
        combt_s[...] = comb.T[:N_EXPERTS, :]

    ts = upper_ref.shape[0]
    for sb in range(xn_s.shape[0] // ts):
        rows = slice(sb * ts, (sb + 1) * ts)
        lane = lax.broadcasted_iota(jnp.int32, (ts, LANES), 1)
        gate_col = jnp.sum(jnp.where(lane == e, comb_s[rows, :], 0.0), axis=-1, keepdims=True)
        gate_row = combt_s[pl.ds(e, 1), rows]
        sel_row = jnp.where(gate_row > 0.0, 1.0, 0.0)
        sel_col = jnp.where(gate_col > 0.0, 1.0, 0.0)
        rank_row = _dot(jnp.broadcast_to(sel_row, (8, ts)).astype(BF16), upper_ref[...])[0:1, :]
        rank_col = _dot(lower_ref[...], jnp.broadcast_to(sel_col, (ts, MOE_SCATTER)).astype(BF16))
        n_chunks = (jnp.max(rank_row).astype(jnp.int32) + (MOE_CHUNK - 1)) // MOE_CHUNK
        slot_row = lax.broadcasted_iota(jnp.int32, (MOE_CHUNK, ts), 0).astype(F32) + 1.0
        slot_lane = lax.broadcasted_iota(jnp.int32, (ts, MOE_SCATTER), 1)
        slot_col = slot_lane.astype(F32) + 1.0
        routed_col = (sel_col > 0.0) & (slot_lane < MOE_CHUNK)

        def chunk(ci, carry, rows=rows, gate_col=gate_col, sel_row=sel_row, rank_row=rank_row, rank_col=rank_col,
                  slot_row=slot_row, slot_col=slot_col, routed_col=routed_col):
            base = (ci * MOE_CHUNK).astype(F32)
            gather = jnp.where((sel_row > 0.0) & (rank_row == slot_row + base), 1.0, 0.0).astype(BF16)
            xg = _dot(gather, xn_s[rows, :]).astype(BF16)
            hh = (_silu(_dot(xg, wg_ref[0])) * _dot(xg, wu_ref[0])).astype(BF16)
            y = _dot(hh, wd_ref[0]).astype(BF16)
            y = jnp.concatenate([y, jnp.zeros((MOE_SCATTER - MOE_CHUNK, D_MODEL), BF16)], axis=0)
            scatter = jnp.where(routed_col & (rank_col == slot_col + base), 1.0, 0.0).astype(BF16)
            acc_s[rows, :] += gate_col * _dot(scatter, y)
            return carry

        lax.fori_loop(0, n_chunks, chunk, 0)

    @pl.when(e == N_EXPERTS - 1)
    def _():
        h2 = _ple_apply(acc_s[...], p_ref[...], gp_ref[...], wpg_ref[...], wpu_ref[...])
        if final_norm:
            h2 = _rms(h2, gfin_ref[...])
        o_ref[...] = h2


def moe_ple(h, p, g_ffn, wr, wg, wu, wd, g_ple, wpg, wpu, g_final, final_norm):
    m = h.shape[0]
    tm = min(MOE_TILE, m)
    row = lambda w: pl.BlockSpec((tm, w), lambda i, e: (i, 0))
    expert = lambda a, b: pl.BlockSpec((1, a, b), lambda i, e: (e, 0, 0))
    ts = min(MOE_SUBTILE, tm)
    upper = jnp.asarray(np.triu(np.ones((ts, ts), np.float32)), BF16)
    lower = jnp.asarray(np.tril(np.ones((ts, ts), np.float32)), BF16)
    return pl.pallas_call(
        functools.partial(_moe_ple_kernel, final_norm=final_norm),
        grid=(m // tm, N_EXPERTS),
        in_specs=[row(D_MODEL), row(PLE_DIM), _resident((1, D_MODEL)), _resident((D_MODEL, LANES)),
                  expert(D_MODEL, D_FF_EXPERT), expert(D_MODEL, D_FF_EXPERT), expert(D_FF_EXPERT, D_MODEL),
                  _resident((1, D_MODEL)), _resident((D_MODEL, D_MODEL)), _resident((PLE_DIM, D_MODEL)),
                  _resident((1, D_MODEL)), _resident((ts, ts)), _resident((ts, ts))],
        out_specs=row(D_MODEL),
        out_shape=jax.ShapeDtypeStruct((m, D_MODEL), F32),
        scratch_shapes=[pltpu.VMEM((tm, D_MODEL), BF16), pltpu.VMEM((tm, LANES), F32),
                        pltpu.VMEM((N_EXPERTS, tm), F32), pltpu.VMEM((tm, D_MODEL), F32)],
        compiler_params=_cparams("parallel", "arbitrary"),
        name="moe_ple",
    )(h, p, g_ffn, wr, wg, wu, wd, g_ple, wpg, wpu, g_final, upper, lower)


def _log_sigmoid(z):
    return jnp.minimum(z, 0.0) - jnp.log1p(jnp.exp(-jnp.abs(z)))


def _fox_proj_kernel(x_ref, g_ref, wqt_ref, wk_ref, wv_ref, wf_ref, bf_ref, sel_ref, ones_ref, hsel_ref,
                     k_out, v_out, lf_out, qt_out, vt_out, kaug_out, ft_out, qn_out, kn_out, carry):
    tm = x_ref.shape[1]

    @pl.when(pl.program_id(1) == 0)
    def _():
        carry[...] = jnp.zeros_like(carry)

    xn = _rms(x_ref[0], g_ref[...])
    xb = xn.astype(BF16)
    xnt = xn.T.astype(BF16)
    k = _dot(xb, wk_ref[...])
    v = _dot(xb, wv_ref[...])
    k_out[0] = k
    v_out[0] = v
    qtb = (_dot(wqt_ref[...], xnt) * (FOX_HD ** -0.5)).astype(BF16)
    qt_out[0] = qtb
    kb = k.astype(BF16)
    qsq = qtb.astype(F32)
    ksq = kb.astype(F32)
    qn2 = _dot(hsel_ref[...], (qsq * qsq).astype(BF16))
    kn2 = _dot_nt((ksq * ksq).astype(BF16), hsel_ref[...])
    qn_out[0, 0] = jnp.broadcast_to(jnp.max(qn2, axis=1, keepdims=True), (LANES, LANES))[:FOX_HEADS]
    kn_out[0, 0] = jnp.broadcast_to(jnp.max(kn2, axis=0, keepdims=True), (8, LANES))
    vt = v.T.astype(BF16)
    ones_rows = jnp.ones((FOX_VROWS - FOX_HD, tm), BF16)
    for h in range(FOX_HEADS):
        vt_out[0, 0, FOX_VROWS * h:FOX_VROWS * h + FOX_HD, :] = vt[FOX_HD * h:FOX_HD * (h + 1), :]
        vt_out[0, 0, FOX_VROWS * h + FOX_HD:FOX_VROWS * (h + 1), :] = ones_rows
    lf = _log_sigmoid(_dot(xb, wf_ref[...]) + bf_ref[...])
    lf_out[0] = lf[:, :FOX_HEADS]
    f = _dot3_left(_tril(tm), lf) + carry[...]
    carry[...] = f[tm - 1:tm, :]
    ft_out[0] = f.T[:FOX_HEADS, :]
    p1, p2, p3 = _split3(f)
    aug = ((_dot(p1, sel_ref[0]) + _dot(p2, sel_ref[1])) + _dot(p3, sel_ref[2]) + ones_ref[...]).astype(BF16)
    for p in range(FOX_PAIRS):
        kaug_out[0, :, 2 * LANES * p:2 * LANES * p + LANES] = kb[:, LANES * p:LANES * (p + 1)]
        kaug_out[0, :, 2 * LANES * p + LANES:2 * LANES * (p + 1)] = aug[:, LANES * p:LANES * (p + 1)]


def _fox_aug_constants():
    h = np.arange(FOX_HEADS)
    sel = np.zeros((3, LANES, FOX_PAIRS * LANES), np.float32)
    for r in range(3):
        sel[r, h, LANES * (h // 2) + 16 * (h % 2) + r] = 1.0
    p = np.arange(FOX_PAIRS)
    ones = np.zeros((1, FOX_PAIRS * LANES), np.float32)
    for r in range(3):
        ones[0, LANES * p + 32 + r] = 1.0
    ch = np.arange(D_MODEL)
    hsel = np.zeros((LANES, D_MODEL), np.float32)
    hsel[ch // FOX_HD, ch] = 1.0
    return jnp.asarray(sel, BF16), jnp.asarray(ones), jnp.asarray(hsel, BF16)


def fox_proj(x, g, wq, wk, wv, wf, bf):
    b, t, d = x.shape
    tm = min(ATTN_BLOCK, t)
    sel, ones, hsel = _fox_aug_constants()
    wf_pad = jnp.zeros((d, LANES), F32).at[:, :FOX_HEADS].set(wf).astype(BF16)
    bf_pad = jnp.zeros((1, LANES), F32).at[0, :FOX_HEADS].set(bf)
    tile = lambda w: pl.BlockSpec((1, tm, w), lambda bi, ti: (bi, ti, 0))
    return pl.pallas_call(
        _fox_proj_kernel,
        grid=(b, t // tm),
        in_specs=[tile(d), _resident((1, d)), _resident((d, d)), _resident((d, d)), _resident((d, d)),
                  _resident((d, LANES)), _resident((1, LANES)), _resident((3, LANES, FOX_PAIRS * LANES)),
                  _resident((1, FOX_PAIRS * LANES)), _resident((LANES, d))],
        out_specs=[tile(d), tile(d), tile(FOX_HEADS),
                   pl.BlockSpec((1, d, tm), lambda bi, ti: (bi, 0, ti)),
                   pl.BlockSpec((1, 1, FOX_HEADS * FOX_VROWS, tm), lambda bi, ti: (bi, ti, 0, 0)),
                   tile(2 * d),
                   pl.BlockSpec((1, FOX_HEADS, tm), lambda bi, ti: (bi, 0, ti)),
                   pl.BlockSpec((1, 1, FOX_HEADS, LANES), lambda bi, ti: (bi, ti, 0, 0)),
                   pl.BlockSpec((1, 1, 8, LANES), lambda bi, ti: (bi, ti, 0, 0))],
        out_shape=[jax.ShapeDtypeStruct((b, t, d), F32), jax.ShapeDtypeStruct((b, t, d), F32),
                   jax.ShapeDtypeStruct((b, t, FOX_HEADS), F32),
                   jax.ShapeDtypeStruct((b, d, t), BF16),
                   jax.ShapeDtypeStruct((b, t // tm, FOX_HEADS * FOX_VROWS, tm), BF16),
                   jax.ShapeDtypeStruct((b, t, 2 * d), BF16),
                   jax.ShapeDtypeStruct((b, FOX_HEADS, t), F32),
                   jax.ShapeDtypeStruct((b, t // tm, FOX_HEADS, LANES), F32),
                   jax.ShapeDtypeStruct((b, t // tm, 8, LANES), F32)],
        scratch_shapes=[pltpu.VMEM((1, LANES), F32)],
        compiler_params=_cparams("arbitrary", "arbitrary"),
        name="fox_proj",
    )(x, g, wq.T.astype(BF16), wk.astype(BF16), wv.astype(BF16), wf_pad, bf_pad, sel, ones, hsel)


def fox_dead_block_bounds(qn2, kn2, ft, blk):
    b, nb = qn2.shape[:2]
    qn = jnp.sqrt(qn2[..., 0] * NORM_SLACK).transpose(0, 2, 1)
    kn = jnp.sqrt(kn2[:, :, 0, :FOX_HEADS] * NORM_SLACK).transpose(0, 2, 1)
    kn_upto = lax.cummax(kn, axis=2)
    fq_first = ft[:, :, ::blk]
    fk_last = ft[:, :, blk - 1::blk]
    ub = qn[:, :, :, None] * kn_upto[:, :, None, :] + fq_first[:, :, :, None] - fk_last[:, :, None, :]
    ub = jnp.pad(ub, ((0, 0), (0, 0), (0, 0), (0, LANES - nb)), constant_values=-jnp.inf)
    return ub.reshape(b, FOX_PAIRS, 2, nb, LANES).transpose(0, 1, 3, 2, 4)


def _fox_attn_kernel(qt_ref, kaug_ref, vt_ref, ft_ref, ub_ref, ot_ref, qaug, m_s, acc, s_buf):
    tq = qt_ref.shape[2]
    tk = vt_ref.shape[3]
    i = pl.program_id(2)
    zeros = lambda r: jnp.zeros((r, tq), BF16)
    minus = jnp.full((16, tq), -1.0, BF16)
    row = lax.broadcasted_iota(jnp.int32, (16, tq), 0)
    for h in range(2):
        qaug[h, 64 * h:64 * (h + 1)] = qt_ref[0, 64 * h:64 * (h + 1), :]
        qaug[h, 64 * (1 - h):64 * (2 - h)] = zeros(64)
        qaug[h, LANES + 16 * h:LANES + 16 * (h + 1)] = minus
        qaug[h, LANES + 16 * (1 - h):LANES + 16 * (2 - h)] = zeros(16)
        p1, p2, p3 = _split3(ft_ref[0, 0, h:h + 1, :])
        blk = jnp.where(row == 0, p1.astype(F32),
                        jnp.where(row == 1, p2.astype(F32), jnp.where(row == 2, p3.astype(F32), 0.0)))
        qaug[h, LANES + 32:LANES + 48] = blk.astype(BF16)
        qaug[h, LANES + 48:2 * LANES] = zeros(LANES - 48)
    m_s[...] = jnp.full(m_s.shape, -jnp.inf, F32)
    acc[...] = jnp.zeros(acc.shape, F32)

    chains = [(h, c) for h in range(2) for c in range(tq // ATTN_STRIP)]

    def qk(n, j, slot):
        h, c = chains[n]
        qa = qaug[h, :, c * ATTN_STRIP:(c + 1) * ATTN_STRIP]
        for r in range(tk // ATTN_KEYS):
            kj = kaug_ref[0, pl.ds(pl.multiple_of(j * tk + r * ATTN_KEYS, ATTN_KEYS), ATTN_KEYS), :]
            s_buf[slot, n, r * ATTN_KEYS:(r + 1) * ATTN_KEYS, :] = _dot(kj, qa)

    def softmax(n, slot, masked):
        h, c = chains[n]
        qs = slice(c * ATTN_STRIP, (c + 1) * ATTN_STRIP)
        st = s_buf[slot, n]
        if masked:
            kpos = lax.broadcasted_iota(jnp.int32, st.shape, 0)
            qpos = lax.broadcasted_iota(jnp.int32, st.shape, 1) + c * ATTN_STRIP
            st = jnp.where(kpos <= qpos, st, -jnp.inf)
        m_prev = m_s[h, :, qs]
        m_new = jnp.maximum(m_prev, jnp.max(st, axis=0, keepdims=True))
        alpha = jnp.exp(m_prev - m_new)
        p = jnp.exp(st - m_new)
        m_s[h, :, qs] = m_new
        return p.astype(BF16), alpha

    def pv(n, j, p, alpha):
        h, c = chains[n]
        qs = slice(c * ATTN_STRIP, (c + 1) * ATTN_STRIP)
        acc[h, :, qs] = alpha * acc[h, :, qs] + _dot(vt_ref[0, j, FOX_VROWS * h:FOX_VROWS * (h + 1), :], p)

    def block(j, slot, masked, prefetch=True):
        jn = jnp.maximum(j - 1, 0)
        pa = {}
        if prefetch:
            for n in range(len(chains)):
                qk(n, jn, 1 - slot)
        for n in range(len(chains)):
            pa[n] = softmax(n, slot, masked)
            if n >= 1:
                pv(n - 1, j, *pa.pop(n - 1))
        pv(len(chains) - 1, j, *pa.pop(len(chains) - 1))

    for n in range(len(chains)):
        qk(n, i, 0)
    block(i, 0, True)

    lane = lax.broadcasted_iota(jnp.int32, (1, LANES), 1)
    live = lane < 0
    for h in range(2):
        m_low = jnp.min(m_s[h], axis=1, keepdims=True)
        live = live | (ub_ref[0, 0, 0, h:h + 1, :] >= m_low - DEAD_LOGIT_GAP)
    n_live = jnp.sum(jnp.where(live & (lane < i), 1.0, 0.0)).astype(jnp.int32)

    def pair(u, carry):
        j = i - 1 - 2 * u
        block(j, 1, False)
        block(j - 1, 0, False)
        return carry

    n_pairs = n_live // 2
    even = n_live % 2 == 0
    lax.fori_loop(0, jnp.where(even, jnp.maximum(n_pairs - 1, 0), n_pairs), pair, 0)

    @pl.when(even & (n_live >= 2))
    def _():
        block(i - n_live + 1, 1, False)
        block(i - n_live, 0, False, prefetch=False)

    @pl.when(n_live % 2 == 1)
    def _():
        block(i - n_live, 1, False, prefetch=False)

    for h in range(2):
        ot_ref[0, 64 * h:64 * (h + 1), :] = (acc[h, :FOX_HD] / acc[h, FOX_HD:FOX_HD + 1]).astype(BF16)


def fox_attention(qt, kaug, vt, ft, ub):
    b, d, t = qt.shape
    blk = vt.shape[3]
    assert t // blk <= LANES
    ft = ft.reshape(b, FOX_PAIRS, 2, t)
    return pl.pallas_call(
        _fox_attn_kernel,
        grid=(b, FOX_PAIRS, t // blk),
        in_specs=[pl.BlockSpec((1, LANES, blk), lambda bi, p, i: (bi, p, i)),
                  pl.BlockSpec((1, t, 2 * LANES), lambda bi, p, i: (bi, 0, p)),
                  pl.BlockSpec((1, t // blk, 2 * FOX_VROWS, blk), lambda bi, p, i: (bi, 0, p, 0)),
                  pl.BlockSpec((1, 1, 2, blk), lambda bi, p, i: (bi, p, 0, i)),
                  pl.BlockSpec((1, 1, 1, 2, LANES), lambda bi, p, i: (bi, p, i, 0, 0))],
        out_specs=pl.BlockSpec((1, LANES, blk), lambda bi, p, i: (bi, p, i)),
        out_shape=jax.ShapeDtypeStruct((b, d, t), BF16),
        scratch_shapes=[pltpu.VMEM((2, 2 * LANES, blk), BF16), pltpu.VMEM((2, 1, blk), F32),
                        pltpu.VMEM((2, FOX_VROWS, blk), F32),
                        pltpu.VMEM((2, 2 * blk // ATTN_STRIP, blk, ATTN_STRIP), F32)],
        compiler_params=_cparams("parallel", "parallel", "arbitrary"),
        name="fox_attention",
    )(qt, kaug, vt, ft, ub)


def _out_proj_t_kernel(h_ref, ot_ref, w_ref, o_ref):
    o_ref[0] = h_ref[0] + pl.dot(ot_ref[0], w_ref[...], trans_a=True)


def out_proj_t(h, ot, w):
    b, t, d = h.shape
    tm = min(TOKEN_TILE, t)
    tile = pl.BlockSpec((1, tm, d), lambda bi, ti: (bi, ti, 0))
    return pl.pallas_call(
        _out_proj_t_kernel,
        grid=(b, t // tm),
        in_specs=[tile, pl.BlockSpec((1, d, tm), lambda bi, ti: (bi, 0, ti)), _resident((d, d))],
        out_specs=tile,
        out_shape=jax.ShapeDtypeStruct((b, t, d), F32),
        compiler_params=_cparams("parallel", "parallel"),
        name="fox_out_proj",
    )(h, ot, w.astype(BF16))


def _fox_sample_kernel(x_ref, g_ref, wq_ref, wk_ref, wv_ref, wf_ref, bf_ref, wo_ref, ck_ref, cv_ref, clf_ref,
                       h_out, k_out, v_out, lf_out):
    s = x_ref.shape[1]
    pl_len = ck_ref.shape[1]
    x = x_ref[0]
    xb = _rms(x, g_ref[...]).astype(BF16)
    q = _dot(xb, wq_ref[...]) * (FOX_HD ** -0.5)
    k = _dot(xb, wk_ref[...])
    v = _dot(xb, wv_ref[...])
    lf = _log_sigmoid(_dot(xb, wf_ref[...]) + bf_ref[...])
    k_out[0] = k
    v_out[0] = v
    lf_out[0] = lf[:, :FOX_HEADS]

    rows = s * FOX_HEADS
    rh = lax.broadcasted_iota(jnp.int32, (rows, D_MODEL), 0) % FOX_HEADS
    lh = lax.broadcasted_iota(jnp.int32, (rows, D_MODEL), 1) // FOX_HD
    head_lanes = rh == lh
    qbd = jnp.where(head_lanes, _rep_rows(q, FOX_HEADS), 0.0).astype(BF16)

    cblk = 256
    carry = jnp.zeros((1, LANES), F32)
    tri = _tril(cblk)
    f_parts = []
    for c in range(pl_len // cblk):
        fc = _dot3_left(tri, clf_ref[0, c * cblk:(c + 1) * cblk, :]) + carry
        carry = fc[cblk - 1:cblk, :]
        f_parts.append(fc)
    f_cache = jnp.concatenate(f_parts, axis=0)
    pad_rows = lambda a: jnp.concatenate([a, jnp.zeros((LANES - s, a.shape[1]), a.dtype)], axis=0)
    f_new = _dot3_left(_tril(LANES), pad_rows(lf)) + carry

    lane = lax.broadcasted_iota(jnp.int32, (rows, LANES), 1)
    row_head = lax.broadcasted_iota(jnp.int32, (rows, LANES), 0) % FOX_HEADS
    fq = jnp.sum(jnp.where(lane == row_head, _rep_rows(f_new[:s], FOX_HEADS), 0.0), axis=-1, keepdims=True)
    fk_cache = _tile_rows(f_cache.T[:FOX_HEADS, :], s)
    fk_new = _tile_rows(f_new.T[:FOX_HEADS, :], s)

    kc = 512
    qk = [_dot_nt(qbd, ck_ref[0, c * kc:(c + 1) * kc, :].astype(BF16)) for c in range(pl_len // kc)]
    s_cache = jnp.concatenate(qk, axis=1) + fq - fk_cache
    s_new = _dot_nt(qbd, pad_rows(k).astype(BF16)) + fq - fk_new
    qi = lax.broadcasted_iota(jnp.int32, (rows, LANES), 0) // FOX_HEADS
    kj = lax.broadcasted_iota(jnp.int32, (rows, LANES), 1)
    s_new = jnp.where(kj <= qi, s_new, -jnp.inf)
    m = jnp.maximum(jnp.max(s_cache, axis=-1, keepdims=True), jnp.max(s_new, axis=-1, keepdims=True))
    p_cache = jnp.exp(s_cache - m)
    p_new = jnp.exp(s_new - m)
    den = jnp.sum(p_cache, axis=-1, keepdims=True) + jnp.sum(p_new, axis=-1, keepdims=True)
    pv = _dot(p_new.astype(BF16), pad_rows(v).astype(BF16))
    pb = p_cache.astype(BF16)
    for c in range(pl_len // kc):
        pv = pv + _dot(pb[:, c * kc:(c + 1) * kc], cv_ref[0, c * kc:(c + 1) * kc, :].astype(BF16))
    o_heads = jnp.where(head_lanes, pv / den, 0.0).astype(BF16)
    pick = (lax.broadcasted_iota(jnp.int32, (s, rows), 1) // FOX_HEADS
            == lax.broadcasted_iota(jnp.int32, (s, rows), 0))
    o = _dot(jnp.where(pick, 1.0, 0.0).astype(BF16), o_heads)
    h_out[0] = x + _dot(o.astype(BF16), wo_ref[...])


def fox_sample(x, g, wq, wk, wv, wf, bf, wo, ck, cv, clf):
    b, s, d = x.shape
    p = ck.shape[1]
    wf_pad = jnp.zeros((d, LANES), F32).at[:, :FOX_HEADS].set(wf).astype(BF16)
    bf_pad = jnp.zeros((1, LANES), F32).at[0, :FOX_HEADS].set(bf)
    clf_pad = jnp.zeros((b, p, LANES), F32).at[:, :, :FOX_HEADS].set(clf)
    seq = lambda n, w: pl.BlockSpec((1, n, w), lambda bi: (bi, 0, 0))
    return pl.pallas_call(
        _fox_sample_kernel,
        grid=(b,),
        in_specs=[seq(s, d), _resident((1, d)), _resident((d, d)), _resident((d, d)), _resident((d, d)),
                  _resident((d, LANES)), _resident((1, LANES)), _resident((d, d)),
                  seq(p, d), seq(p, d), seq(p, LANES)],
        out_specs=[seq(s, d), seq(s, d), seq(s, d), seq(s, FOX_HEADS)],
        out_shape=[jax.ShapeDtypeStruct((b, s, d), F32)] * 3 + [jax.ShapeDtypeStruct((b, s, FOX_HEADS), F32)],
        compiler_params=_cparams("parallel"),
        name="fox_sample",
    )(x, g, wq.astype(BF16), wk.astype(BF16), wv.astype(BF16), wf_pad, bf_pad, wo.astype(BF16),
      ck.reshape(b, p, d), cv.reshape(b, p, d), clf_pad)


def _halo_specs(b, t, tm, d, halo):
    main = pl.BlockSpec((1, tm, d), lambda bi, ti: (bi, ti, 0))
    if t == tm:
        return [main]
    per = tm // halo
    return [main, pl.BlockSpec((1, halo, d), lambda bi, ti: (bi, jnp.maximum(ti * per - 1, 0), 0))]


def _conf_kernel(*refs, has_halo):
    if has_halo:
        x_ref, xh_ref, *refs = refs
    else:
        x_ref, *refs = refs
    (hist_ref, g_ref, w1_ref, b1_ref, wdw_ref, bdw_ref, gln_ref, bln_ref, w2_ref, b2_ref,
     o_ref, hs_ref, uext, ushift) = refs
    tm = x_ref.shape[1]

    def glu(rows):
        ag = _dot(_rms(rows, g_ref[...]).astype(BF16), w1_ref[...]) + b1_ref[...]
        return ag[:, :D_MODEL] * _sigmoid(ag[:, D_MODEL:])

    x = x_ref[0]
    if has_halo:
        prev = jnp.where(pl.program_id(1) == 0, hist_ref[0], glu(xh_ref[0]))
    else:
        prev = hist_ref[0]
    uext[0:CONF_HALO] = prev
    uext[CONF_HALO:CONF_HALO + tm] = glu(x)
    uext[CONF_HALO + tm:] = jnp.zeros((8, D_MODEL), F32)
    off = CONF_HALO - (CONF_K - 1)
    y = jnp.broadcast_to(bdw_ref[...], (tm, D_MODEL))
    for s in range(8):
        ushift[...] = uext[s:s + tm + CONF_HALO, :]
        for k in range(CONF_K):
            if (off + k) % 8 == s:
                a = off + k - s
                y = y + wdw_ref[k:k + 1, :] * ushift[a:a + tm, :]
    hs_ref[0] = uext[tm + off:tm + CONF_HALO, :]
    mu = jnp.mean(y, axis=-1, keepdims=True)
    yc = y - mu
    var = jnp.mean(yc * yc, axis=-1, keepdims=True)
    v = _silu(yc * lax.rsqrt(var + EPS) * gln_ref[...] + bln_ref[...])
    o_ref[0] = x + _dot(v.astype(BF16), w2_ref[...]) + b2_ref[...]


def conformer(x, hist, g, w1, b1, wdw, bdw, gln, bln, w2, b2):
    b, t, d = x.shape
    tm = min(TOKEN_TILE, t)
    xs = _halo_specs(b, t, tm, d, CONF_HALO)
    hist_pad = jnp.pad(hist, ((0, 0), (CONF_HALO - (CONF_K - 1), 0), (0, 0)))
    per_b = lambda n: pl.BlockSpec((1, n, d), lambda bi, ti: (bi, 0, 0))
    return pl.pallas_call(
        functools.partial(_conf_kernel, has_halo=len(xs) == 2),
        grid=(b, t // tm),
        in_specs=xs + [per_b(CONF_HALO), _resident((1, d)), _resident((d, 2 * d)), _resident((1, 2 * d)),
                       _resident((CONF_K, d)), _resident((1, d)), _resident((1, d)), _resident((1, d)),
                       _resident((d, d)), _resident((1, d))],
        out_specs=[pl.BlockSpec((1, tm, d), lambda bi, ti: (bi, ti, 0)), per_b(CONF_K - 1)],
        out_shape=[jax.ShapeDtypeStruct((b, t, d), F32), jax.ShapeDtypeStruct((b, CONF_K - 1, d), F32)],
        scratch_shapes=[pltpu.VMEM((CONF_HALO + tm + 8, d), F32),
                        pltpu.VMEM((CONF_HALO + tm, d), F32)],
        compiler_params=_cparams("parallel", "arbitrary"),
        name="conformer",
    )(*([x] * len(xs)), hist_pad, g, w1.astype(BF16), b1.reshape(1, -1), wdw, bdw.reshape(1, -1),
      gln.reshape(1, -1), bln.reshape(1, -1), w2.astype(BF16), b2.reshape(1, -1))


def _pool_kernel(*refs, has_halo, pos0):
    if has_halo:
        x_ref, xh_ref, *refs = refs
    else:
        x_ref, *refs = refs
    hist_ref, g_ref, w_ref, b_ref, scale_ref, o_ref, hs_ref, ext = refs
    tm = x_ref.shape[1]
    i = pl.program_id(1)
    x = x_ref[0]
    xn = _rms(x, g_ref[...])
    if has_halo:
        prev = jnp.where(i == 0, hist_ref[0], _rms(xh_ref[0], g_ref[...]))
    else:
        prev = hist_ref[0]
    ext[0:POOL_HALO] = prev
    ext[POOL_HALO:] = xn
    hs_ref[0] = ext[tm + POOL_HALO - POOL_HIST:tm + POOL_HALO, :]
    pos = pos0 + i * tm + lax.broadcasted_iota(jnp.int32, (tm, 1), 0)
    ys = []
    for gi, wlen in enumerate(POOL_WINDOWS):
        lanes = slice(gi * POOL_GROUP, (gi + 1) * POOL_GROUP)
        s = ext[POOL_HALO:POOL_HALO + tm, lanes]
        for k in range(1, wlen):
            s = s + ext[POOL_HALO - k:POOL_HALO - k + tm, lanes]
        cnt = jnp.minimum(pos + 1, wlen).astype(F32)
        dg = s / cnt - xn[:, lanes]
        ys.append(_dot(dg.astype(BF16), w_ref[gi]))
    y = (jnp.concatenate(ys, axis=1) + b_ref[...]) * scale_ref[...]
    o_ref[0] = x + y


def pool_mixer(x, hist, pos0, g, w, bias, scale):
    b, t, d = x.shape
    tm = min(TOKEN_TILE, t)
    xs = _halo_specs(b, t, tm, d, POOL_HALO)
    hist_pad = jnp.pad(hist, ((0, 0), (POOL_HALO - POOL_HIST, 0), (0, 0)))
    per_b = lambda n: pl.BlockSpec((1, n, d), lambda bi, ti: (bi, 0, 0))
    ng = len(POOL_WINDOWS)
    return pl.pallas_call(
        functools.partial(_pool_kernel, has_halo=len(xs) == 2, pos0=pos0),
        grid=(b, t // tm),
        in_specs=xs + [per_b(POOL_HALO), _resident((1, d)), _resident((ng, POOL_GROUP, POOL_GROUP)),
                       _resident((1, d)), _resident((1, d))],
        out_specs=[pl.BlockSpec((1, tm, d), lambda bi, ti: (bi, ti, 0)), per_b(POOL_HIST)],
        out_shape=[jax.ShapeDtypeStruct((b, t, d), F32), jax.ShapeDtypeStruct((b, POOL_HIST, d), F32)],
        scratch_shapes=[pltpu.VMEM((POOL_HALO + tm, d), F32)],
        compiler_params=_cparams("parallel", "arbitrary"),
        name="pool_mixer",
    )(*([x] * len(xs)), hist_pad, g, w.astype(BF16), bias.reshape(1, -1), scale.reshape(1, -1))


def _ssd_kernel(*refs, has_halo, valid):
    if has_halo:
        x_ref, xh_ref, *refs = refs
    else:
        x_ref, *refs = refs
    (chist_ref, h0_ref, g_ref, wz_ref, wxbc_ref, wdt_ref, wconv_ref, bconv_ref, dtb_ref, a_ref, dskip_ref,
     gn_ref, wout_ref, exp_ref, expt_ref, o_ref, cs_out_ref, h_out_ref, ext, hstate) = refs
    L = x_ref.shape[1]
    i = pl.program_id(1)
    gw = SSD_HPG * SSD_HD
    gn = SSD_GROUPS * SSD_N

    @pl.when(i == 0)
    def _():
        hstate[...] = h0_ref[0]

    x = x_ref[0]
    xb = _rms(x, g_ref[...]).astype(BF16)
    z = _dot(xb, wz_ref[...])
    dtr = _dot(xb, wdt_ref[...])
    if has_halo:
        prev = jnp.where(i == 0, chist_ref[0], _dot(_rms(xh_ref[0], g_ref[...]).astype(BF16), wxbc_ref[...]))
    else:
        prev = chist_ref[0]
    ext[0:SSD_HALO] = prev
    ext[SSD_HALO:] = _dot(xb, wxbc_ref[...])
    off = SSD_HALO - (SSD_CONV_K - 1)
    y = jnp.broadcast_to(bconv_ref[...], (L, SSD_CONV_DIM))
    for k in range(SSD_CONV_K):
        y = y + wconv_ref[k:k + 1, :] * ext[off + k:off + k + L, :]
    cs_out_ref[0] = ext[valid + off:valid + SSD_HALO, :]
    xbc = _silu(y)
    xs = xbc[:, :SSD_INNER]
    bm = xbc[:, SSD_INNER:SSD_INNER + gn].astype(BF16)
    cm = xbc[:, SSD_INNER + gn:].astype(BF16)

    head_lane = lax.broadcasted_iota(jnp.int32, (L, LANES), 1) < SSD_HEADS
    live = head_lane
    if valid < L:
        live = live & (lax.broadcasted_iota(jnp.int32, (L, LANES), 0) < valid)
    dt = jnp.where(live, _softplus(dtr + dtb_ref[...]), 0.0)
    cs = _dot3_left(_tril(L), dt * a_ref[...])
    cst = cs.T
    dtt = dt.T
    to_end = jnp.exp(cs[L - 1:L, :] - cs) * dt
    td_x = _dot(to_end.astype(BF16), exp_ref[...])
    ecs_x = _dot3(jnp.exp(cs), exp_ref[...])
    chunk_decay = jnp.broadcast_to(jnp.exp(cst[:, L - 1:L]), (LANES, LANES))
    xs_b = xs.astype(BF16)
    xtd = xs * td_x
    r = lax.broadcasted_iota(jnp.int32, (L, L), 0)
    c = lax.broadcasted_iota(jnp.int32, (L, L), 1)
    causal = c <= r
    low_half = lax.broadcasted_iota(jnp.int32, (L, 2 * SSD_HD), 1) < SSD_HD

    y_groups = []
    for g in range(SSD_GROUPS):
        cg = cm[:, g * SSD_N:(g + 1) * SSD_N]
        bg = bm[:, g * SSD_N:(g + 1) * SSD_N]
        cb = _dot_nt(cg, bg)
        hg = hstate[g]
        y_off = _dot_nt(cg, hg.astype(BF16)) * ecs_x[:, g * gw:(g + 1) * gw]
        pairs = []
        for pr in range(SSD_HPG // 2):
            lo = g * gw + pr * 2 * SSD_HD
            xpair = xs_b[:, lo:lo + 2 * SSD_HD]
            acc = None
            for hh in range(2):
                e = g * SSD_HPG + 2 * pr + hh
                seg = cs[:, e:e + 1] - cst[e:e + 1, :]
                w = cb * jnp.exp(jnp.where(causal, seg, -jnp.inf)) * dtt[e:e + 1, :]
                xm = jnp.where(low_half == (hh == 0), xpair, jnp.zeros_like(xpair))
                term = _dot(w.astype(BF16), xm)
                acc = term if acc is None else acc + term
            pairs.append(acc)
        y_groups.append(jnp.concatenate(pairs, axis=1) + y_off)
        s_g = _dot(xtd[:, g * gw:(g + 1) * gw].T.astype(BF16), bg)
        hstate[g] = hg * _dot3_left(expt_ref[g], chunk_decay) + s_g

    ys = (jnp.concatenate(y_groups, axis=1) + dskip_ref[...] * xs) * _silu(z)
    normed = []
    for g in range(SSD_GROUPS):
        yg = ys[:, g * gw:(g + 1) * gw]
        normed.append(yg * lax.rsqrt(jnp.mean(yg * yg, axis=-1, keepdims=True) + EPS))
    yn = jnp.concatenate(normed, axis=1) * gn_ref[...]
    o_ref[0] = x + _dot(yn.astype(BF16), wout_ref[...])

    @pl.when(i == pl.num_programs(1) - 1)
    def _():
        h_out_ref[0] = hstate[...]


def _ssd_expand_constants():
    ch = np.arange(SSD_INNER)
    exp = np.zeros((LANES, SSD_INNER), np.float32)
    exp[ch // SSD_HD, ch] = 1.0
    gw = SSD_HPG * SSD_HD
    rows = np.arange(gw)
    expt = np.zeros((SSD_GROUPS, gw, LANES), np.float32)
    for g in range(SSD_GROUPS):
        expt[g, rows, g * SSD_HPG + rows // SSD_HD] = 1.0
    return jnp.asarray(exp, BF16), jnp.asarray(expt, BF16)


def ssd_mixer(x, conv_hist, h0, g, w_in, w_conv, b_conv, dt_bias, a_log, d_skip, g_norm, w_out):
    b, t, d = x.shape
    valid = t
    if t < LANES:
        x = jnp.pad(x, ((0, 0), (0, LANES - t), (0, 0)))
    tp = x.shape[1]
    L = min(SSD_CHUNK, tp)
    valid = L if tp > L else valid
    xs = _halo_specs(b, tp, L, d, SSD_HALO)
    gw = SSD_HPG * SSD_HD
    hist_pad = jnp.pad(conv_hist, ((0, 0), (SSD_HALO - (SSD_CONV_K - 1), 0), (0, 0)))
    wz = w_in[:, :SSD_INNER].astype(BF16)
    wxbc = w_in[:, SSD_INNER:SSD_INNER + SSD_CONV_DIM].astype(BF16)
    pad_heads = lambda v: jnp.zeros((v.shape[0], LANES), F32).at[:, :SSD_HEADS].set(v)
    wdt = pad_heads(w_in[:, SSD_INNER + SSD_CONV_DIM:]).astype(BF16)
    dtb = pad_heads(dt_bias.reshape(1, -1))
    a = pad_heads(-jnp.exp(a_log.astype(F32)).reshape(1, -1))
    dskip = jnp.repeat(d_skip, SSD_HD).reshape(1, -1)
    exp, expt = _ssd_expand_constants()
    per_b3 = lambda n, w: pl.BlockSpec((1, n, w), lambda bi, ti: (bi, 0, 0))
    state = pl.BlockSpec((1, SSD_GROUPS, gw, SSD_N), lambda bi, ti: (bi, 0, 0, 0))
    out, conv_new, h_new = pl.pallas_call(
        functools.partial(_ssd_kernel, has_halo=len(xs) == 2, valid=valid),
        grid=(b, tp // L),
        in_specs=xs + [per_b3(SSD_HALO, SSD_CONV_DIM), state, _resident((1, d)),
                       _resident((d, SSD_INNER)), _resident((d, SSD_CONV_DIM)), _resident((d, LANES)),
                       _resident((SSD_CONV_K, SSD_CONV_DIM)), _resident((1, SSD_CONV_DIM)),
                       _resident((1, LANES)), _resident((1, LANES)), _resident((1, SSD_INNER)),
                       _resident((1, SSD_INNER)), _resident((SSD_INNER, d)),
                       _resident((LANES, SSD_INNER)), _resident((SSD_GROUPS, gw, LANES))],
        out_specs=[pl.BlockSpec((1, L, d), lambda bi, ti: (bi, ti, 0)),
                   per_b3(SSD_CONV_K - 1, SSD_CONV_DIM), state],
        out_shape=[jax.ShapeDtypeStruct((b, tp, d), F32),
                   jax.ShapeDtypeStruct((b, SSD_CONV_K - 1, SSD_CONV_DIM), F32),
                   jax.ShapeDtypeStruct((b, SSD_GROUPS, gw, SSD_N), F32)],
        scratch_shapes=[pltpu.VMEM((SSD_HALO + L, SSD_CONV_DIM), F32), pltpu.VMEM((SSD_GROUPS, gw, SSD_N), F32)],
        compiler_params=_cparams("parallel", "arbitrary"),
        name="ssd_mixer",
    )(*([x] * len(xs)), hist_pad, h0.reshape(b, SSD_GROUPS, gw, SSD_N), g, wz, wxbc, wdt, w_conv,
      b_conv.reshape(1, -1), dtb, a, dskip, g_norm.reshape(1, -1), w_out.astype(BF16), exp, expt)
    return out[:, :t], conv_new, h_new.reshape(b, SSD_HEADS, SSD_HD, SSD_N)


def kernel(x_prompt, x_sample, cache_fox_k, cache_fox_v, cache_fox_logf, state_conf_conv, state_ssd_conv, state_ssd, state_pool, p_prompt, p_sample, g_mix, g_ffn, g_ple, g_final, fox_w_q, fox_w_k, fox_w_v, fox_w_f, fox_b_f, fox_w_o, conf_w_pw1, conf_b_pw1, conf_w_dw, conf_b_dw, conf_g_ln, conf_b_ln, conf_w_pw2, conf_b_pw2, ssd_w_in, ssd_w_conv, ssd_b_conv, ssd_dt_bias, ssd_a_log, ssd_d, ssd_g_norm, ssd_w_out, pool_w, pool_b, pool_scale, ffn_w_gate, ffn_w_up, ffn_w_down, moe_w_router, moe_w_gate, moe_w_up, moe_w_down, ple_w_up, ple_w_gate):
    b, t, d = x_prompt.shape
    bs, s, _ = x_sample.shape
    past = cache_fox_k.shape[1]
    row = lambda v: v.reshape(1, -1)
    bf = lambda w: w.astype(BF16)
    flat = lambda a: a.reshape(-1, a.shape[-1])
    hp, hs = x_prompt, x_sample

    def dense_ffn(h, p, i):
        j = i // 2
        out = ffn_ple(flat(h), flat(p[i]), row(g_ffn[i]), bf(ffn_w_gate[j]), bf(ffn_w_up[j]), bf(ffn_w_down[j]),
                      row(g_ple[i]), bf(ple_w_gate[i]), bf(ple_w_up[i]))
        return out.reshape(h.shape)

    def moe_ffn(h, p, i, final_norm):
        j = i // 2
        wr = jnp.zeros((d, LANES), F32).at[:, :N_EXPERTS].set(moe_w_router[j]).astype(BF16)
        out = moe_ple(flat(h), flat(p[i]), row(g_ffn[i]), wr, bf(moe_w_gate[j]), bf(moe_w_up[j]),
                      bf(moe_w_down[j]), row(g_ple[i]), bf(ple_w_gate[i]), bf(ple_w_up[i]), row(g_final),
                      final_norm)
        return out.reshape(h.shape)

    fox_k_p, fox_v_p, fox_lf_p, qt, vt, kaug, ft, qn2, kn2 = fox_proj(hp, row(g_mix[0]), fox_w_q, fox_w_k,
                                                                      fox_w_v, fox_w_f, fox_b_f)
    ub = fox_dead_block_bounds(qn2, kn2, ft, vt.shape[3])
    hp = proj_ffn_ple(hp, fox_attention(qt, kaug, vt, ft, ub), fox_w_o, p_prompt[0], row(g_ffn[0]),
                      bf(ffn_w_gate[0]), bf(ffn_w_up[0]), bf(ffn_w_down[0]), row(g_ple[0]), bf(ple_w_gate[0]),
                      bf(ple_w_up[0]))
    hs, fox_k_s, fox_v_s, fox_lf_s = fox_sample(hs, row(g_mix[0]), fox_w_q, fox_w_k, fox_w_v, fox_w_f, fox_b_f,
                                                fox_w_o, cache_fox_k, cache_fox_v, cache_fox_logf)
    hs = dense_ffn(hs, p_sample, 0)

    conf_args = (row(g_mix[1]), conf_w_pw1, conf_b_pw1, conf_w_dw, conf_b_dw, conf_g_ln, conf_b_ln,
                 conf_w_pw2, conf_b_pw2)
    hp, conf_p = conformer(hp, jnp.zeros((b, CONF_K - 1, d), F32), *conf_args)
    hs, conf_s = conformer(hs, state_conf_conv, *conf_args)
    hp = moe_ffn(hp, p_prompt, 1, False)
    hs = moe_ffn(hs, p_sample, 1, False)

    ssd_args = (row(g_mix[2]), ssd_w_in, ssd_w_conv, ssd_b_conv, ssd_dt_bias, ssd_a_log, ssd_d, ssd_g_norm,
                ssd_w_out)
    hp, ssdc_p, ssdh_p = ssd_mixer(hp, jnp.zeros((b, SSD_CONV_K - 1, SSD_CONV_DIM), F32),
                                   jnp.zeros((b, SSD_HEADS, SSD_HD, SSD_N), F32), *ssd_args)
    hs, ssdc_s, ssdh_s = ssd_mixer(hs, state_ssd_conv, state_ssd, *ssd_args)
    hp = dense_ffn(hp, p_prompt, 2)
    hs = dense_ffn(hs, p_sample, 2)

    pool_args = (row(g_mix[3]), pool_w, pool_b, pool_scale)
    hp, pool_p = pool_mixer(hp, jnp.zeros((b, POOL_HIST, d), F32), 0, *pool_args)
    hs, pool_s = pool_mixer(hs, state_pool, past, *pool_args)
    y_prompt = moe_ffn(hp, p_prompt, 3, True)
    y_sample = moe_ffn(hs, p_sample, 3, True)

    heads = lambda a: a.reshape(a.shape[0], a.shape[1], FOX_HEADS, FOX_HD)
    return (y_prompt, y_sample, heads(fox_k_p), heads(fox_v_p), fox_lf_p, heads(fox_k_s), heads(fox_v_s), fox_lf_s,
            conf_p, conf_s, ssdc_p, ssdc_s, ssdh_p, ssdh_s, pool_p, pool_s)
```
